```python
import jax, jax.numpy as jnp
from jax import lax
import numpy as np

D_MODEL = 1024
BATCH = 4
SEQ = 4096
DEPTH = 2

CHUNK = 64
EPS = 1e-6
HEAD_DV = 64
RWKV_WIDTH = D_MODEL // 2
RWKV_HEADS = RWKV_WIDTH // HEAD_DV
RWKV_N = HEAD_DV
RWKV_DECAY_LORA = 64
RWKV_AAA_LORA = 64
RWKV_MV_LORA = 32
RWKV_GATE_LORA = 128
RWKV_GN_EPS = 64e-5
GLA_WIDTH = D_MODEL // 4
GLA_HEADS = GLA_WIDTH // HEAD_DV
GLA_DK = HEAD_DV // 2
GLA_GATE_LORA = 16
GLA_GATE_TAU = 16.0
RET_WIDTH = D_MODEL - RWKV_WIDTH - GLA_WIDTH
RET_HEADS = RET_WIDTH // HEAD_DV
RET_DK = HEAD_DV // 2
ROPE_BASE = 10000.0
MIX_WIDTH = RWKV_WIDTH + GLA_WIDTH + RET_WIDTH
IN_SIZES = (RWKV_WIDTH, RWKV_WIDTH, RWKV_WIDTH,
            GLA_HEADS * GLA_DK, GLA_HEADS * GLA_DK, GLA_WIDTH, GLA_WIDTH,
            RET_HEADS * RET_DK, RET_HEADS * RET_DK, RET_WIDTH, RET_WIDTH)
D_IN = sum(IN_SIZES)
D_FF = -(-8 * D_MODEL // (3 * 256)) * 256

kernel_name = 'hymba_rwkv7_gla_retnet_adaln_trunk'


def _token_shift(x):
    return jnp.pad(x, ((0, 0), (1, 0), (0, 0)))[:, :-1, :]


def _rmsnorm(x, g):
    x32 = x.astype(jnp.float32)
    y = x32 * lax.rsqrt(jnp.mean(x32 * x32, axis=-1, keepdims=True) + EPS)
    return (y * g.astype(jnp.float32)).astype(x.dtype)


def _head_layernorm(y, eps):
    mu = jnp.mean(y, axis=-1, keepdims=True)
    yc = y - mu
    return yc * lax.rsqrt(jnp.mean(yc * yc, axis=-1, keepdims=True) + eps)


def _rope_tables(seq):
    half = RET_DK // 2
    inv_freq = ROPE_BASE ** (-jnp.arange(half, dtype=jnp.float32) / half)
    ang = jnp.arange(seq, dtype=jnp.float32)[:, None] * inv_freq[None, :]
    return jnp.cos(ang), jnp.sin(ang)


def _rope(x, cos, sin):
    half = x.shape[-1] // 2
    x1, x2 = x[..., :half], x[..., half:]
    c, s = cos[None, :, None, :], sin[None, :, None, :]
    return jnp.concatenate([x1 * c - x2 * s, x2 * c + x1 * s], axis=-1)


def _rwkv7_mixer(h, r, k, v, v_first, mu_rkv, mu_x, w0, w1, w2, a0, a1, a2,
                 g1, g2, k_k, k_a, r_k, ln_g, ln_b, vres):
    f32 = jnp.float32
    h = h.astype(f32)
    B, S, _ = h.shape
    H, N = RWKV_HEADS, RWKV_N
    dh = _token_shift(h) - h
    xw = h + dh * mu_x[0]
    xa = h + dh * mu_x[1]
    xg = h + dh * mu_x[2]
    r = r + (_token_shift(r) - r) * mu_rkv[0]
    k = k + (_token_shift(k) - k) * mu_rkv[1]
    v = v + (_token_shift(v) - v) * mu_rkv[2]
    w = -jax.nn.softplus(-(w0 + jnp.tanh(xw @ w1) @ w2)) - 0.5
    decay = jnp.exp(-jnp.exp(w))
    a = jax.nn.sigmoid(a0 + (xa @ a1) @ a2)
    g = jax.nn.sigmoid(xg @ g1) @ g2
    kk = (k * k_k).reshape(B, S, H, N)
    kk = kk / jnp.maximum(jnp.sqrt(jnp.sum(kk * kk, axis=-1, keepdims=True)), 1e-12)
    k = k * (1.0 + (a - 1.0) * k_a)
    if vres is not None:
        mu_v, v0, v1, v2 = vres
        xv = h + dh * mu_v
        v = v + (v_first - v) * jax.nn.sigmoid(v0 + (xv @ v1) @ v2)

    def to_tm(t):
        return jnp.moveaxis(t.reshape(B, S, H, N), 1, 0)

    xs = (to_tm(r), to_tm(decay), to_tm(k), to_tm(v), jnp.moveaxis(kk, 1, 0), to_tm(a))

    def step(state, inp):
        r_t, w_t, k_t, v_t, kk_t, a_t = inp
        s_kk = jnp.einsum('bhvk,bhk->bhv', state, kk_t)
        state = (state * w_t[:, :, None, :]
                 - s_kk[..., None] * (kk_t * a_t)[:, :, None, :]
                 + v_t[..., None] * k_t[:, :, None, :])
        return state, jnp.einsum('bhvk,bhk->bhv', state, r_t)

    _, y = lax.scan(step, jnp.zeros((B, H, N, N), f32), xs)
    y = jnp.moveaxis(y, 0, 1)
    y = _head_layernorm(y, RWKV_GN_EPS).reshape(B, S, H * N) * ln_g + ln_b
    rh, kh, vh = (t.reshape(B, S, H, N) for t in (r, k, v))
    bonus = jnp.sum(rh * kh * r_k, axis=-1, keepdims=True) * vh
    return (y + bonus.reshape(B, S, H * N)) * g, v


def _gla_mixer(h, q, k, v, gate, a1, a2, ab, ln_g):
    f32 = jnp.float32
    h = h.astype(f32)
    B, S, _ = h.shape
    NC, H, DK, DV = S // CHUNK, GLA_HEADS, GLA_DK, HEAD_DV
    log_a = jax.nn.log_sigmoid((h @ a1) @ a2 + ab) / GLA_GATE_TAU

    def chunked(t, d):
        return t.reshape(B, NC, CHUNK, H, d).transpose(1, 0, 3, 2, 4)

    qc = chunked(q * DK ** -0.5, DK)
    kc = chunked(k, DK)
    vc = chunked(v, DV)
    bc = jnp.cumsum(chunked(log_a, DK), axis=3)
    b_end = bc[:, :, :, -1:, :]
    kv = jnp.einsum('nbhcd,nbhce->nbhde', kc * jnp.exp(b_end - bc), vc)

    def step(state, inp):
        kv_c, dec_c = inp
        return state * dec_c[..., None] + kv_c, state

    _, s_prev = lax.scan(step, jnp.zeros((B, H, DK, DV), f32), (kv, jnp.exp(b_end[:, :, :, 0, :])))
    inter = jnp.einsum('nbhcd,nbhde->nbhce', qc * jnp.exp(bc), s_prev)

    def intra(blk):
        q_, k_, v_, b_ = blk
        dec = jnp.exp(-jnp.abs(b_[:, :, :, None, :] - b_[:, :, None, :, :]))
        att = jnp.sum(q_[:, :, :, None, :] * k_[:, :, None, :, :] * dec, axis=-1)
        return jnp.einsum('bhnm,bhme->bhne', att, v_)

    o = inter + lax.map(intra, (qc, kc, vc, bc))
    o = o.transpose(1, 0, 3, 2, 4).reshape(B, S, H, DV)
    o = o * lax.rsqrt(jnp.mean(o * o, axis=-1, keepdims=True) + EPS) * ln_g
    return o.reshape(B, S, H * DV) * jax.nn.silu(gate)


def _retention_mixer(q, k, v, gate, cos, sin):
    f32 = jnp.float32
    B, S, _ = q.shape
    NC, H, DK, DV = S // CHUNK, RET_HEADS, RET_DK, HEAD_DV
    qh = _rope(q.reshape(B, S, H, DK), cos, sin) * DK ** -0.5
    kh = _rope(k.reshape(B, S, H, DK), cos, sin)
    log_gamma = jnp.log1p(-(2.0 ** (-5.0 - jnp.arange(H, dtype=f32))))
    pos = jnp.arange(CHUNK, dtype=f32)
    intra_dec = jnp.exp(log_gamma[:, None, None] * jnp.abs(pos[:, None] - pos[None, :]))
    k_dec = jnp.exp(log_gamma[None, :] * (CHUNK - 1.0 - pos)[:, None])
    q_dec = jnp.exp(log_gamma[None, :] * (pos + 1.0)[:, None])
    chunk_dec = jnp.exp(log_gamma * CHUNK)
    qc = qh.reshape(B, NC, CHUNK, H, DK)
    kc = kh.reshape(B, NC, CHUNK, H, DK)
    vc = v.reshape(B, NC, CHUNK, H, DV)
    scores = jnp.einsum('bnchd,bnmhd->bnhcm', qc, kc) * intra_dec
    intra = jnp.einsum('bnhcm,bnmhe->bnche', scores, vc)
    kv = jnp.einsum('bnmhd,bnmhe->nbhde', kc * k_dec[:, :, None], vc)

    def step(state, kv_c):
        return state * chunk_dec[:, None, None] + kv_c, state

    _, s_prev = lax.scan(step, jnp.zeros((B, H, DK, DV), f32), kv)
    inter = jnp.einsum('bnchd,nbhde->bnche', qc * q_dec[:, :, None], s_prev)
    o = _head_layernorm((intra + inter).reshape(B, S, H, DV), EPS)
    return o.reshape(B, S, H * DV) * jax.nn.silu(gate)


def setup_inputs(seed: int = 0) -> dict:
    key = jax.random.key(seed)
    ks = iter(jax.random.split(key, 48))
    f32 = jnp.float32

    def nrm(shape, scale):
        return jax.random.normal(next(ks), shape, f32) * scale

    def unif(shape, lo, hi):
        return jax.random.uniform(next(ks), shape, f32, lo, hi)

    L, D, RW = DEPTH, D_MODEL, RWKV_WIDTH
    return {
        'x': nrm((BATCH, SEQ, D), 1.0),
        'c': nrm((BATCH, D), 1.0),
        'ada_w': nrm((L, D, 6 * D), 0.02),
        'ada_b': nrm((L, 6 * D), 0.01),
        'norm1_g': 1.0 + nrm((L, D), 0.02),
        'norm2_g': 1.0 + nrm((L, D), 0.02),
        'w_in': nrm((L, D, D_IN), D ** -0.5),
        'w_out': nrm((L, MIX_WIDTH, D), MIX_WIDTH ** -0.5),
        'rk_mu_rkv': unif((L, 3, RW), 0.0, 1.0),
        'rk_mu_x': unif((L, 3, D), 0.0, 1.0),
        'rk_w0': unif((L, RW), -6.0, 1.0),
        'rk_w1': nrm((L, D, RWKV_DECAY_LORA), D ** -0.5),
        'rk_w2': nrm((L, RWKV_DECAY_LORA, RW), 0.1),
        'rk_a0': nrm((L, RW), 0.5),
        'rk_a1': nrm((L, D, RWKV_AAA_LORA), D ** -0.5),
        'rk_a2': nrm((L, RWKV_AAA_LORA, RW), 0.1),
        'rk_g1': nrm((L, D, RWKV_GATE_LORA), D ** -0.5),
        'rk_g2': nrm((L, RWKV_GATE_LORA, RW), RWKV_GATE_LORA ** -0.5),
        'rk_k_k': 0.85 + nrm((L, RW), 0.05),
        'rk_k_a': 1.0 + nrm((L, RW), 0.05),
        'rk_r_k': nrm((L, RWKV_HEADS, RWKV_N), 0.1),
        'rk_ln_g': 1.0 + nrm((L, RW), 0.02),
        'rk_ln_b': nrm((L, RW), 0.01),
        'rk_mu_v': unif((L - 1, D), 0.0, 1.0),
        'rk_v0': nrm((L - 1, RW), 0.5),
        'rk_v1': nrm((L - 1, D, RWKV_MV_LORA), D ** -0.5),
        'rk_v2': nrm((L - 1, RWKV_MV_LORA, RW), 0.1),
        'gla_a1': nrm((L, D, GLA_GATE_LORA), D ** -0.5),
        'gla_a2': nrm((L, GLA_GATE_LORA, GLA_HEADS * GLA_DK), GLA_GATE_LORA ** -0.5),
        'gla_ab': nrm((L, GLA_HEADS * GLA_DK), 0.1),
        'gla_ln_g': 1.0 + nrm((L, HEAD_DV), 0.02),
        'ffn_w_gate': nrm((L, D, D_FF), D ** -0.5),
        'ffn_w_up': nrm((L, D, D_FF), D ** -0.5),
        'ffn_w_down': nrm((L, D_FF, D), D_FF ** -0.5),
        'norm_f_g': 1.0 + nrm((D,), 0.02),
    }


def reference(x, c, ada_w, ada_b, norm1_g, norm2_g, w_in, w_out,
              rk_mu_rkv, rk_mu_x, rk_w0, rk_w1, rk_w2, rk_a0, rk_a1, rk_a2,
              rk_g1, rk_g2, rk_k_k, rk_k_a, rk_r_k, rk_ln_g, rk_ln_b,
              rk_mu_v, rk_v0, rk_v1, rk_v2,
              gla_a1, gla_a2, gla_ab, gla_ln_g,
              ffn_w_gate, ffn_w_up, ffn_w_down, norm_f_g):
    f32 = jnp.float32
    S = x.shape[1]
    cos, sin = _rope_tables(S)
    splits = np.cumsum(IN_SIZES)[:-1].tolist()
    cond = jax.nn.silu(c)
    v_first = None
    for l in range(DEPTH):
        mod = (cond @ ada_w[l] + ada_b[l])[:, None, :]
        sh1, sc1, gt1, sh2, sc2, gt2 = jnp.split(mod, 6, axis=-1)
        h = _rmsnorm(x, norm1_g[l]) * (1.0 + sc1) + sh1
        (rw_r, rw_k, rw_v, gl_q, gl_k, gl_v, gl_g,
         rt_q, rt_k, rt_v, rt_g) = jnp.split((h @ w_in[l]).astype(f32), splits, axis=-1)
        vres = None if l == 0 else (rk_mu_v[l - 1], rk_v0[l - 1], rk_v1[l - 1], rk_v2[l - 1])
        y_a, v_l = _rwkv7_mixer(h, rw_r, rw_k, rw_v, v_first, rk_mu_rkv[l], rk_mu_x[l],
                                rk_w0[l], rk_w1[l], rk_w2[l], rk_a0[l], rk_a1[l], rk_a2[l],
                                rk_g1[l], rk_g2[l], rk_k_k[l], rk_k_a[l], rk_r_k[l],
                                rk_ln_g[l], rk_ln_b[l], vres)
        if l == 0:
            v_first = v_l
        y_b = _gla_mixer(h, gl_q, gl_k, gl_v, gl_g, gla_a1[l], gla_a2[l], gla_ab[l], gla_ln_g[l])
        y_c = _retention_mixer(rt_q, rt_k, rt_v, rt_g, cos, sin)
        y = jnp.concatenate([y_a, y_b, y_c], axis=-1).astype(x.dtype) @ w_out[l]
        x = x + gt1 * y
        h = _rmsnorm(x, norm2_g[l]) * (1.0 + sc2) + sh2
        x = x + gt2 * ((jax.nn.silu(h @ ffn_w_gate[l]) * (h @ ffn_w_up[l])) @ ffn_w_down[l])
    return _rmsnorm(x, norm_f_g)
```

```python
import functools

import numpy as np
import jax
import jax.numpy as jnp
from jax import lax
from jax.experimental import pallas as pl
from jax.experimental.pallas import tpu as pltpu

F32 = jnp.float32
BF16 = jnp.bfloat16

CHUNK = 64
EPS = 1e-6
HEAD_DV = 64
RWKV_HEADS = 8
RWKV_WIDTH = RWKV_HEADS * HEAD_DV
RWKV_GN_EPS = 64e-5
GLA_HEADS = 4
GLA_DK = 32
GLA_GATE_TAU = 16.0
RET_HEADS = 4
RET_DK = 32
ROPE_BASE = 10000.0
HEADS_PER_GROUP = 4
GROUP_W = HEADS_PER_GROUP * HEAD_DV
VMEM_LIMIT_BYTES = 56 * 1024 * 1024

_GQ, _GK, _GLA, _GV, _GG, _RQ, _RK, _RV, _RG, _GR_W = 0, 128, 256, 384, 640, 896, 1024, 1152, 1408, 1664


def _dot(a, b):
    return jnp.dot(a.astype(BF16), b.astype(BF16), preferred_element_type=F32)


def _dot_nt(a, b):
    return lax.dot_general(a.astype(BF16), b.astype(BF16), (((1,), (1,)), ((), ())),
                           preferred_element_type=F32)


def _dot_tn(a, b):
    return lax.dot_general(a.astype(BF16), b.astype(BF16), (((0,), (0,)), ((), ())),
                           preferred_element_type=F32)


def _split3(x):
    hi = x.astype(BF16)
    r1 = x - hi.astype(F32)
    mid = r1.astype(BF16)
    lo = (r1 - mid.astype(F32)).astype(BF16)
    return hi, mid, lo


def _dot_exact_rhs(x, m):
    hi, mid, lo = _split3(x)
    mb = m.astype(BF16)
    return (jnp.dot(hi, mb, preferred_element_type=F32)
            + jnp.dot(mid, mb, preferred_element_type=F32)
            + jnp.dot(lo, mb, preferred_element_type=F32))


def _dot_exact_lhs(m, x):
    hi, mid, lo = _split3(x)
    mb = m.astype(BF16)
    return (jnp.dot(mb, hi, preferred_element_type=F32)
            + jnp.dot(mb, mid, preferred_element_type=F32)
            + jnp.dot(mb, lo, preferred_element_type=F32))


def _sigmoid(x):
    return 1.0 / (1.0 + jnp.exp(-x))


def _softplus(x):
    return jnp.maximum(x, 0.0) + jnp.log(1.0 + jnp.exp(-jnp.abs(x)))


def _iota(shape, axis):
    return lax.broadcasted_iota(jnp.int32, shape, axis)


def _bd_rows(x, groups):
    c, w = x.shape
    n = w // groups
    t = jnp.concatenate([x] * groups, axis=0)
    keep = (_iota(t.shape, 0) // c) == (_iota(t.shape, 1) // n)
    return jnp.where(keep, t, 0.0)


def _block_mask(rows, cols, rblk, cblk):
    return (_iota((rows, cols), 0) // rblk) == (_iota((rows, cols), 1) // cblk)


def _const_spec(shape):
    nd = len(shape)
    return pl.BlockSpec(shape, lambda *_: (0,) * nd)


def _adaln_kernel(c_ref, w_ref, b_ref, o_ref):
    c = c_ref[...]
    cond = c * _sigmoid(c)
    o_ref[0] = _dot(cond, w_ref[0]) + b_ref[0]


def _adaln(c, ada_w, ada_b):
    L, D, D6 = ada_w.shape
    B = c.shape[0]
    tn = 1536
    return pl.pallas_call(
        _adaln_kernel,
        grid=(L, D6 // tn),
        in_specs=[pl.BlockSpec((B, D), lambda l, j: (0, 0)),
                  pl.BlockSpec((1, D, tn), lambda l, j: (l, 0, j)),
                  pl.BlockSpec((1, 1, tn), lambda l, j: (l, 0, j))],
        out_specs=pl.BlockSpec((1, B, tn), lambda l, j: (l, 0, j)),
        out_shape=jax.ShapeDtypeStruct((L, B, D6), F32),
        compiler_params=pltpu.CompilerParams(
            dimension_semantics=("arbitrary", "arbitrary"), vmem_limit_bytes=VMEM_LIMIT_BYTES),
        name="adaln_mod",
    )(c, ada_w, ada_b.reshape(L, 1, D6))


def _prep_kernel(tiles_per_seq, has_vres, *refs):
    if has_vres:
        (x_ref, xh_ref, mod_ref, n1g_ref, win_ref, mux_ref, w1_ref, w2_ref, a1_ref, a2_ref,
         g1_ref, g2_ref, ga2_ref, gab_ref, vec_ref, obd_ref, cos_ref, sin_ref,
         v1_ref, v2_ref, vf_ref, rwp_ref, gr_ref) = refs
    else:
        (x_ref, xh_ref, mod_ref, n1g_ref, win_ref, mux_ref, w1_ref, w2_ref, a1_ref, a2_ref,
         g1_ref, g2_ref, ga2_ref, gab_ref, vec_ref, obd_ref, cos_ref, sin_ref,
         rwp_ref, gr_ref) = refs
    tm = x_ref.shape[0]
    rw = RWKV_WIDTH
    first = (pl.program_id(0) % tiles_per_seq) == 0

    xe = jnp.concatenate([xh_ref[...], x_ref[...]], axis=0)
    mod = mod_ref[0]
    ms = jnp.mean(xe * xe, axis=-1, keepdims=True)
    he = xe * lax.rsqrt(ms + EPS) * n1g_ref[...]
    he = he * (1.0 + mod[1:2]) + mod[0:1]
    halo = jnp.logical_and(_iota((tm + 8, 1), 0) < 8, first)
    he = jnp.where(halo, 0.0, he)

    proj = _dot(he, win_ref[...])
    p = proj[8:]
    ps = pltpu.roll(proj[:, :3 * rw], 1, 0)[8:]
    h = he[8:]
    dh = pltpu.roll(he, 1, 0)[8:] - h
    mux = mux_ref[...]
    vec = vec_ref[...]

    z = vec[3:4] + _dot(jnp.tanh(_dot(h + dh * mux[0:1], w1_ref[...])), w2_ref[...])
    lw = -jnp.exp(-_softplus(-z) - 0.5)
    a = _sigmoid(vec[4:5] + _dot(_dot(h + dh * mux[1:2], a1_ref[...]), a2_ref[...]))
    g = _dot(_sigmoid(_dot(h + dh * mux[2:3], g1_ref[...])), g2_ref[...])

    r = p[:, 0:rw]
    r = r + (ps[:, 0:rw] - r) * vec[0:1]
    k = p[:, rw:2 * rw]
    k = k + (ps[:, rw:2 * rw] - k) * vec[1:2]
    v = p[:, 2 * rw:3 * rw]
    v = v + (ps[:, 2 * rw:3 * rw] - v) * vec[2:3]

    kk = k * vec[6:7]
    ss = _dot_exact_rhs(kk * kk, obd_ref[...])
    kk = kk / jnp.maximum(jnp.sqrt(ss), 1e-12)
    k = k * (1.0 + (a - 1.0) * vec[7:8])
    if has_vres:
        vg = _sigmoid(vec[5:6] + _dot(_dot(h + dh * mux[3:4], v1_ref[...]), v2_ref[...]))
        v = v + (vf_ref[0] - v) * vg

    rwp_ref[0] = r
    rwp_ref[1] = lw
    rwp_ref[2] = k
    rwp_ref[3] = v
    rwp_ref[4] = kk
    rwp_ref[5] = kk * a
    rwp_ref[6] = g

    o = 3 * rw
    gr_ref[:, _GQ:_GK] = p[:, o:o + 128] * (GLA_DK ** -0.5)
    gr_ref[:, _GK:_GLA] = p[:, o + 128:o + 256]
    la_pre = _dot(p[:, 3072:3200], ga2_ref[...]) + gab_ref[...]
    gr_ref[:, _GLA:_GV] = -_softplus(-la_pre) * (1.0 / GLA_GATE_TAU)
    gr_ref[:, _GV:_GG] = p[:, o + 256:o + 512]
    gate = p[:, o + 512:o + 768]
    gr_ref[:, _GG:_RQ] = gate * _sigmoid(gate)

    o = o + 768
    cos = cos_ref[...]
    sin = sin_ref[...]
    lo_half = (_iota((tm, 128), 1) % RET_DK) < (RET_DK // 2)

    def rope(t):
        swapped = jnp.where(lo_half, pltpu.roll(t, 128 - RET_DK // 2, 1), pltpu.roll(t, RET_DK // 2, 1))
        return t * cos + swapped * sin

    gr_ref[:, _RQ:_RK] = rope(p[:, o:o + 128]) * (RET_DK ** -0.5)
    gr_ref[:, _RK:_RV] = rope(p[:, o + 128:o + 256])
    gr_ref[:, _RV:_RG] = p[:, o + 256:o + 512]
    gate = p[:, o + 512:o + 768]
    gr_ref[:, _RG:_GR_W] = gate * _sigmoid(gate)


def _prep(x2d, mod_l, seq, tm, has_vres, n1g, win, mux, w1, w2, a1, a2, g1, g2, ga2, gab, vec, obd,
          cos_t, sin_t, v1=None, v2=None, rwp_first=None):
    T, D = x2d.shape
    tiles_per_seq = seq // tm
    n_tiles = T // tm
    consts = [n1g, win, mux, w1, w2, a1, a2, g1, g2, ga2, gab, vec, obd]
    in_specs = [pl.BlockSpec((tm, D), lambda i: (i, 0)),
                pl.BlockSpec((8, D), lambda i: (jnp.maximum(i * (tm // 8) - 1, 0), 0)),
                pl.BlockSpec((1, 6, D), lambda i: (i // tiles_per_seq, 0, 0))]
    in_specs += [_const_spec(a.shape) for a in consts]
    in_specs += [pl.BlockSpec((tm, 128), lambda i: (i % tiles_per_seq, 0)),
                 pl.BlockSpec((tm, 128), lambda i: (i % tiles_per_seq, 0))]
    args = [x2d, x2d, mod_l] + consts + [cos_t, sin_t]
    if has_vres:
        in_specs += [_const_spec(v1.shape), _const_spec(v2.shape),
                     pl.BlockSpec((1, tm, RWKV_WIDTH), lambda i: (3, i, 0))]
        args += [v1, v2, rwp_first]
    return pl.pallas_call(
        functools.partial(_prep_kernel, tiles_per_seq, has_vres),
        grid=(n_tiles,),
        in_specs=in_specs,
        out_specs=[pl.BlockSpec((7, tm, RWKV_WIDTH), lambda i: (0, i, 0)),
                   pl.BlockSpec((tm, _GR_W), lambda i: (i, 0))],
        out_shape=[jax.ShapeDtypeStruct((7, T, RWKV_WIDTH), F32),
                   jax.ShapeDtypeStruct((T, _GR_W), F32)],
        compiler_params=pltpu.CompilerParams(
            dimension_semantics=("parallel",), vmem_limit_bytes=VMEM_LIMIT_BYTES),
        name="proj_prep",
    )(*args)


def _rwkv_kernel(rwp_ref, par_ref, o_ref, st_ref):
    C, G, W, N = CHUNK, HEADS_PER_GROUP, GROUP_W, HEAD_DV

    @pl.when(pl.program_id(1) == 0)
    def _():
        st_ref[...] = jnp.zeros_like(st_ref)

    r = rwp_ref[0]
    lw = rwp_ref[1]
    k = rwp_ref[2]
    v = rwp_ref[3]
    kk = rwp_ref[4]
    kb = rwp_ref[5]
    g = rwp_ref[6]
    par = par_ref[...]

    ltri = (_iota((C, C), 0) >= _iota((C, C), 1)).astype(F32)
    cl = _dot_exact_lhs(ltri, lw)
    cle = cl[C - 1:C]
    e_neg = jnp.exp(-cl)
    e_end = jnp.exp(cle - cl)
    at = -kk * jnp.exp(cl - lw)
    bt = kb * e_neg
    kt = k * e_neg
    rt = r * jnp.exp(cl)
    bh = kb * e_end
    kh = k * e_end

    ar = jnp.concatenate([at, rt], axis=0)
    sb = _dot_nt(ar, _bd_rows(bt, G))
    sk = _dot_nt(ar, _bd_rows(kt, G))
    row = _iota((C, G * C), 0)
    col = _iota((C, G * C), 1) % C
    strict = row > col
    incl = row >= col
    a_ab = jnp.where(strict, sb[:C], 0.0)
    a_rb = jnp.where(incl, sb[C:], 0.0)
    a_ak = jnp.where(strict, sk[:C], 0.0)
    a_rk = jnp.where(incl, sk[C:], 0.0)

    tm = (row == col).astype(F32)
    xp = a_ab
    for _ in range(6):
        rr = _dot(jnp.concatenate([tm, xp], axis=0), _bd_rows(xp, G))
        tm = tm + rr[:C]
        xp = rr[C:]

    av = _dot(a_ak, _bd_rows(v, G))
    wt = _dot(tm, _bd_rows(at, G))
    ut = _dot(tm, _bd_rows(av, G))
    qh = rt + _dot(a_rb, _bd_rows(wt, G))
    yh = _dot(a_rb, _bd_rows(ut, G)) + _dot(a_rk, _bd_rows(v, G))
    bdm = _block_mask(W, W, N, N)
    eye = _iota((W, W), 0) == _iota((W, W), 1)
    pm = jnp.where(bdm, _dot_tn(bh, wt), 0.0) + jnp.where(eye, jnp.exp(cle), 0.0)
    zm = jnp.where(bdm, _dot_tn(jnp.concatenate([bh, kh], axis=0),
                                jnp.concatenate([ut, v], axis=0)), 0.0)
    m0 = st_ref[...]
    y = _dot(qh, m0) + yh
    st_ref[...] = _dot(pm, m0) + zm

    ones_bd = bdm.astype(F32)
    mean = _dot_exact_rhs(y, ones_bd) * (1.0 / N)
    yc = y - mean
    var = _dot_exact_rhs(yc * yc, ones_bd) * (1.0 / N)
    yn = yc * lax.rsqrt(var + RWKV_GN_EPS) * par[1:2] + par[2:3]
    bonus = _dot_exact_rhs(r * k * par[0:1], ones_bd) * v
    o_ref[...] = (yn + bonus) * g


def _rwkv(rwp, par, batch, seq):
    _, T, RW = rwp.shape
    nc = seq // CHUNK
    ng = RW // GROUP_W
    return pl.pallas_call(
        _rwkv_kernel,
        grid=(batch * ng, nc),
        in_specs=[pl.BlockSpec((7, CHUNK, GROUP_W), lambda bg, c: (0, (bg // ng) * nc + c, bg % ng)),
                  pl.BlockSpec((3, GROUP_W), lambda bg, c: (0, bg % ng))],
        out_specs=pl.BlockSpec((CHUNK, GROUP_W), lambda bg, c: ((bg // ng) * nc + c, bg % ng)),
        out_shape=jax.ShapeDtypeStruct((T, RW), F32),
        scratch_shapes=[pltpu.VMEM((GROUP_W, GROUP_W), F32)],
        compiler_params=pltpu.CompilerParams(
            dimension_semantics=("parallel", "arbitrary"), vmem_limit_bytes=VMEM_LIMIT_BYTES),
        name="rwkv7_chunk",
    )(rwp, par)


def _glaret_kernel(gr_ref, lng_ref, dec_ref, kdec_ref, qdec_ref, cdec_ref, o_ref, sg_ref, sr_ref):
    C, G, DV = CHUNK, GLA_HEADS, HEAD_DV
    WV = G * DV

    @pl.when(pl.program_id(1) == 0)
    def _():
        sg_ref[...] = jnp.zeros_like(sg_ref)
        sr_ref[...] = jnp.zeros_like(sr_ref)

    row = _iota((C, G * C), 0)
    col = _iota((C, G * C), 1) % C
    incl = row >= col
    ones_bd = _block_mask(WV, WV, DV, DV).astype(F32)
    st_mask = _block_mask(WV, G * GLA_DK, DV, GLA_DK)

    q = gr_ref[:, _GQ:_GK]
    k = gr_ref[:, _GK:_GLA]
    la = gr_ref[:, _GLA:_GV]
    v = gr_ref[:, _GV:_GG]
    ltri = (_iota((C, C), 0) >= _iota((C, C), 1)).astype(F32)
    bc = _dot_exact_lhs(ltri, la)
    be = bc[C - 1:C]
    ep = jnp.exp(bc)
    en = jnp.exp(-bc)
    qp = q * ep
    att = jnp.where(incl, _dot_nt(qp, _bd_rows(k * en, G)), _dot_nt(q * en, _bd_rows(k * ep, G)))
    sg = sg_ref[...]
    o = _dot(att, _bd_rows(v, G)) + _dot_nt(qp, sg)
    sg_ref[...] = sg * jnp.exp(be) + jnp.where(st_mask, _dot_tn(v, k * jnp.exp(be - bc)), 0.0)
    o = o * lax.rsqrt(_dot_exact_rhs(o * o, ones_bd) * (1.0 / DV) + EPS) * lng_ref[...]
    o_ref[:, 0:WV] = o * gr_ref[:, _GG:_RQ]

    q = gr_ref[:, _RQ:_RK]
    k = gr_ref[:, _RK:_RV]
    v = gr_ref[:, _RV:_RG]
    sr = sr_ref[...]
    o = _dot(_dot_nt(q, _bd_rows(k, G)) * dec_ref[...], _bd_rows(v, G)) + _dot_nt(q * qdec_ref[...], sr)
    sr_ref[...] = sr * cdec_ref[...] + jnp.where(st_mask, _dot_tn(v, k * kdec_ref[...]), 0.0)
    mean = _dot_exact_rhs(o, ones_bd) * (1.0 / DV)
    oc = o - mean
    var = _dot_exact_rhs(oc * oc, ones_bd) * (1.0 / DV)
    o_ref[:, WV:2 * WV] = oc * lax.rsqrt(var + EPS) * gr_ref[:, _RG:_GR_W]


def _glaret(gr, lng, dec, kdec, qdec, cdec, batch, seq):
    T = gr.shape[0]
    nc = seq // CHUNK
    wv = GLA_HEADS * HEAD_DV
    consts = [lng, dec, kdec, qdec, cdec]
    return pl.pallas_call(
        _glaret_kernel,
        grid=(batch, nc),
        in_specs=[pl.BlockSpec((CHUNK, _GR_W), lambda b, c: (b * nc + c, 0))]
                 + [_const_spec(a.shape) for a in consts],
        out_specs=pl.BlockSpec((CHUNK, 2 * wv), lambda b, c: (b * nc + c, 0)),
        out_shape=jax.ShapeDtypeStruct((T, 2 * wv), F32),
        scratch_shapes=[pltpu.VMEM((wv, GLA_HEADS * GLA_DK), F32),
                        pltpu.VMEM((wv, RET_HEADS * RET_DK), F32)],
        compiler_params=pltpu.CompilerParams(
            dimension_semantics=("parallel", "arbitrary"), vmem_limit_bytes=VMEM_LIMIT_BYTES),
        name="gla_retention_chunk",
    )(gr, *consts)


def _outffn_kernel(final_norm, x_ref, ya_ref, ybc_ref, mod_ref, n2g_ref, wo_ref, wg_ref, wu_ref,
                   wd_ref, nfg_ref, o_ref):
    mod = mod_ref[0]
    y = jnp.concatenate([ya_ref[...], ybc_ref[...]], axis=-1)
    x = x_ref[...] + mod[2:3] * _dot(y, wo_ref[...])
    ms = jnp.mean(x * x, axis=-1, keepdims=True)
    h = x * lax.rsqrt(ms + EPS) * n2g_ref[...]
    h = (h * (1.0 + mod[4:5]) + mod[3:4]).astype(BF16)
    gate = jnp.dot(h, wg_ref[...], preferred_element_type=F32)
    up = jnp.dot(h, wu_ref[...], preferred_element_type=F32)
    x = x + mod[5:6] * _dot(gate * _sigmoid(gate) * up, wd_ref[...])
    if final_norm:
        ms = jnp.mean(x * x, axis=-1, keepdims=True)
        x = x * lax.rsqrt(ms + EPS) * nfg_ref[...]
    o_ref[...] = x


def _outffn(x2d, ya, ybc, mod_l, seq, tm, final_norm, n2g, wo, wg, wu, wd, nfg):
    T, D = x2d.shape
    tiles_per_seq = seq // tm
    consts = [n2g, wo, wg, wu, wd, nfg]
    return pl.pallas_call(
        functools.partial(_outffn_kernel, final_norm),
        grid=(T // tm,),
        in_specs=[pl.BlockSpec((tm, D), lambda i: (i, 0)),
                  pl.BlockSpec((tm, ya.shape[1]), lambda i: (i, 0)),
                  pl.BlockSpec((tm, ybc.shape[1]), lambda i: (i, 0)),
                  pl.BlockSpec((1, 6, D), lambda i: (i // tiles_per_seq, 0, 0))]
                 + [_const_spec(a.shape) for a in consts],
        out_specs=pl.BlockSpec((tm, D), lambda i: (i, 0)),
        out_shape=jax.ShapeDtypeStruct((T, D), F32),
        compiler_params=pltpu.CompilerParams(
            dimension_semantics=("parallel",), vmem_limit_bytes=VMEM_LIMIT_BYTES),
        name="outproj_swiglu",
    )(x2d, ya, ybc, mod_l, *consts)


def _rope_tables(seq):
    half = RET_DK // 2
    inv_freq = ROPE_BASE ** (-jnp.arange(half, dtype=F32) / half)
    ang = jnp.arange(seq, dtype=F32)[:, None] * inv_freq[None, :]
    cos, sin = jnp.cos(ang), jnp.sin(ang)
    cos_t = jnp.tile(jnp.concatenate([cos, cos], axis=-1), (1, RET_HEADS))
    sin_t = jnp.tile(jnp.concatenate([-sin, sin], axis=-1), (1, RET_HEADS))
    return cos_t, sin_t


def _retention_tables():
    H, C = RET_HEADS, CHUNK
    log_gamma = jnp.log1p(-(2.0 ** (-5.0 - jnp.arange(H, dtype=F32))))
    pos = jnp.arange(C, dtype=F32)
    intra = jnp.exp(log_gamma[:, None, None] * jnp.abs(pos[:, None] - pos[None, :]))
    dec = jnp.transpose(intra, (1, 0, 2)).reshape(C, H * C)
    k_dec = jnp.exp(log_gamma[None, :] * (C - 1.0 - pos)[:, None])
    q_dec = jnp.exp(log_gamma[None, :] * (pos + 1.0)[:, None])
    chunk_dec = jnp.exp(log_gamma * C)
    kdec = jnp.repeat(k_dec, RET_DK, axis=1)
    qdec = jnp.repeat(q_dec, RET_DK, axis=1)
    cdec = jnp.repeat(chunk_dec, RET_DK)[None, :]
    return dec, kdec, qdec, cdec


def kernel(x, c, ada_w, ada_b, norm1_g, norm2_g, w_in, w_out, rk_mu_rkv, rk_mu_x, rk_w0, rk_w1, rk_w2, rk_a0, rk_a1, rk_a2, rk_g1, rk_g2, rk_k_k, rk_k_a, rk_r_k, rk_ln_g, rk_ln_b, rk_mu_v, rk_v0, rk_v1, rk_v2, gla_a1, gla_a2, gla_ab, gla_ln_g, ffn_w_gate, ffn_w_up, ffn_w_down, norm_f_g):
    B, S, D = x.shape
    L = ada_w.shape[0]
    T = B * S
    tm = 256
    assert S % tm == 0 and S % CHUNK == 0

    mod = _adaln(c, ada_w, ada_b).reshape(L, B, 6, D)
    cos_t, sin_t = _rope_tables(S)
    dec, kdec, qdec, cdec = _retention_tables()
    obd = _block_mask(RWKV_WIDTH, RWKV_WIDTH, HEAD_DV, HEAD_DV).astype(BF16)
    zeros_rw = jnp.zeros((1, RWKV_WIDTH), F32)

    x2d = x.reshape(T, D)
    rwp_first = None
    for l in range(L):
        has_vres = l > 0
        win = jnp.concatenate([w_in[l], gla_a1[l], jnp.zeros((D, 128 - gla_a1.shape[2]), F32)],
                              axis=1).astype(BF16)
        ga2 = jnp.zeros((128, 128), F32).at[:gla_a2.shape[1]].set(gla_a2[l]).astype(BF16)
        mux = jnp.concatenate([rk_mu_x[l], rk_mu_v[l - 1][None] if has_vres else jnp.zeros((1, D), F32)], axis=0)
        vec = jnp.stack([rk_mu_rkv[l, 0], rk_mu_rkv[l, 1], rk_mu_rkv[l, 2], rk_w0[l], rk_a0[l],
                         rk_v0[l - 1] if has_vres else zeros_rw[0], rk_k_k[l], rk_k_a[l]], axis=0)
        extra = {}
        if has_vres:
            extra = dict(v1=rk_v1[l - 1].astype(BF16), v2=rk_v2[l - 1].astype(BF16), rwp_first=rwp_first)
        rwp, gr = _prep(x2d, mod[l], S, tm, has_vres, norm1_g[l][None], win, mux,
                        rk_w1[l].astype(BF16), rk_w2[l].astype(BF16), rk_a1[l].astype(BF16),
                        rk_a2[l].astype(BF16), rk_g1[l].astype(BF16), rk_g2[l].astype(BF16),
                        ga2, gla_ab[l][None], vec, obd, cos_t, sin_t, **extra)
        if l == 0:
            rwp_first = rwp
        par = jnp.stack([rk_r_k[l].reshape(-1), rk_ln_g[l], rk_ln_b[l]], axis=0)
        ya = _rwkv(rwp, par, B, S)
        ybc = _glaret(gr, jnp.tile(gla_ln_g[l], GLA_HEADS)[None], dec, kdec, qdec, cdec, B, S)
        x2d = _outffn(x2d, ya, ybc, mod[l], S, tm, l == L - 1, norm2_g[l][None],
                      w_out[l].astype(BF16), ffn_w_gate[l].astype(BF16), ffn_w_up[l].astype(BF16),
                      ffn_w_down[l].astype(BF16), norm_f_g[None])
    return x2d.reshape(B, S, D)
```

```python
import functools

import numpy as np
import jax
import jax.numpy as jnp
from jax import lax
from jax.experimental import pallas as pl
from jax.experimental.pallas import tpu as pltpu

F32 = jnp.float32
BF16 = jnp.bfloat16

CHUNK = 64
EPS = 1e-6
HEAD_DV = 64
RWKV_HEADS = 8
RWKV_WIDTH = RWKV_HEADS * HEAD_DV
RWKV_GN_EPS = 64e-5
GLA_HEADS = 4
GLA_DK = 32
GLA_GATE_TAU = 16.0
RET_HEADS = 4
RET_DK = 32
ROPE_BASE = 10000.0
HEADS_PER_GROUP = 4
GROUP_W = HEADS_PER_GROUP * HEAD_DV
VMEM_LIMIT_BYTES = 56 * 1024 * 1024
RWKV_CHUNKS_PER_STEP = 4
GLARET_CHUNKS_PER_STEP = 4

_GQ, _GK, _GLA, _GV, _GG, _RQ, _RK, _RV, _RG, _GR_W = 0, 128, 256, 384, 640, 896, 1024, 1152, 1408, 1664


def _dot(a, b):
    return jnp.dot(a.astype(BF16), b.astype(BF16), preferred_element_type=F32)


def _dot_nt(a, b):
    return lax.dot_general(a.astype(BF16), b.astype(BF16), (((1,), (1,)), ((), ())),
                           preferred_element_type=F32)


def _dot_tn(a, b):
    return lax.dot_general(a.astype(BF16), b.astype(BF16), (((0,), (0,)), ((), ())),
                           preferred_element_type=F32)


def _split3(x):
    hi = x.astype(BF16)
    r1 = x - hi.astype(F32)
    mid = r1.astype(BF16)
    lo = (r1 - mid.astype(F32)).astype(BF16)
    return hi, mid, lo


def _dot_exact_rhs(x, m):
    hi, mid, lo = _split3(x)
    mb = m.astype(BF16)
    return (jnp.dot(hi, mb, preferred_element_type=F32)
            + jnp.dot(mid, mb, preferred_element_type=F32)
            + jnp.dot(lo, mb, preferred_element_type=F32))


def _dot_exact_lhs(m, x):
    hi, mid, lo = _split3(x)
    mb = m.astype(BF16)
    return (jnp.dot(mb, hi, preferred_element_type=F32)
            + jnp.dot(mb, mid, preferred_element_type=F32)
            + jnp.dot(mb, lo, preferred_element_type=F32))


def _sigmoid(x):
    return 1.0 / (1.0 + jnp.exp(-x))


def _softplus(x):
    return jnp.maximum(x, 0.0) + jnp.log(1.0 + jnp.exp(-jnp.abs(x)))


def _iota(shape, axis):
    return lax.broadcasted_iota(jnp.int32, shape, axis)


def _bd_rows(x, groups):
    c, w = x.shape
    n = w // groups
    t = jnp.concatenate([x] * groups, axis=0)
    keep = (_iota(t.shape, 0) // c) == (_iota(t.shape, 1) // n)
    return jnp.where(keep, t, 0.0)


def _block_mask(rows, cols, rblk, cblk):
    return (_iota((rows, cols), 0) // rblk) == (_iota((rows, cols), 1) // cblk)


def _const_spec(shape):
    nd = len(shape)
    return pl.BlockSpec(shape, lambda *_: (0,) * nd)


def _adaln_kernel(c_ref, w_ref, b_ref, o_ref):
    c = c_ref[...]
    cond = c * _sigmoid(c)
    o_ref[0] = _dot(cond, w_ref[0]) + b_ref[0]


def _adaln(c, ada_w, ada_b):
    L, D, D6 = ada_w.shape
    B = c.shape[0]
    tn = 1536
    return pl.pallas_call(
        _adaln_kernel,
        grid=(L, D6 // tn),
        in_specs=[pl.BlockSpec((B, D), lambda l, j: (0, 0)),
                  pl.BlockSpec((1, D, tn), lambda l, j: (l, 0, j)),
                  pl.BlockSpec((1, 1, tn), lambda l, j: (l, 0, j))],
        out_specs=pl.BlockSpec((1, B, tn), lambda l, j: (l, 0, j)),
        out_shape=jax.ShapeDtypeStruct((L, B, D6), F32),
        compiler_params=pltpu.CompilerParams(
            dimension_semantics=("arbitrary", "arbitrary"), vmem_limit_bytes=VMEM_LIMIT_BYTES),
        name="adaln_mod",
    )(c, ada_w, ada_b.reshape(L, 1, D6))


def _prep_kernel(tiles_per_seq, has_vres, *refs):
    if has_vres:
        (x_ref, xh_ref, mod_ref, n1g_ref, win_ref, mux_ref, w1_ref, w2_ref, a1_ref, a2_ref,
         g1_ref, g2_ref, ga2_ref, gab_ref, vec_ref, obd_ref, cos_ref, sin_ref,
         v1_ref, v2_ref, vf_ref, rwp_ref, gr_ref) = refs
    else:
        (x_ref, xh_ref, mod_ref, n1g_ref, win_ref, mux_ref, w1_ref, w2_ref, a1_ref, a2_ref,
         g1_ref, g2_ref, ga2_ref, gab_ref, vec_ref, obd_ref, cos_ref, sin_ref,
         rwp_ref, gr_ref) = refs
    tm = x_ref.shape[0]
    rw = RWKV_WIDTH
    first = (pl.program_id(0) % tiles_per_seq) == 0

    xe = jnp.concatenate([xh_ref[...], x_ref[...]], axis=0)
    mod = mod_ref[0]
    ms = jnp.mean(xe * xe, axis=-1, keepdims=True)
    he = xe * lax.rsqrt(ms + EPS) * n1g_ref[...]
    he = he * (1.0 + mod[1:2]) + mod[0:1]
    halo = jnp.logical_and(_iota((tm + 8, 1), 0) < 8, first)
    he = jnp.where(halo, 0.0, he)

    proj = _dot(he, win_ref[...])
    p = proj[8:]
    ps = pltpu.roll(proj[:, :3 * rw], 1, 0)[8:]
    h = he[8:]
    dh = pltpu.roll(he, 1, 0)[8:] - h
    mux = mux_ref[...]
    vec = vec_ref[...]

    z = vec[3:4] + _dot(jnp.tanh(_dot(h + dh * mux[0:1], w1_ref[...])), w2_ref[...])
    lw = -jnp.exp(-_softplus(-z) - 0.5)
    a = _sigmoid(vec[4:5] + _dot(_dot(h + dh * mux[1:2], a1_ref[...]), a2_ref[...]))
    g = _dot(_sigmoid(_dot(h + dh * mux[2:3], g1_ref[...])), g2_ref[...])

    r = p[:, 0:rw]
    r = r + (ps[:, 0:rw] - r) * vec[0:1]
    k = p[:, rw:2 * rw]
    k = k + (ps[:, rw:2 * rw] - k) * vec[1:2]
    v = p[:, 2 * rw:3 * rw]
    v = v + (ps[:, 2 * rw:3 * rw] - v) * vec[2:3]

    kk = k * vec[6:7]
    ss = _dot_exact_rhs(kk * kk, obd_ref[...])
    kk = kk / jnp.maximum(jnp.sqrt(ss), 1e-12)
    k = k * (1.0 + (a - 1.0) * vec[7:8])
    if has_vres:
        vg = _sigmoid(vec[5:6] + _dot(_dot(h + dh * mux[3:4], v1_ref[...]), v2_ref[...]))
        v = v + (vf_ref[0] - v) * vg

    rwp_ref[0] = r
    rwp_ref[1] = lw
    rwp_ref[2] = k
    rwp_ref[3] = v
    rwp_ref[4] = kk
    rwp_ref[5] = kk * a
    rwp_ref[6] = g

    o = 3 * rw
    gr_ref[:, _GQ:_GK] = p[:, o:o + 128] * (GLA_DK ** -0.5)
    gr_ref[:, _GK:_GLA] = p[:, o + 128:o + 256]
    la_pre = _dot(p[:, 3072:3200], ga2_ref[...]) + gab_ref[...]
    gr_ref[:, _GLA:_GV] = -_softplus(-la_pre) * (1.0 / GLA_GATE_TAU)
    gr_ref[:, _GV:_GG] = p[:, o + 256:o + 512]
    gate = p[:, o + 512:o + 768]
    gr_ref[:, _GG:_RQ] = gate * _sigmoid(gate)

    o = o + 768
    cos = cos_ref[...]
    sin = sin_ref[...]
    lo_half = (_iota((tm, 128), 1) % RET_DK) < (RET_DK // 2)

    def rope(t):
        swapped = jnp.where(lo_half, pltpu.roll(t, 128 - RET_DK // 2, 1), pltpu.roll(t, RET_DK // 2, 1))
        return t * cos + swapped * sin

    gr_ref[:, _RQ:_RK] = rope(p[:, o:o + 128]) * (RET_DK ** -0.5)
    gr_ref[:, _RK:_RV] = rope(p[:, o + 128:o + 256])
    gr_ref[:, _RV:_RG] = p[:, o + 256:o + 512]
    gate = p[:, o + 512:o + 768]
    gr_ref[:, _RG:_GR_W] = gate * _sigmoid(gate)


def _prep(x2d, mod_l, seq, tm, has_vres, n1g, win, mux, w1, w2, a1, a2, g1, g2, ga2, gab, vec, obd,
          cos_t, sin_t, v1=None, v2=None, rwp_first=None):
    T, D = x2d.shape
    tiles_per_seq = seq // tm
    n_tiles = T // tm
    consts = [n1g, win, mux, w1, w2, a1, a2, g1, g2, ga2, gab, vec, obd]
    in_specs = [pl.BlockSpec((tm, D), lambda i: (i, 0)),
                pl.BlockSpec((8, D), lambda i: (jnp.maximum(i * (tm // 8) - 1, 0), 0)),
                pl.BlockSpec((1, 6, D), lambda i: (i // tiles_per_seq, 0, 0))]
    in_specs += [_const_spec(a.shape) for a in consts]
    in_specs += [pl.BlockSpec((tm, 128), lambda i: (i % tiles_per_seq, 0)),
                 pl.BlockSpec((tm, 128), lambda i: (i % tiles_per_seq, 0))]
    args = [x2d, x2d, mod_l] + consts + [cos_t, sin_t]
    if has_vres:
        in_specs += [_const_spec(v1.shape), _const_spec(v2.shape),
                     pl.BlockSpec((1, tm, RWKV_WIDTH), lambda i: (3, i, 0))]
        args += [v1, v2, rwp_first]
    return pl.pallas_call(
        functools.partial(_prep_kernel, tiles_per_seq, has_vres),
        grid=(n_tiles,),
        in_specs=in_specs,
        out_specs=[pl.BlockSpec((7, tm, RWKV_WIDTH), lambda i: (0, i, 0)),
                   pl.BlockSpec((tm, _GR_W), lambda i: (i, 0))],
        out_shape=[jax.ShapeDtypeStruct((7, T, RWKV_WIDTH), F32),
                   jax.ShapeDtypeStruct((T, _GR_W), F32)],
        compiler_params=pltpu.CompilerParams(
            dimension_semantics=("parallel",), vmem_limit_bytes=VMEM_LIMIT_BYTES),
        name="proj_prep",
    )(*args)


def _chunk_masks(groups):
    C = CHUNK
    row = _iota((C, groups * C), 0)
    col = _iota((C, groups * C), 1) % C
    ltri = (_iota((C, C), 0) >= _iota((C, C), 1)).astype(F32)
    return ltri, row > col, row >= col, (row == col).astype(F32)


def _rwkv_chunk_factors(units, masks, bdm, eye):
    C, G = CHUNK, HEADS_PER_GROUP
    ltri, strict, incl, ident = masks
    n = range(len(units))
    r, lw, k, v, kk, kb = ([u[i] for u in units] for i in range(6))
    cl = [_dot_exact_lhs(ltri, lw[i]) for i in n]
    cle = [cl[i][C - 1:C] for i in n]
    e_neg = [jnp.exp(-cl[i]) for i in n]
    e_end = [jnp.exp(cle[i] - cl[i]) for i in n]
    at = [-kk[i] * jnp.exp(cl[i] - lw[i]) for i in n]
    rt = [r[i] * jnp.exp(cl[i]) for i in n]
    ar = [jnp.concatenate([at[i], rt[i]], axis=0) for i in n]
    sb = [_dot_nt(ar[i], _bd_rows(kb[i] * e_neg[i], G)) for i in n]
    sk = [_dot_nt(ar[i], _bd_rows(k[i] * e_neg[i], G)) for i in n]
    a_rb = [jnp.where(incl, sb[i][C:], 0.0) for i in n]
    a_ak = [jnp.where(strict, sk[i][:C], 0.0) for i in n]
    a_rk = [jnp.where(incl, sk[i][C:], 0.0) for i in n]

    tm = [ident for _ in n]
    xp = [jnp.where(strict, sb[i][:C], 0.0) for i in n]
    for _ in range(6):
        rr = [_dot(jnp.concatenate([tm[i], xp[i]], axis=0), _bd_rows(xp[i], G)) for i in n]
        tm = [tm[i] + rr[i][:C] for i in n]
        xp = [rr[i][C:] for i in n]

    av = [_dot(a_ak[i], _bd_rows(v[i], G)) for i in n]
    wt = [_dot(tm[i], _bd_rows(at[i], G)) for i in n]
    ut = [_dot(tm[i], _bd_rows(av[i], G)) for i in n]
    qh = [rt[i] + _dot(a_rb[i], _bd_rows(wt[i], G)) for i in n]
    yh = [_dot(a_rb[i], _bd_rows(ut[i], G)) + _dot(a_rk[i], _bd_rows(v[i], G)) for i in n]
    bh = [kb[i] * e_end[i] for i in n]
    pm = [jnp.where(bdm, _dot_tn(bh[i], wt[i]), 0.0) + jnp.where(eye, jnp.exp(cle[i]), 0.0) for i in n]
    zm = [jnp.where(bdm, _dot_tn(jnp.concatenate([bh[i], k[i] * e_end[i]], axis=0),
                                 jnp.concatenate([ut[i], v[i]], axis=0)), 0.0) for i in n]
    return qh, yh, pm, zm


def _rwkv_kernel(cb, rwp_ref, par_ref, o_ref, st_ref):
    C, W, N = CHUNK, GROUP_W, HEAD_DV
    ngroups = o_ref.shape[1] // W

    @pl.when(pl.program_id(1) == 0)
    def _():
        st_ref[...] = jnp.zeros_like(st_ref)

    masks = _chunk_masks(HEADS_PER_GROUP)
    bdm = _block_mask(W, W, N, N)
    eye = _iota((W, W), 0) == _iota((W, W), 1)
    ones_bd = bdm.astype(F32)

    where = [(slice(j * C, (j + 1) * C), slice(gi * W, (gi + 1) * W))
             for j in range(cb) for gi in range(ngroups)]
    qh, yh, pm, zm = _rwkv_chunk_factors(
        [[rwp_ref[i, rows, lanes] for i in range(6)] for rows, lanes in where], masks, bdm, eye)

    m = [st_ref[gi] for gi in range(ngroups)]
    y = []
    for j in range(cb):
        for gi in range(ngroups):
            u = j * ngroups + gi
            y.append(_dot(qh[u], m[gi]) + yh[u])
            m[gi] = _dot(pm[u], m[gi]) + zm[u]
    for gi in range(ngroups):
        st_ref[gi] = m[gi]

    n = range(len(where))
    par = [par_ref[:, lanes] for _, lanes in where]
    mean = [_dot_exact_rhs(y[u], ones_bd) * (1.0 / N) for u in n]
    yc = [y[u] - mean[u] for u in n]
    var = [_dot_exact_rhs(yc[u] * yc[u], ones_bd) * (1.0 / N) for u in n]
    bsum = [_dot_exact_rhs(rwp_ref[0, rows, lanes] * rwp_ref[2, rows, lanes] * par[u][0:1], ones_bd)
            for u, (rows, lanes) in enumerate(where)]
    for u, (rows, lanes) in enumerate(where):
        yn = yc[u] * lax.rsqrt(var[u] + RWKV_GN_EPS) * par[u][1:2] + par[u][2:3]
        o_ref[rows, lanes] = (yn + bsum[u] * rwp_ref[3, rows, lanes]) * rwp_ref[6, rows, lanes]


def _rwkv(rwp, par, batch, seq, cb):
    _, T, RW = rwp.shape
    steps = seq // (CHUNK * cb)
    return pl.pallas_call(
        functools.partial(_rwkv_kernel, cb),
        grid=(batch, steps),
        in_specs=[pl.BlockSpec((7, cb * CHUNK, RW), lambda b, c: (0, b * steps + c, 0)),
                  _const_spec(par.shape)],
        out_specs=pl.BlockSpec((cb * CHUNK, RW), lambda b, c: (b * steps + c, 0)),
        out_shape=jax.ShapeDtypeStruct((T, RW), F32),
        scratch_shapes=[pltpu.VMEM((RW // GROUP_W, GROUP_W, GROUP_W), F32)],
        compiler_params=pltpu.CompilerParams(
            dimension_semantics=("parallel", "arbitrary"), vmem_limit_bytes=VMEM_LIMIT_BYTES),
        name="rwkv7_chunk",
    )(rwp, par)


def _glaret_kernel(cb, gr_ref, lng_ref, dec_ref, kdec_ref, qdec_ref, cdec_ref, o_ref, sg_ref, sr_ref):
    C, G, DV = CHUNK, GLA_HEADS, HEAD_DV
    WV = G * DV

    @pl.when(pl.program_id(1) == 0)
    def _():
        sg_ref[...] = jnp.zeros_like(sg_ref)
        sr_ref[...] = jnp.zeros_like(sr_ref)

    ltri, _, incl, _ = _chunk_masks(G)
    ones_bd = _block_mask(WV, WV, DV, DV).astype(F32)
    st_mask = _block_mask(WV, G * GLA_DK, DV, GLA_DK)
    lng = lng_ref[...]
    dec = dec_ref[...]
    kdec = kdec_ref[...]
    qdec = qdec_ref[...]
    cdec = cdec_ref[...]

    n = range(cb)
    rows = [slice(j * C, (j + 1) * C) for j in n]
    gq = [gr_ref[rows[j], _GQ:_GK] for j in n]
    gk = [gr_ref[rows[j], _GK:_GLA] for j in n]
    gv = [gr_ref[rows[j], _GV:_GG] for j in n]
    bc = [_dot_exact_lhs(ltri, gr_ref[rows[j], _GLA:_GV]) for j in n]
    be = [bc[j][C - 1:C] for j in n]
    ep = [jnp.exp(bc[j]) for j in n]
    en = [jnp.exp(-bc[j]) for j in n]
    qp = [gq[j] * ep[j] for j in n]
    att_lo = [_dot_nt(qp[j], _bd_rows(gk[j] * en[j], G)) for j in n]
    att_hi = [_dot_nt(gq[j] * en[j], _bd_rows(gk[j] * ep[j], G)) for j in n]
    rq = [gr_ref[rows[j], _RQ:_RK] for j in n]
    rk = [gr_ref[rows[j], _RK:_RV] for j in n]
    rv = [gr_ref[rows[j], _RV:_RG] for j in n]
    sc = [_dot_nt(rq[j], _bd_rows(rk[j], G)) * dec for j in n]
    g_intra = [_dot(jnp.where(incl, att_lo[j], att_hi[j]), _bd_rows(gv[j], G)) for j in n]
    r_intra = [_dot(sc[j], _bd_rows(rv[j], G)) for j in n]
    g_kv = [jnp.where(st_mask, _dot_tn(gv[j], gk[j] * jnp.exp(be[j] - bc[j])), 0.0) for j in n]
    r_kv = [jnp.where(st_mask, _dot_tn(rv[j], rk[j] * kdec), 0.0) for j in n]

    sg = sg_ref[...]
    sr = sr_ref[...]
    g_o = []
    r_o = []
    for j in n:
        g_o.append(g_intra[j] + _dot_nt(qp[j], sg))
        sg = sg * jnp.exp(be[j]) + g_kv[j]
        r_o.append(r_intra[j] + _dot_nt(rq[j] * qdec, sr))
        sr = sr * cdec + r_kv[j]
    sg_ref[...] = sg
    sr_ref[...] = sr

    g_ms = [_dot_exact_rhs(g_o[j] * g_o[j], ones_bd) * (1.0 / DV) for j in n]
    r_mean = [_dot_exact_rhs(r_o[j], ones_bd) * (1.0 / DV) for j in n]
    r_c = [r_o[j] - r_mean[j] for j in n]
    r_var = [_dot_exact_rhs(r_c[j] * r_c[j], ones_bd) * (1.0 / DV) for j in n]
    for j in n:
        o_ref[rows[j], 0:WV] = g_o[j] * lax.rsqrt(g_ms[j] + EPS) * lng * gr_ref[rows[j], _GG:_RQ]
        o_ref[rows[j], WV:2 * WV] = r_c[j] * lax.rsqrt(r_var[j] + EPS) * gr_ref[rows[j], _RG:_GR_W]


def _glaret(gr, lng, dec, kdec, qdec, cdec, batch, seq, cb):
    T = gr.shape[0]
    nc = seq // (CHUNK * cb)
    wv = GLA_HEADS * HEAD_DV
    consts = [lng, dec, kdec, qdec, cdec]
    return pl.pallas_call(
        functools.partial(_glaret_kernel, cb),
        grid=(batch, nc),
        in_specs=[pl.BlockSpec((cb * CHUNK, _GR_W), lambda b, c: (b * nc + c, 0))]
                 + [_const_spec(a.shape) for a in consts],
        out_specs=pl.BlockSpec((cb * CHUNK, 2 * wv), lambda b, c: (b * nc + c, 0)),
        out_shape=jax.ShapeDtypeStruct((T, 2 * wv), F32),
        scratch_shapes=[pltpu.VMEM((wv, GLA_HEADS * GLA_DK), F32),
                        pltpu.VMEM((wv, RET_HEADS * RET_DK), F32)],
        compiler_params=pltpu.CompilerParams(
            dimension_semantics=("parallel", "arbitrary"), vmem_limit_bytes=VMEM_LIMIT_BYTES),
        name="gla_retention_chunk",
    )(gr, *consts)


def _outffn_kernel(final_norm, x_ref, ya_ref, ybc_ref, mod_ref, n2g_ref, wo_ref, wg_ref, wu_ref,
                   wd_ref, nfg_ref, o_ref):
    mod = mod_ref[0]
    y = jnp.concatenate([ya_ref[...], ybc_ref[...]], axis=-1)
    x = x_ref[...] + mod[2:3] * _dot(y, wo_ref[...])
    ms = jnp.mean(x * x, axis=-1, keepdims=True)
    h = x * lax.rsqrt(ms + EPS) * n2g_ref[...]
    h = (h * (1.0 + mod[4:5]) + mod[3:4]).astype(BF16)
    gate = jnp.dot(h, wg_ref[...], preferred_element_type=F32)
    up = jnp.dot(h, wu_ref[...], preferred_element_type=F32)
    x = x + mod[5:6] * _dot(gate * _sigmoid(gate) * up, wd_ref[...])
    if final_norm:
        ms = jnp.mean(x * x, axis=-1, keepdims=True)
        x = x * lax.rsqrt(ms + EPS) * nfg_ref[...]
    o_ref[...] = x


def _outffn(x2d, ya, ybc, mod_l, seq, tm, final_norm, n2g, wo, wg, wu, wd, nfg):
    T, D = x2d.shape
    tiles_per_seq = seq // tm
    consts = [n2g, wo, wg, wu, wd, nfg]
    return pl.pallas_call(
        functools.partial(_outffn_kernel, final_norm),
        grid=(T // tm,),
        in_specs=[pl.BlockSpec((tm, D), lambda i: (i, 0)),
                  pl.BlockSpec((tm, ya.shape[1]), lambda i: (i, 0)),
                  pl.BlockSpec((tm, ybc.shape[1]), lambda i: (i, 0)),
                  pl.BlockSpec((1, 6, D), lambda i: (i // tiles_per_seq, 0, 0))]
                 + [_const_spec(a.shape) for a in consts],
        out_specs=pl.BlockSpec((tm, D), lambda i: (i, 0)),
        out_shape=jax.ShapeDtypeStruct((T, D), F32),
        compiler_params=pltpu.CompilerParams(
            dimension_semantics=("parallel",), vmem_limit_bytes=VMEM_LIMIT_BYTES),
        name="outproj_swiglu",
    )(x2d, ya, ybc, mod_l, *consts)


def _rope_tables(seq):
    half = RET_DK // 2
    inv_freq = ROPE_BASE ** (-jnp.arange(half, dtype=F32) / half)
    ang = jnp.arange(seq, dtype=F32)[:, None] * inv_freq[None, :]
    cos, sin = jnp.cos(ang), jnp.sin(ang)
    cos_t = jnp.tile(jnp.concatenate([cos, cos], axis=-1), (1, RET_HEADS))
    sin_t = jnp.tile(jnp.concatenate([-sin, sin], axis=-1), (1, RET_HEADS))
    return cos_t, sin_t


def _retention_tables():
    H, C = RET_HEADS, CHUNK
    log_gamma = jnp.log1p(-(2.0 ** (-5.0 - jnp.arange(H, dtype=F32))))
    pos = jnp.arange(C, dtype=F32)
    intra = jnp.exp(log_gamma[:, None, None] * jnp.abs(pos[:, None] - pos[None, :]))
    dec = jnp.transpose(intra, (1, 0, 2)).reshape(C, H * C)
    k_dec = jnp.exp(log_gamma[None, :] * (C - 1.0 - pos)[:, None])
    q_dec = jnp.exp(log_gamma[None, :] * (pos + 1.0)[:, None])
    chunk_dec = jnp.exp(log_gamma * C)
    kdec = jnp.repeat(k_dec, RET_DK, axis=1)
    qdec = jnp.repeat(q_dec, RET_DK, axis=1)
    cdec = jnp.repeat(chunk_dec, RET_DK)[None, :]
    return dec, kdec, qdec, cdec


def kernel(x, c, ada_w, ada_b, norm1_g, norm2_g, w_in, w_out, rk_mu_rkv, rk_mu_x, rk_w0, rk_w1, rk_w2, rk_a0, rk_a1, rk_a2, rk_g1, rk_g2, rk_k_k, rk_k_a, rk_r_k, rk_ln_g, rk_ln_b, rk_mu_v, rk_v0, rk_v1, rk_v2, gla_a1, gla_a2, gla_ab, gla_ln_g, ffn_w_gate, ffn_w_up, ffn_w_down, norm_f_g):
    B, S, D = x.shape
    L = ada_w.shape[0]
    T = B * S
    tm = 256
    assert S % tm == 0 and S % CHUNK == 0

    mod = _adaln(c, ada_w, ada_b).reshape(L, B, 6, D)
    cos_t, sin_t = _rope_tables(S)
    dec, kdec, qdec, cdec = _retention_tables()
    obd = _block_mask(RWKV_WIDTH, RWKV_WIDTH, HEAD_DV, HEAD_DV).astype(BF16)
    zeros_rw = jnp.zeros((1, RWKV_WIDTH), F32)

    x2d = x.reshape(T, D)
    rwp_first = None
    for l in range(L):
        has_vres = l > 0
        win = jnp.concatenate([w_in[l], gla_a1[l], jnp.zeros((D, 128 - gla_a1.shape[2]), F32)],
                              axis=1).astype(BF16)
        ga2 = jnp.zeros((128, 128), F32).at[:gla_a2.shape[1]].set(gla_a2[l]).astype(BF16)
        mux = jnp.concatenate([rk_mu_x[l], rk_mu_v[l - 1][None] if has_vres else jnp.zeros((1, D), F32)], axis=0)
        vec = jnp.stack([rk_mu_rkv[l, 0], rk_mu_rkv[l, 1], rk_mu_rkv[l, 2], rk_w0[l], rk_a0[l],
                         rk_v0[l - 1] if has_vres else zeros_rw[0], rk_k_k[l], rk_k_a[l]], axis=0)
        extra = {}
        if has_vres:
            extra = dict(v1=rk_v1[l - 1].astype(BF16), v2=rk_v2[l - 1].astype(BF16), rwp_first=rwp_first)
        rwp, gr = _prep(x2d, mod[l], S, tm, has_vres, norm1_g[l][None], win, mux,
                        rk_w1[l].astype(BF16), rk_w2[l].astype(BF16), rk_a1[l].astype(BF16),
                        rk_a2[l].astype(BF16), rk_g1[l].astype(BF16), rk_g2[l].astype(BF16),
                        ga2, gla_ab[l][None], vec, obd, cos_t, sin_t, **extra)
        if l == 0:
            rwp_first = rwp
        par = jnp.stack([rk_r_k[l].reshape(-1), rk_ln_g[l], rk_ln_b[l]], axis=0)
        ya = _rwkv(rwp, par, B, S, RWKV_CHUNKS_PER_STEP)
        ybc = _glaret(gr, jnp.tile(gla_ln_g[l], GLA_HEADS)[None], dec, kdec, qdec, cdec, B, S,
                      GLARET_CHUNKS_PER_STEP)
        x2d = _outffn(x2d, ya, ybc, mod[l], S, tm, l == L - 1, norm2_g[l][None],
                      w_out[l].astype(BF16), ffn_w_gate[l].astype(BF16), ffn_w_up[l].astype(BF16),
                      ffn_w_down[l].astype(BF16), norm_f_g[None])
    return x2d.reshape(B, S, D)
```

```python
import functools

import numpy as np
import jax
import jax.numpy as jnp
from jax import lax
from jax.experimental import pallas as pl
from jax.experimental.pallas import tpu as pltpu

F32 = jnp.float32
BF16 = jnp.bfloat16

CHUNK = 64
EPS = 1e-6
HEAD_DV = 64
RWKV_HEADS = 8
RWKV_WIDTH = RWKV_HEADS * HEAD_DV
RWKV_GN_EPS = 64e-5
GLA_HEADS = 4
GLA_DK = 32
GLA_GATE_TAU = 16.0
RET_HEADS = 4
RET_DK = 32
ROPE_BASE = 10000.0
HEADS_PER_GROUP = 4
GROUP_W = HEADS_PER_GROUP * HEAD_DV
VMEM_LIMIT_BYTES = 56 * 1024 * 1024
RWKV_CHUNKS_PER_STEP = 4
GLARET_CHUNKS_PER_STEP = 4
PREP_TOKENS_PER_STEP = 256
FFN_TOKENS_PER_STEP = 512

_GQ, _GK, _GLA, _GV, _GG, _RQ, _RK, _RV, _RG, _GR_W = 0, 128, 256, 384, 640, 896, 1024, 1152, 1408, 1664


def _dot(a, b):
    return jnp.dot(a.astype(BF16), b.astype(BF16), preferred_element_type=F32)


def _dot_nt(a, b):
    return lax.dot_general(a.astype(BF16), b.astype(BF16), (((1,), (1,)), ((), ())),
                           preferred_element_type=F32)


def _dot_tn(a, b):
    return lax.dot_general(a.astype(BF16), b.astype(BF16), (((0,), (0,)), ((), ())),
                           preferred_element_type=F32)


def _group_sums(xs, ones_bd):
    rows = [x.shape[0] for x in xs]
    hi = [x.astype(BF16) for x in xs]
    lo = [(x - h.astype(F32)).astype(BF16) for x, h in zip(xs, hi)]
    s = jnp.dot(jnp.concatenate(hi + lo, axis=0), ones_bd.astype(BF16), preferred_element_type=F32)
    total = sum(rows)
    out, off = [], 0
    for n in rows:
        out.append(s[off:off + n] + s[total + off:total + off + n])
        off += n
    return out


def _cumsum_rows(x):
    n = x.shape[0]
    row = _iota(x.shape, 0)
    s = 1
    while s < n:
        x = x + jnp.where(row >= s, pltpu.roll(x, s, 0), 0.0)
        s *= 2
    return x


def _sigmoid(x):
    return 1.0 / (1.0 + jnp.exp(-x))


def _softplus(x):
    return jnp.maximum(x, 0.0) + jnp.log(1.0 + jnp.exp(-jnp.abs(x)))


def _iota(shape, axis):
    return lax.broadcasted_iota(jnp.int32, shape, axis)


def _bd_rows(x, groups):
    c, w = x.shape
    n = w // groups
    t = jnp.concatenate([x] * groups, axis=0)
    keep = (_iota(t.shape, 0) // c) == (_iota(t.shape, 1) // n)
    return jnp.where(keep, t, 0.0)


def _block_mask(rows, cols, rblk, cblk):
    return (_iota((rows, cols), 0) // rblk) == (_iota((rows, cols), 1) // cblk)


def _const_spec(shape):
    nd = len(shape)
    return pl.BlockSpec(shape, lambda *_: (0,) * nd, pipeline_mode=pl.Buffered(1))


def _adaln_kernel(c_ref, w_ref, b_ref, o_ref):
    c = c_ref[...]
    cond = c * _sigmoid(c)
    o_ref[0] = _dot(cond, w_ref[0]) + b_ref[0]


def _adaln(c, ada_w, ada_b):
    L, D, D6 = ada_w.shape
    B = c.shape[0]
    tn = 1536
    return pl.pallas_call(
        _adaln_kernel,
        grid=(L, D6 // tn),
        in_specs=[pl.BlockSpec((B, D), lambda l, j: (0, 0)),
                  pl.BlockSpec((1, D, tn), lambda l, j: (l, 0, j)),
                  pl.BlockSpec((1, 1, tn), lambda l, j: (l, 0, j))],
        out_specs=pl.BlockSpec((1, B, tn), lambda l, j: (l, 0, j)),
        out_shape=jax.ShapeDtypeStruct((L, B, D6), F32),
        compiler_params=pltpu.CompilerParams(
            dimension_semantics=("arbitrary", "arbitrary"), vmem_limit_bytes=VMEM_LIMIT_BYTES),
        name="adaln_mod",
    )(c, ada_w, ada_b.reshape(L, 1, D6))


def _prep_kernel(tiles_per_seq, has_vres, *refs):
    if has_vres:
        (x_ref, xh_ref, mod_ref, n1g_ref, win_ref, mux_ref, w1_ref, w2_ref, a1_ref, a2_ref,
         g1_ref, g2_ref, ga2_ref, gab_ref, vec_ref, obd_ref, cos_ref, sin_ref,
         v1_ref, v2_ref, vf_ref, rwp_ref, gr_ref) = refs
    else:
        (x_ref, xh_ref, mod_ref, n1g_ref, win_ref, mux_ref, w1_ref, w2_ref, a1_ref, a2_ref,
         g1_ref, g2_ref, ga2_ref, gab_ref, vec_ref, obd_ref, cos_ref, sin_ref,
         rwp_ref, gr_ref) = refs
    tm = x_ref.shape[0]
    rw = RWKV_WIDTH
    first = (pl.program_id(0) % tiles_per_seq) == 0

    xe = jnp.concatenate([xh_ref[...], x_ref[...]], axis=0)
    mod = mod_ref[0]
    ms = jnp.mean(xe * xe, axis=-1, keepdims=True)
    he = xe * lax.rsqrt(ms + EPS) * n1g_ref[...]
    he = he * (1.0 + mod[1:2]) + mod[0:1]
    halo = jnp.logical_and(_iota((tm + 8, 1), 0) < 8, first)
    he = jnp.where(halo, 0.0, he)

    proj = _dot(he, win_ref[...])
    p = proj[8:]
    ps = pltpu.roll(proj[:, :3 * rw], 1, 0)[8:]
    h = he[8:]
    dh = pltpu.roll(he, 1, 0)[8:] - h
    mux = mux_ref[...]
    vec = vec_ref[...]

    z = vec[3:4] + _dot(jnp.tanh(_dot(h + dh * mux[0:1], w1_ref[...])), w2_ref[...])
    lw = -jnp.exp(-_softplus(-z) - 0.5)
    a = _sigmoid(vec[4:5] + _dot(_dot(h + dh * mux[1:2], a1_ref[...]), a2_ref[...]))
    g = _dot(_sigmoid(_dot(h + dh * mux[2:3], g1_ref[...])), g2_ref[...])

    r = p[:, 0:rw]
    r = r + (ps[:, 0:rw] - r) * vec[0:1]
    k = p[:, rw:2 * rw]
    k = k + (ps[:, rw:2 * rw] - k) * vec[1:2]
    v = p[:, 2 * rw:3 * rw]
    v = v + (ps[:, 2 * rw:3 * rw] - v) * vec[2:3]

    kk = k * vec[6:7]
    kk2 = kk * kk
    ss = jnp.concatenate(_group_sums([kk2[:, :GROUP_W], kk2[:, GROUP_W:]], obd_ref[...]), axis=1)
    kk = kk / jnp.maximum(jnp.sqrt(ss), 1e-12)
    k = k * (1.0 + (a - 1.0) * vec[7:8])
    if has_vres:
        vg = _sigmoid(vec[5:6] + _dot(_dot(h + dh * mux[3:4], v1_ref[...]), v2_ref[...]))
        v = v + (vf_ref[0] - v) * vg

    rwp_ref[0] = r
    rwp_ref[1] = lw
    rwp_ref[2] = k
    rwp_ref[3] = v
    rwp_ref[4] = kk
    rwp_ref[5] = kk * a
    rwp_ref[6] = g

    o = 3 * rw
    gr_ref[:, _GQ:_GK] = p[:, o:o + 128] * (GLA_DK ** -0.5)
    gr_ref[:, _GK:_GLA] = p[:, o + 128:o + 256]
    la_pre = _dot(p[:, 3072:3200], ga2_ref[...]) + gab_ref[...]
    gr_ref[:, _GLA:_GV] = -_softplus(-la_pre) * (1.0 / GLA_GATE_TAU)
    gr_ref[:, _GV:_GG] = p[:, o + 256:o + 512]
    gate = p[:, o + 512:o + 768]
    gr_ref[:, _GG:_RQ] = gate * _sigmoid(gate)

    o = o + 768
    cos = cos_ref[...]
    sin = sin_ref[...]
    lo_half = (_iota((tm, 128), 1) % RET_DK) < (RET_DK // 2)

    def rope(t):
        swapped = jnp.where(lo_half, pltpu.roll(t, 128 - RET_DK // 2, 1), pltpu.roll(t, RET_DK // 2, 1))
        return t * cos + swapped * sin

    gr_ref[:, _RQ:_RK] = rope(p[:, o:o + 128]) * (RET_DK ** -0.5)
    gr_ref[:, _RK:_RV] = rope(p[:, o + 128:o + 256])
    gr_ref[:, _RV:_RG] = p[:, o + 256:o + 512]
    gate = p[:, o + 512:o + 768]
    gr_ref[:, _RG:_GR_W] = gate * _sigmoid(gate)


def _prep(x2d, mod_l, seq, tm, has_vres, n1g, win, mux, w1, w2, a1, a2, g1, g2, ga2, gab, vec, obd,
          cos_t, sin_t, v1=None, v2=None, rwp_first=None):
    T, D = x2d.shape
    tiles_per_seq = seq // tm
    n_tiles = T // tm
    consts = [n1g, win, mux, w1, w2, a1, a2, g1, g2, ga2, gab, vec, obd]
    in_specs = [pl.BlockSpec((tm, D), lambda i: (i, 0)),
                pl.BlockSpec((8, D), lambda i: (jnp.maximum(i * (tm // 8) - 1, 0), 0)),
                pl.BlockSpec((1, 6, D), lambda i: (i // tiles_per_seq, 0, 0))]
    in_specs += [_const_spec(a.shape) for a in consts]
    in_specs += [pl.BlockSpec((tm, 128), lambda i: (i % tiles_per_seq, 0)),
                 pl.BlockSpec((tm, 128), lambda i: (i % tiles_per_seq, 0))]
    args = [x2d, x2d, mod_l] + consts + [cos_t, sin_t]
    if has_vres:
        in_specs += [_const_spec(v1.shape), _const_spec(v2.shape),
                     pl.BlockSpec((1, tm, RWKV_WIDTH), lambda i: (3, i, 0))]
        args += [v1, v2, rwp_first]
    return pl.pallas_call(
        functools.partial(_prep_kernel, tiles_per_seq, has_vres),
        grid=(n_tiles,),
        in_specs=in_specs,
        out_specs=[pl.BlockSpec((7, tm, RWKV_WIDTH), lambda i: (0, i, 0)),
                   pl.BlockSpec((tm, _GR_W), lambda i: (i, 0))],
        out_shape=[jax.ShapeDtypeStruct((7, T, RWKV_WIDTH), F32),
                   jax.ShapeDtypeStruct((T, _GR_W), F32)],
        compiler_params=pltpu.CompilerParams(
            dimension_semantics=("parallel",), vmem_limit_bytes=VMEM_LIMIT_BYTES),
        name="proj_prep",
    )(*args)


def _chunk_masks(groups):
    C = CHUNK
    row = _iota((C, groups * C), 0)
    col = _iota((C, groups * C), 1) % C
    return row > col, row >= col, (row == col).astype(F32)


def _rwkv_chunk_factors(units, masks, bdm, eye):
    C, G = CHUNK, HEADS_PER_GROUP
    strict, incl, ident = masks
    n = range(len(units))
    r, lw, k, v, kk, kb = ([u[i] for u in units] for i in range(6))
    cl = [_cumsum_rows(lw[i]) for i in n]
    cle = [cl[i][C - 1:C] for i in n]
    e_neg = [jnp.exp(-cl[i]) for i in n]
    e_end = [jnp.exp(cle[i] - cl[i]) for i in n]
    at = [-kk[i] * jnp.exp(cl[i] - lw[i]) for i in n]
    rt = [r[i] * jnp.exp(cl[i]) for i in n]
    ar = [jnp.concatenate([at[i], rt[i]], axis=0) for i in n]
    sb = [_dot_nt(ar[i], _bd_rows(kb[i] * e_neg[i], G)) for i in n]
    sk = [_dot_nt(ar[i], _bd_rows(k[i] * e_neg[i], G)) for i in n]
    a_rb = [jnp.where(incl, sb[i][C:], 0.0) for i in n]
    a_ak = [jnp.where(strict, sk[i][:C], 0.0) for i in n]
    a_rk = [jnp.where(incl, sk[i][C:], 0.0) for i in n]

    a_ab = [jnp.where(strict, sb[i][:C], 0.0) for i in n]
    tm = [ident + a_ab[i] for i in n]
    xp = [_dot(a_ab[i], _bd_rows(a_ab[i], G)) for i in n]
    for _ in range(4):
        rr = [_dot(jnp.concatenate([tm[i], xp[i]], axis=0), _bd_rows(xp[i], G)) for i in n]
        tm = [tm[i] + rr[i][:C] for i in n]
        xp = [rr[i][C:] for i in n]
    tm = [tm[i] + _dot(tm[i], _bd_rows(xp[i], G)) for i in n]

    vv = [_dot(jnp.concatenate([a_ak[i], a_rk[i]], axis=0), _bd_rows(v[i], G)) for i in n]
    wt = [_dot(tm[i], _bd_rows(at[i], G)) for i in n]
    ut = [_dot(tm[i], _bd_rows(vv[i][:C], G)) for i in n]
    qh = [rt[i] + _dot(a_rb[i], _bd_rows(wt[i], G)) for i in n]
    yh = [_dot(a_rb[i], _bd_rows(ut[i], G)) + vv[i][C:] for i in n]
    zero = jnp.zeros((C, GROUP_W), F32)
    pz = [_dot_tn(jnp.concatenate([kb[i] * e_end[i], k[i] * e_end[i]], axis=0),
                  jnp.concatenate([jnp.concatenate([wt[i], ut[i]], axis=1),
                                   jnp.concatenate([zero, v[i]], axis=1)], axis=0)) for i in n]
    pm = [jnp.where(bdm, pz[i][:, :GROUP_W], 0.0) + jnp.where(eye, jnp.exp(cle[i]), 0.0) for i in n]
    zm = [jnp.where(bdm, pz[i][:, GROUP_W:], 0.0) for i in n]
    return qh, yh, pm, zm


def _rwkv_kernel(cb, rwp_ref, par_ref, o_ref, st_ref):
    C, W, N = CHUNK, GROUP_W, HEAD_DV
    ngroups = o_ref.shape[1] // W

    @pl.when(pl.program_id(1) == 0)
    def _():
        st_ref[...] = jnp.zeros_like(st_ref)

    masks = _chunk_masks(HEADS_PER_GROUP)
    bdm = _block_mask(W, W, N, N)
    eye = _iota((W, W), 0) == _iota((W, W), 1)
    ones_bd = bdm.astype(F32)

    where = [(slice(j * C, (j + 1) * C), slice(gi * W, (gi + 1) * W))
             for j in range(cb) for gi in range(ngroups)]
    qh, yh, pm, zm = _rwkv_chunk_factors(
        [[rwp_ref[i, rows, lanes] for i in range(6)] for rows, lanes in where], masks, bdm, eye)

    m = [st_ref[gi] for gi in range(ngroups)]
    y = []
    for j in range(cb):
        for gi in range(ngroups):
            u = j * ngroups + gi
            ym = _dot(jnp.concatenate([qh[u], pm[u]], axis=0), m[gi])
            y.append(ym[:C] + yh[u])
            m[gi] = ym[C:] + zm[u]
    for gi in range(ngroups):
        st_ref[gi] = m[gi]

    n = range(len(where))
    par = [par_ref[:, lanes] for _, lanes in where]
    rkr = [rwp_ref[0, rows, lanes] * rwp_ref[2, rows, lanes] * par[u][0:1]
           for u, (rows, lanes) in enumerate(where)]
    sums = _group_sums(y + rkr, ones_bd)
    yc = [y[u] - sums[u] * (1.0 / N) for u in n]
    var = _group_sums([yc[u] * yc[u] for u in n], ones_bd)
    for u, (rows, lanes) in enumerate(where):
        yn = yc[u] * lax.rsqrt(var[u] * (1.0 / N) + RWKV_GN_EPS) * par[u][1:2] + par[u][2:3]
        o_ref[rows, lanes] = (yn + sums[len(where) + u] * rwp_ref[3, rows, lanes]) * rwp_ref[6, rows, lanes]


def _rwkv(rwp, par, batch, seq, cb):
    _, T, RW = rwp.shape
    steps = seq // (CHUNK * cb)
    return pl.pallas_call(
        functools.partial(_rwkv_kernel, cb),
        grid=(batch, steps),
        in_specs=[pl.BlockSpec((7, cb * CHUNK, RW), lambda b, c: (0, b * steps + c, 0)),
                  _const_spec(par.shape)],
        out_specs=pl.BlockSpec((cb * CHUNK, RW), lambda b, c: (b * steps + c, 0)),
        out_shape=jax.ShapeDtypeStruct((T, RW), F32),
        scratch_shapes=[pltpu.VMEM((RW // GROUP_W, GROUP_W, GROUP_W), F32)],
        compiler_params=pltpu.CompilerParams(
            dimension_semantics=("parallel", "arbitrary"), vmem_limit_bytes=VMEM_LIMIT_BYTES),
        name="rwkv7_chunk",
    )(rwp, par)


def _glaret_kernel(cb, gr_ref, lng_ref, dec_ref, kdec_ref, qdec_ref, cdec_ref, o_ref, sg_ref, sr_ref):
    C, G, DV = CHUNK, GLA_HEADS, HEAD_DV
    WV = G * DV

    @pl.when(pl.program_id(1) == 0)
    def _():
        sg_ref[...] = jnp.zeros_like(sg_ref)
        sr_ref[...] = jnp.zeros_like(sr_ref)

    _, incl, _ = _chunk_masks(G)
    ones_bd = _block_mask(WV, WV, DV, DV).astype(F32)
    st_mask = _block_mask(WV, G * GLA_DK, DV, GLA_DK)
    lng = lng_ref[...]
    dec = dec_ref[...]
    kdec = kdec_ref[...]
    qdec = qdec_ref[...]
    cdec = cdec_ref[...]

    n = range(cb)
    rows = [slice(j * C, (j + 1) * C) for j in n]
    gq = [gr_ref[rows[j], _GQ:_GK] for j in n]
    gk = [gr_ref[rows[j], _GK:_GLA] for j in n]
    gv = [gr_ref[rows[j], _GV:_GG] for j in n]
    bc = [_cumsum_rows(gr_ref[rows[j], _GLA:_GV]) for j in n]
    be = [bc[j][C - 1:C] for j in n]
    ep = [jnp.exp(bc[j]) for j in n]
    en = [jnp.exp(-bc[j]) for j in n]
    qp = [gq[j] * ep[j] for j in n]
    att_lo = [_dot_nt(qp[j], _bd_rows(gk[j] * en[j], G)) for j in n]
    att_hi = [_dot_nt(gq[j] * en[j], _bd_rows(gk[j] * ep[j], G)) for j in n]
    rq = [gr_ref[rows[j], _RQ:_RK] for j in n]
    rk = [gr_ref[rows[j], _RK:_RV] for j in n]
    rv = [gr_ref[rows[j], _RV:_RG] for j in n]
    sc = [_dot_nt(rq[j], _bd_rows(rk[j], G)) * dec for j in n]
    g_intra = [_dot(jnp.where(incl, att_lo[j], att_hi[j]), _bd_rows(gv[j], G)) for j in n]
    r_intra = [_dot(sc[j], _bd_rows(rv[j], G)) for j in n]
    g_kv = [jnp.where(st_mask, _dot_tn(gv[j], gk[j] * jnp.exp(be[j] - bc[j])), 0.0) for j in n]
    r_kv = [jnp.where(st_mask, _dot_tn(rv[j], rk[j] * kdec), 0.0) for j in n]

    sg = sg_ref[...]
    sr = sr_ref[...]
    g_o = []
    r_o = []
    for j in n:
        g_o.append(g_intra[j] + _dot_nt(qp[j], sg))
        sg = sg * jnp.exp(be[j]) + g_kv[j]
        r_o.append(r_intra[j] + _dot_nt(rq[j] * qdec, sr))
        sr = sr * cdec + r_kv[j]
    sg_ref[...] = sg
    sr_ref[...] = sr

    sums = _group_sums([g_o[j] * g_o[j] for j in n] + r_o, ones_bd)
    r_c = [r_o[j] - sums[cb + j] * (1.0 / DV) for j in n]
    r_var = _group_sums([r_c[j] * r_c[j] for j in n], ones_bd)
    for j in n:
        o_ref[rows[j], 0:WV] = (g_o[j] * lax.rsqrt(sums[j] * (1.0 / DV) + EPS) * lng
                                * gr_ref[rows[j], _GG:_RQ])
        o_ref[rows[j], WV:2 * WV] = (r_c[j] * lax.rsqrt(r_var[j] * (1.0 / DV) + EPS)
                                     * gr_ref[rows[j], _RG:_GR_W])


def _glaret(gr, lng, dec, kdec, qdec, cdec, batch, seq, cb):
    T = gr.shape[0]
    nc = seq // (CHUNK * cb)
    wv = GLA_HEADS * HEAD_DV
    consts = [lng, dec, kdec, qdec, cdec]
    return pl.pallas_call(
        functools.partial(_glaret_kernel, cb),
        grid=(batch, nc),
        in_specs=[pl.BlockSpec((cb * CHUNK, _GR_W), lambda b, c: (b * nc + c, 0))]
                 + [_const_spec(a.shape) for a in consts],
        out_specs=pl.BlockSpec((cb * CHUNK, 2 * wv), lambda b, c: (b * nc + c, 0)),
        out_shape=jax.ShapeDtypeStruct((T, 2 * wv), F32),
        scratch_shapes=[pltpu.VMEM((wv, GLA_HEADS * GLA_DK), F32),
                        pltpu.VMEM((wv, RET_HEADS * RET_DK), F32)],
        compiler_params=pltpu.CompilerParams(
            dimension_semantics=("parallel", "arbitrary"), vmem_limit_bytes=VMEM_LIMIT_BYTES),
        name="gla_retention_chunk",
    )(gr, *consts)


def _outffn_kernel(final_norm, x_ref, ya_ref, ybc_ref, mod_ref, n2g_ref, wo_ref, wg_ref, wu_ref,
                   wd_ref, nfg_ref, o_ref):
    mod = mod_ref[0]
    y = jnp.concatenate([ya_ref[...], ybc_ref[...]], axis=-1)
    x = x_ref[...] + mod[2:3] * _dot(y, wo_ref[...])
    ms = jnp.mean(x * x, axis=-1, keepdims=True)
    h = x * lax.rsqrt(ms + EPS) * n2g_ref[...]
    h = (h * (1.0 + mod[4:5]) + mod[3:4]).astype(BF16)
    gate = jnp.dot(h, wg_ref[...], preferred_element_type=F32)
    up = jnp.dot(h, wu_ref[...], preferred_element_type=F32)
    x = x + mod[5:6] * _dot(gate * _sigmoid(gate) * up, wd_ref[...])
    if final_norm:
        ms = jnp.mean(x * x, axis=-1, keepdims=True)
        x = x * lax.rsqrt(ms + EPS) * nfg_ref[...]
    o_ref[...] = x


def _outffn(x2d, ya, ybc, mod_l, seq, tm, final_norm, n2g, wo, wg, wu, wd, nfg):
    T, D = x2d.shape
    tiles_per_seq = seq // tm
    consts = [n2g, wo, wg, wu, wd, nfg]
    return pl.pallas_call(
        functools.partial(_outffn_kernel, final_norm),
        grid=(T // tm,),
        in_specs=[pl.BlockSpec((tm, D), lambda i: (i, 0)),
                  pl.BlockSpec((tm, ya.shape[1]), lambda i: (i, 0)),
                  pl.BlockSpec((tm, ybc.shape[1]), lambda i: (i, 0)),
                  pl.BlockSpec((1, 6, D), lambda i: (i // tiles_per_seq, 0, 0))]
                 + [_const_spec(a.shape) for a in consts],
        out_specs=pl.BlockSpec((tm, D), lambda i: (i, 0)),
        out_shape=jax.ShapeDtypeStruct((T, D), F32),
        compiler_params=pltpu.CompilerParams(
            dimension_semantics=("parallel",), vmem_limit_bytes=VMEM_LIMIT_BYTES),
        name="outproj_swiglu",
    )(x2d, ya, ybc, mod_l, *consts)


def _rope_tables(seq):
    half = RET_DK // 2
    inv_freq = ROPE_BASE ** (-jnp.arange(half, dtype=F32) / half)
    ang = jnp.arange(seq, dtype=F32)[:, None] * inv_freq[None, :]
    cos, sin = jnp.cos(ang), jnp.sin(ang)
    cos_t = jnp.tile(jnp.concatenate([cos, cos], axis=-1), (1, RET_HEADS))
    sin_t = jnp.tile(jnp.concatenate([-sin, sin], axis=-1), (1, RET_HEADS))
    return cos_t, sin_t


def _retention_tables():
    H, C = RET_HEADS, CHUNK
    log_gamma = jnp.log1p(-(2.0 ** (-5.0 - jnp.arange(H, dtype=F32))))
    pos = jnp.arange(C, dtype=F32)
    intra = jnp.exp(log_gamma[:, None, None] * jnp.abs(pos[:, None] - pos[None, :]))
    dec = jnp.transpose(intra, (1, 0, 2)).reshape(C, H * C)
    k_dec = jnp.exp(log_gamma[None, :] * (C - 1.0 - pos)[:, None])
    q_dec = jnp.exp(log_gamma[None, :] * (pos + 1.0)[:, None])
    chunk_dec = jnp.exp(log_gamma * C)
    kdec = jnp.repeat(k_dec, RET_DK, axis=1)
    qdec = jnp.repeat(q_dec, RET_DK, axis=1)
    cdec = jnp.repeat(chunk_dec, RET_DK)[None, :]
    return dec, kdec, qdec, cdec


def kernel(x, c, ada_w, ada_b, norm1_g, norm2_g, w_in, w_out, rk_mu_rkv, rk_mu_x, rk_w0, rk_w1, rk_w2, rk_a0, rk_a1, rk_a2, rk_g1, rk_g2, rk_k_k, rk_k_a, rk_r_k, rk_ln_g, rk_ln_b, rk_mu_v, rk_v0, rk_v1, rk_v2, gla_a1, gla_a2, gla_ab, gla_ln_g, ffn_w_gate, ffn_w_up, ffn_w_down, norm_f_g):
    B, S, D = x.shape
    L = ada_w.shape[0]
    T = B * S
    assert S % PREP_TOKENS_PER_STEP == 0 and S % FFN_TOKENS_PER_STEP == 0
    assert S % (CHUNK * RWKV_CHUNKS_PER_STEP) == 0 and S % (CHUNK * GLARET_CHUNKS_PER_STEP) == 0

    mod = _adaln(c, ada_w, ada_b).reshape(L, B, 6, D)
    cos_t, sin_t = _rope_tables(S)
    dec, kdec, qdec, cdec = _retention_tables()
    obd = _block_mask(GROUP_W, GROUP_W, HEAD_DV, HEAD_DV).astype(BF16)
    zeros_rw = jnp.zeros((1, RWKV_WIDTH), F32)

    x2d = x.reshape(T, D)
    rwp_first = None
    for l in range(L):
        has_vres = l > 0
        win = jnp.concatenate([w_in[l], gla_a1[l], jnp.zeros((D, 128 - gla_a1.shape[2]), F32)],
                              axis=1).astype(BF16)
        ga2 = jnp.zeros((128, 128), F32).at[:gla_a2.shape[1]].set(gla_a2[l]).astype(BF16)
        mux = jnp.concatenate([rk_mu_x[l], rk_mu_v[l - 1][None] if has_vres else jnp.zeros((1, D), F32)], axis=0)
        vec = jnp.stack([rk_mu_rkv[l, 0], rk_mu_rkv[l, 1], rk_mu_rkv[l, 2], rk_w0[l], rk_a0[l],
                         rk_v0[l - 1] if has_vres else zeros_rw[0], rk_k_k[l], rk_k_a[l]], axis=0)
        extra = {}
        if has_vres:
            extra = dict(v1=rk_v1[l - 1].astype(BF16), v2=rk_v2[l - 1].astype(BF16), rwp_first=rwp_first)
        rwp, gr = _prep(x2d, mod[l], S, PREP_TOKENS_PER_STEP, has_vres, norm1_g[l][None], win, mux,
                        rk_w1[l].astype(BF16), rk_w2[l].astype(BF16), rk_a1[l].astype(BF16),
                        rk_a2[l].astype(BF16), rk_g1[l].astype(BF16), rk_g2[l].astype(BF16),
                        ga2, gla_ab[l][None], vec, obd, cos_t, sin_t, **extra)
        if l == 0:
            rwp_first = rwp
        par = jnp.stack([rk_r_k[l].reshape(-1), rk_ln_g[l], rk_ln_b[l]], axis=0)
        ya = _rwkv(rwp, par, B, S, RWKV_CHUNKS_PER_STEP)
        ybc = _glaret(gr, jnp.tile(gla_ln_g[l], GLA_HEADS)[None], dec, kdec, qdec, cdec, B, S,
                      GLARET_CHUNKS_PER_STEP)
        x2d = _outffn(x2d, ya, ybc, mod[l], S, FFN_TOKENS_PER_STEP, l == L - 1, norm2_g[l][None],
                      w_out[l].astype(BF16), ffn_w_gate[l].astype(BF16), ffn_w_up[l].astype(BF16),
                      ffn_w_down[l].astype(BF16), norm_f_g[None])
    return x2d.reshape(B, S, D)
```

```python
import functools

import numpy as np
import jax
import jax.numpy as jnp
from jax import lax
from jax.experimental import pallas as pl
from jax.experimental.pallas import tpu as pltpu

F32 = jnp.float32
BF16 = jnp.bfloat16

CHUNK = 64
EPS = 1e-6
HEAD_DV = 64
RWKV_HEADS = 8
RWKV_WIDTH = RWKV_HEADS * HEAD_DV
RWKV_GN_EPS = 64e-5
GLA_HEADS = 4
GLA_DK = 32
GLA_GATE_TAU = 16.0
RET_HEADS = 4
RET_DK = 32
ROPE_BASE = 10000.0
HEADS_PER_GROUP = 4
GROUP_W = HEADS_PER_GROUP * HEAD_DV
VMEM_LIMIT_BYTES = 56 * 1024 * 1024
RWKV_CHUNKS_PER_STEP = 4
GLARET_CHUNKS_PER_STEP = 4
PREP_TOKENS_PER_STEP = 256
FFN_TOKENS_PER_STEP = 512

_GQ, _GK, _GLA, _GV, _GG, _RQ, _RK, _RV, _RG, _GR_W = 0, 128, 256, 384, 640, 896, 1024, 1152, 1408, 1664
_W_MISC, _W_LORA_H, _W_LORA_S, _W_IN, _WIN_W = 0, 128, 384, 640, 3712


def _dot(a, b):
    return jnp.dot(a.astype(BF16), b.astype(BF16), preferred_element_type=F32)


def _dot_nt(a, b):
    return lax.dot_general(a.astype(BF16), b.astype(BF16), (((1,), (1,)), ((), ())),
                           preferred_element_type=F32)


def _dot_tn(a, b):
    return lax.dot_general(a.astype(BF16), b.astype(BF16), (((0,), (0,)), ((), ())),
                           preferred_element_type=F32)


def _group_sums(xs, ones_bd):
    rows = [x.shape[0] for x in xs]
    hi = [x.astype(BF16) for x in xs]
    lo = [(x - h.astype(F32)).astype(BF16) for x, h in zip(xs, hi)]
    s = jnp.dot(jnp.concatenate(hi + lo, axis=0), ones_bd.astype(BF16), preferred_element_type=F32)
    total = sum(rows)
    out, off = [], 0
    for n in rows:
        out.append(s[off:off + n] + s[total + off:total + off + n])
        off += n
    return out


def _cumsum_rows(x):
    n = x.shape[0]
    row = _iota(x.shape, 0)
    s = 1
    while s < n:
        x = x + jnp.where(row >= s, pltpu.roll(x, s, 0), 0.0)
        s *= 2
    return x


def _sigmoid(x):
    return 1.0 / (1.0 + jnp.exp(-x))


def _softplus(x):
    return jnp.maximum(x, 0.0) + jnp.log(1.0 + jnp.exp(-jnp.abs(x)))


def _iota(shape, axis):
    return lax.broadcasted_iota(jnp.int32, shape, axis)


def _bd_rows(x, groups):
    c, w = x.shape
    n = w // groups
    t = jnp.concatenate([x] * groups, axis=0)
    keep = (_iota(t.shape, 0) // c) == (_iota(t.shape, 1) // n)
    return jnp.where(keep, t, 0.0)


def _block_mask(rows, cols, rblk, cblk):
    return (_iota((rows, cols), 0) // rblk) == (_iota((rows, cols), 1) // cblk)


def _const_spec(shape):
    nd = len(shape)
    return pl.BlockSpec(shape, lambda *_: (0,) * nd, pipeline_mode=pl.Buffered(1))


def _layer_spec(shape, layer):
    nd = len(shape)
    return pl.BlockSpec((None,) + tuple(shape[1:]), lambda *_: (layer,) + (0,) * (nd - 1),
                        pipeline_mode=pl.Buffered(1))


def _adaln_kernel(c_ref, w_ref, b_ref, o_ref):
    c = c_ref[...]
    cond = c * _sigmoid(c)
    o_ref[0] = _dot(cond, w_ref[0]) + b_ref[0]


def _adaln(c, ada_w, ada_b):
    L, D, D6 = ada_w.shape
    B = c.shape[0]
    tn = 1536
    return pl.pallas_call(
        _adaln_kernel,
        grid=(L, D6 // tn),
        in_specs=[pl.BlockSpec((B, D), lambda l, j: (0, 0)),
                  pl.BlockSpec((1, D, tn), lambda l, j: (l, 0, j)),
                  pl.BlockSpec((1, 1, tn), lambda l, j: (l, 0, j))],
        out_specs=pl.BlockSpec((1, B, tn), lambda l, j: (l, 0, j)),
        out_shape=jax.ShapeDtypeStruct((L, B, D6), F32),
        compiler_params=pltpu.CompilerParams(
            dimension_semantics=("arbitrary", "arbitrary"), vmem_limit_bytes=VMEM_LIMIT_BYTES),
        name="adaln_mod",
    )(c, ada_w, ada_b.reshape(L, 1, D6))


def _prep_kernel(tiles_per_seq, has_vres, *refs):
    if has_vres:
        (x_ref, xh_ref, mod_ref, n1g_ref, win_ref, w2_ref, ga2_ref, gab_ref, vec_ref, obd_ref,
         cos_ref, sin_ref, v2_ref, vf_ref, rwp_ref, gr_ref) = refs
    else:
        (x_ref, xh_ref, mod_ref, n1g_ref, win_ref, w2_ref, ga2_ref, gab_ref, vec_ref, obd_ref,
         cos_ref, sin_ref, rwp_ref, gr_ref) = refs
    tm = x_ref.shape[0]
    rw = RWKV_WIDTH
    first = (pl.program_id(0) % tiles_per_seq) == 0

    xe = jnp.concatenate([xh_ref[...], x_ref[...]], axis=0)
    mod = mod_ref[0]
    ms = jnp.mean(xe * xe, axis=-1, keepdims=True)
    he = xe * lax.rsqrt(ms + EPS) * (n1g_ref[...] * (1.0 + mod[1:2])) + mod[0:1]
    halo = jnp.logical_and(_iota((tm + 8, 1), 0) < 8, first)
    he = jnp.where(halo, 0.0, he)

    heb = he.astype(BF16)

    def project(lo, hi):
        return jnp.dot(heb, win_ref[:, lo:hi], preferred_element_type=F32)

    def prev_rows(t):
        return pltpu.roll(t, 1, 0)[8:]

    p_lora = project(_W_MISC, _W_IN)
    p_rkv = project(_W_IN, _W_IN + 3 * rw)
    vec = vec_ref[...]

    pre = p_lora[8:, _W_LORA_H:_W_LORA_S] + prev_rows(p_lora[:, _W_LORA_S:_W_IN])
    lane = _iota(pre.shape, 1)
    act = jnp.where(lane < 64, jnp.tanh(pre), jnp.where(lane < 128, pre, _sigmoid(pre)))
    second = _dot(act, w2_ref[...])
    p_gla = project(_W_IN + 3 * rw, _W_IN + 3 * rw + 768)
    lw = -jnp.exp(-_softplus(-(vec[3:4] + second[:, 0:rw])) - 0.5)
    a = _sigmoid(vec[4:5] + second[:, rw:2 * rw])
    g = second[:, 2 * rw:3 * rw]

    p = p_rkv[8:]
    ps = prev_rows(p_rkv)
    r = p[:, 0:rw]
    r = r + (ps[:, 0:rw] - r) * vec[0:1]
    k = p[:, rw:2 * rw]
    k = k + (ps[:, rw:2 * rw] - k) * vec[1:2]
    v = p[:, 2 * rw:3 * rw]
    v = v + (ps[:, 2 * rw:3 * rw] - v) * vec[2:3]

    kk = k * vec[6:7]
    kk2 = kk * kk
    ss = jnp.concatenate(_group_sums([kk2[:, :GROUP_W], kk2[:, GROUP_W:]], obd_ref[...]), axis=1)
    kk = kk / jnp.maximum(jnp.sqrt(ss), 1e-12)
    k = k * (1.0 + (a - 1.0) * vec[7:8])
    misc = p_lora[8:, _W_MISC:_W_LORA_H]
    if has_vres:
        mv = jnp.where(_iota(misc.shape, 1) < 64, misc, prev_rows(p_lora[:, _W_MISC:_W_LORA_H]))
        v = v + (vf_ref[0] - v) * _sigmoid(vec[5:6] + _dot(mv, v2_ref[...]))
    p_ret = project(_W_IN + 3 * rw + 768, _WIN_W)

    rwp_ref[0] = r
    rwp_ref[1] = lw
    rwp_ref[2] = k
    rwp_ref[3] = v
    rwp_ref[4] = kk
    rwp_ref[5] = kk * a
    rwp_ref[6] = g

    p = p_gla[8:]
    gr_ref[:, _GQ:_GK] = p[:, 0:128] * (GLA_DK ** -0.5)
    gr_ref[:, _GK:_GLA] = p[:, 128:256]
    la_pre = _dot(misc, ga2_ref[...]) + gab_ref[...]
    gr_ref[:, _GLA:_GV] = -_softplus(-la_pre) * (1.0 / GLA_GATE_TAU)
    gr_ref[:, _GV:_GG] = p[:, 256:512]
    gate = p[:, 512:768]
    gr_ref[:, _GG:_RQ] = gate * _sigmoid(gate)

    p = p_ret[8:]
    o = 0
    cos = cos_ref[...]
    sin = sin_ref[...]
    lo_half = (_iota((tm, 128), 1) % RET_DK) < (RET_DK // 2)

    def rope(t):
        swapped = jnp.where(lo_half, pltpu.roll(t, 128 - RET_DK // 2, 1), pltpu.roll(t, RET_DK // 2, 1))
        return t * cos + swapped * sin

    gr_ref[:, _RQ:_RK] = rope(p[:, o:o + 128]) * (RET_DK ** -0.5)
    gr_ref[:, _RK:_RV] = rope(p[:, o + 128:o + 256])
    gr_ref[:, _RV:_RG] = p[:, o + 256:o + 512]
    gate = p[:, o + 512:o + 768]
    gr_ref[:, _RG:_GR_W] = gate * _sigmoid(gate)


def _prep(x2d, mod, layer, seq, tm, has_vres, n1g, win, w2, ga2, gab, vec, obd, cos_t, sin_t,
          v2=None, rwp_first=None):
    T, D = x2d.shape
    tiles_per_seq = seq // tm
    n_tiles = T // tm
    per_layer = [n1g, win, w2, ga2, gab, vec]
    in_specs = [pl.BlockSpec((tm, D), lambda i: (i, 0)),
                pl.BlockSpec((8, D), lambda i: (jnp.maximum(i * (tm // 8) - 1, 0), 0)),
                pl.BlockSpec((None, 1, 6, D), lambda i: (layer, i // tiles_per_seq, 0, 0))]
    in_specs += [_layer_spec(a.shape, layer) for a in per_layer]
    in_specs += [_const_spec(obd.shape),
                 pl.BlockSpec((tm, 128), lambda i: (i % tiles_per_seq, 0)),
                 pl.BlockSpec((tm, 128), lambda i: (i % tiles_per_seq, 0))]
    args = [x2d, x2d, mod] + per_layer + [obd, cos_t, sin_t]
    if has_vres:
        in_specs += [_layer_spec(v2.shape, layer),
                     pl.BlockSpec((1, tm, RWKV_WIDTH), lambda i: (3, i, 0))]
        args += [v2, rwp_first]
    return pl.pallas_call(
        functools.partial(_prep_kernel, tiles_per_seq, has_vres),
        grid=(n_tiles,),
        in_specs=in_specs,
        out_specs=[pl.BlockSpec((7, tm, RWKV_WIDTH), lambda i: (0, i, 0)),
                   pl.BlockSpec((tm, _GR_W), lambda i: (i, 0))],
        out_shape=[jax.ShapeDtypeStruct((7, T, RWKV_WIDTH), F32),
                   jax.ShapeDtypeStruct((T, _GR_W), F32)],
        compiler_params=pltpu.CompilerParams(
            dimension_semantics=("parallel",), vmem_limit_bytes=VMEM_LIMIT_BYTES),
        name="proj_prep",
    )(*args)


def _chunk_masks(groups):
    C = CHUNK
    row = _iota((C, groups * C), 0)
    col = _iota((C, groups * C), 1) % C
    return row > col, row >= col, (row == col).astype(F32)


def _rwkv_chunk_factors(units, masks, bdm, eye):
    C, G = CHUNK, HEADS_PER_GROUP
    strict, incl, ident = masks
    n = range(len(units))
    r, lw, k, v, kk, kb = ([u[i] for u in units] for i in range(6))
    cl = [_cumsum_rows(lw[i]) for i in n]
    cle = [cl[i][C - 1:C] for i in n]
    e_neg = [jnp.exp(-cl[i]) for i in n]
    e_end = [jnp.exp(cle[i] - cl[i]) for i in n]
    at = [-kk[i] * jnp.exp(cl[i] - lw[i]) for i in n]
    rt = [r[i] * jnp.exp(cl[i]) for i in n]
    ar = [jnp.concatenate([at[i], rt[i]], axis=0) for i in n]
    sb = [_dot_nt(ar[i], _bd_rows(kb[i] * e_neg[i], G)) for i in n]
    sk = [_dot_nt(ar[i], _bd_rows(k[i] * e_neg[i], G)) for i in n]
    a_rb = [jnp.where(incl, sb[i][C:], 0.0) for i in n]
    a_ak = [jnp.where(strict, sk[i][:C], 0.0) for i in n]
    a_rk = [jnp.where(incl, sk[i][C:], 0.0) for i in n]

    a_ab = [jnp.where(strict, sb[i][:C], 0.0) for i in n]
    tm = [ident + a_ab[i] for i in n]
    xp = [_dot(a_ab[i], _bd_rows(a_ab[i], G)) for i in n]
    for _ in range(4):
        rr = [_dot(jnp.concatenate([tm[i], xp[i]], axis=0), _bd_rows(xp[i], G)) for i in n]
        tm = [tm[i] + rr[i][:C] for i in n]
        xp = [rr[i][C:] for i in n]
    tm = [tm[i] + _dot(tm[i], _bd_rows(xp[i], G)) for i in n]

    vv = [_dot(jnp.concatenate([a_ak[i], a_rk[i]], axis=0), _bd_rows(v[i], G)) for i in n]
    wt = [_dot(tm[i], _bd_rows(at[i], G)) for i in n]
    ut = [_dot(tm[i], _bd_rows(vv[i][:C], G)) for i in n]
    qh = [rt[i] + _dot(a_rb[i], _bd_rows(wt[i], G)) for i in n]
    yh = [_dot(a_rb[i], _bd_rows(ut[i], G)) + vv[i][C:] for i in n]
    zero = jnp.zeros((C, GROUP_W), F32)
    pz = [_dot_tn(jnp.concatenate([kb[i] * e_end[i], k[i] * e_end[i]], axis=0),
                  jnp.concatenate([jnp.concatenate([wt[i], ut[i]], axis=1),
                                   jnp.concatenate([zero, v[i]], axis=1)], axis=0)) for i in n]
    pm = [jnp.where(bdm, pz[i][:, :GROUP_W], 0.0) + jnp.where(eye, jnp.exp(cle[i]), 0.0) for i in n]
    zm = [jnp.where(bdm, pz[i][:, GROUP_W:], 0.0) for i in n]
    return qh, yh, pm, zm


def _rwkv_kernel(cb, rwp_ref, par_ref, o_ref, st_ref):
    C, W, N = CHUNK, GROUP_W, HEAD_DV
    ngroups = o_ref.shape[1] // W

    @pl.when(pl.program_id(1) == 0)
    def _():
        st_ref[...] = jnp.zeros_like(st_ref)

    masks = _chunk_masks(HEADS_PER_GROUP)
    bdm = _block_mask(W, W, N, N)
    eye = _iota((W, W), 0) == _iota((W, W), 1)
    ones_bd = bdm.astype(F32)

    where = [(slice(j * C, (j + 1) * C), slice(gi * W, (gi + 1) * W))
             for j in range(cb) for gi in range(ngroups)]
    qh, yh, pm, zm = _rwkv_chunk_factors(
        [[rwp_ref[i, rows, lanes] for i in range(6)] for rows, lanes in where], masks, bdm, eye)

    m = [st_ref[gi] for gi in range(ngroups)]
    y = []
    for j in range(cb):
        for gi in range(ngroups):
            u = j * ngroups + gi
            ym = _dot(jnp.concatenate([qh[u], pm[u]], axis=0), m[gi])
            y.append(ym[:C] + yh[u])
            m[gi] = ym[C:] + zm[u]
    for gi in range(ngroups):
        st_ref[gi] = m[gi]

    n = range(len(where))
    par = [par_ref[:, lanes] for _, lanes in where]
    rkr = [rwp_ref[0, rows, lanes] * rwp_ref[2, rows, lanes] * par[u][0:1]
           for u, (rows, lanes) in enumerate(where)]
    sums = _group_sums(y + rkr, ones_bd)
    yc = [y[u] - sums[u] * (1.0 / N) for u in n]
    var = _group_sums([yc[u] * yc[u] for u in n], ones_bd)
    for u, (rows, lanes) in enumerate(where):
        yn = yc[u] * lax.rsqrt(var[u] * (1.0 / N) + RWKV_GN_EPS) * par[u][1:2] + par[u][2:3]
        o_ref[rows, lanes] = (yn + sums[len(where) + u] * rwp_ref[3, rows, lanes]) * rwp_ref[6, rows, lanes]


def _rwkv(rwp, par, batch, seq, cb):
    _, T, RW = rwp.shape
    steps = seq // (CHUNK * cb)
    return pl.pallas_call(
        functools.partial(_rwkv_kernel, cb),
        grid=(batch, steps),
        in_specs=[pl.BlockSpec((7, cb * CHUNK, RW), lambda b, c: (0, b * steps + c, 0)),
                  _const_spec(par.shape)],
        out_specs=pl.BlockSpec((cb * CHUNK, RW), lambda b, c: (b * steps + c, 0)),
        out_shape=jax.ShapeDtypeStruct((T, RW), F32),
        scratch_shapes=[pltpu.VMEM((RW // GROUP_W, GROUP_W, GROUP_W), F32)],
        compiler_params=pltpu.CompilerParams(
            dimension_semantics=("parallel", "arbitrary"), vmem_limit_bytes=VMEM_LIMIT_BYTES),
        name="rwkv7_chunk",
    )(rwp, par)


def _glaret_kernel(cb, gr_ref, lng_ref, dec_ref, kdec_ref, qdec_ref, cdec_ref, o_ref, sg_ref, sr_ref):
    C, G, DV = CHUNK, GLA_HEADS, HEAD_DV
    WV = G * DV

    @pl.when(pl.program_id(1) == 0)
    def _():
        sg_ref[...] = jnp.zeros_like(sg_ref)
        sr_ref[...] = jnp.zeros_like(sr_ref)

    _, incl, _ = _chunk_masks(G)
    ones_bd = _block_mask(WV, WV, DV, DV).astype(F32)
    st_mask = _block_mask(WV, G * GLA_DK, DV, GLA_DK)
    lng = lng_ref[...]
    dec = dec_ref[...]
    kdec = kdec_ref[...]
    qdec = qdec_ref[...]
    cdec = cdec_ref[...]

    n = range(cb)
    rows = [slice(j * C, (j + 1) * C) for j in n]
    gq = [gr_ref[rows[j], _GQ:_GK] for j in n]
    gk = [gr_ref[rows[j], _GK:_GLA] for j in n]
    gv = [gr_ref[rows[j], _GV:_GG] for j in n]
    bc = [_cumsum_rows(gr_ref[rows[j], _GLA:_GV]) for j in n]
    be = [bc[j][C - 1:C] for j in n]
    ep = [jnp.exp(bc[j]) for j in n]
    en = [jnp.exp(-bc[j]) for j in n]
    qp = [gq[j] * ep[j] for j in n]
    att_lo = [_dot_nt(qp[j], _bd_rows(gk[j] * en[j], G)) for j in n]
    att_hi = [_dot_nt(gq[j] * en[j], _bd_rows(gk[j] * ep[j], G)) for j in n]
    rq = [gr_ref[rows[j], _RQ:_RK] for j in n]
    rk = [gr_ref[rows[j], _RK:_RV] for j in n]
    rv = [gr_ref[rows[j], _RV:_RG] for j in n]
    sc = [_dot_nt(rq[j], _bd_rows(rk[j], G)) * dec for j in n]
    g_intra = [_dot(jnp.where(incl, att_lo[j], att_hi[j]), _bd_rows(gv[j], G)) for j in n]
    r_intra = [_dot(sc[j], _bd_rows(rv[j], G)) for j in n]
    g_kv = [jnp.where(st_mask, _dot_tn(gv[j], gk[j] * jnp.exp(be[j] - bc[j])), 0.0) for j in n]
    r_kv = [jnp.where(st_mask, _dot_tn(rv[j], rk[j] * kdec), 0.0) for j in n]

    sg = sg_ref[...]
    sr = sr_ref[...]
    g_o = []
    r_o = []
    for j in n:
        g_o.append(g_intra[j] + _dot_nt(qp[j], sg))
        sg = sg * jnp.exp(be[j]) + g_kv[j]
        r_o.append(r_intra[j] + _dot_nt(rq[j] * qdec, sr))
        sr = sr * cdec + r_kv[j]
    sg_ref[...] = sg
    sr_ref[...] = sr

    sums = _group_sums([g_o[j] * g_o[j] for j in n] + r_o, ones_bd)
    r_c = [r_o[j] - sums[cb + j] * (1.0 / DV) for j in n]
    r_var = _group_sums([r_c[j] * r_c[j] for j in n], ones_bd)
    for j in n:
        o_ref[rows[j], 0:WV] = (g_o[j] * lax.rsqrt(sums[j] * (1.0 / DV) + EPS) * lng
                                * gr_ref[rows[j], _GG:_RQ])
        o_ref[rows[j], WV:2 * WV] = (r_c[j] * lax.rsqrt(r_var[j] * (1.0 / DV) + EPS)
                                     * gr_ref[rows[j], _RG:_GR_W])


def _glaret(gr, lng, dec, kdec, qdec, cdec, batch, seq, cb):
    T = gr.shape[0]
    nc = seq // (CHUNK * cb)
    wv = GLA_HEADS * HEAD_DV
    consts = [lng, dec, kdec, qdec, cdec]
    return pl.pallas_call(
        functools.partial(_glaret_kernel, cb),
        grid=(batch, nc),
        in_specs=[pl.BlockSpec((cb * CHUNK, _GR_W), lambda b, c: (b * nc + c, 0))]
                 + [_const_spec(a.shape) for a in consts],
        out_specs=pl.BlockSpec((cb * CHUNK, 2 * wv), lambda b, c: (b * nc + c, 0)),
        out_shape=jax.ShapeDtypeStruct((T, 2 * wv), F32),
        scratch_shapes=[pltpu.VMEM((wv, GLA_HEADS * GLA_DK), F32),
                        pltpu.VMEM((wv, RET_HEADS * RET_DK), F32)],
        compiler_params=pltpu.CompilerParams(
            dimension_semantics=("parallel", "arbitrary"), vmem_limit_bytes=VMEM_LIMIT_BYTES),
        name="gla_retention_chunk",
    )(gr, *consts)


def _outffn_kernel(final_norm, x_ref, ya_ref, ybc_ref, mod_ref, n2g_ref, wo_ref, wg_ref, wu_ref,
                   wd_ref, nfg_ref, o_ref):
    mod = mod_ref[0]
    y = jnp.concatenate([ya_ref[...], ybc_ref[...]], axis=-1)
    x = x_ref[...] + mod[2:3] * _dot(y, wo_ref[...])
    ms = jnp.mean(x * x, axis=-1, keepdims=True)
    h = (x * lax.rsqrt(ms + EPS) * (n2g_ref[...] * (1.0 + mod[4:5])) + mod[3:4]).astype(BF16)
    gate = jnp.dot(h, wg_ref[...], preferred_element_type=F32)
    up = jnp.dot(h, wu_ref[...], preferred_element_type=F32)
    x = x + mod[5:6] * _dot(gate * _sigmoid(gate) * up, wd_ref[...])
    if final_norm:
        ms = jnp.mean(x * x, axis=-1, keepdims=True)
        x = x * lax.rsqrt(ms + EPS) * nfg_ref[...]
    o_ref[...] = x


def _outffn(x2d, ya, ybc, mod, layer, seq, tm, final_norm, n2g, wo, wg, wu, wd, nfg):
    T, D = x2d.shape
    tiles_per_seq = seq // tm
    per_layer = [n2g, wo, wg, wu, wd]
    return pl.pallas_call(
        functools.partial(_outffn_kernel, final_norm),
        grid=(T // tm,),
        in_specs=[pl.BlockSpec((tm, D), lambda i: (i, 0)),
                  pl.BlockSpec((tm, ya.shape[1]), lambda i: (i, 0)),
                  pl.BlockSpec((tm, ybc.shape[1]), lambda i: (i, 0)),
                  pl.BlockSpec((None, 1, 6, D), lambda i: (layer, i // tiles_per_seq, 0, 0))]
                 + [_layer_spec(a.shape, layer) for a in per_layer] + [_const_spec(nfg.shape)],
        out_specs=pl.BlockSpec((tm, D), lambda i: (i, 0)),
        out_shape=jax.ShapeDtypeStruct((T, D), F32),
        compiler_params=pltpu.CompilerParams(
            dimension_semantics=("parallel",), vmem_limit_bytes=VMEM_LIMIT_BYTES),
        name="outproj_swiglu",
    )(x2d, ya, ybc, mod, *per_layer, nfg)


def _rope_tables(seq):
    half = RET_DK // 2
    inv_freq = ROPE_BASE ** (-jnp.arange(half, dtype=F32) / half)
    ang = jnp.arange(seq, dtype=F32)[:, None] * inv_freq[None, :]
    cos, sin = jnp.cos(ang), jnp.sin(ang)
    cos_t = jnp.tile(jnp.concatenate([cos, cos], axis=-1), (1, RET_HEADS))
    sin_t = jnp.tile(jnp.concatenate([-sin, sin], axis=-1), (1, RET_HEADS))
    return cos_t, sin_t


def _retention_tables():
    H, C = RET_HEADS, CHUNK
    log_gamma = jnp.log1p(-(2.0 ** (-5.0 - jnp.arange(H, dtype=F32))))
    pos = jnp.arange(C, dtype=F32)
    intra = jnp.exp(log_gamma[:, None, None] * jnp.abs(pos[:, None] - pos[None, :]))
    dec = jnp.transpose(intra, (1, 0, 2)).reshape(C, H * C)
    k_dec = jnp.exp(log_gamma[None, :] * (C - 1.0 - pos)[:, None])
    q_dec = jnp.exp(log_gamma[None, :] * (pos + 1.0)[:, None])
    chunk_dec = jnp.exp(log_gamma * C)
    kdec = jnp.repeat(k_dec, RET_DK, axis=1)
    qdec = jnp.repeat(q_dec, RET_DK, axis=1)
    cdec = jnp.repeat(chunk_dec, RET_DK)[None, :]
    return dec, kdec, qdec, cdec


def kernel(x, c, ada_w, ada_b, norm1_g, norm2_g, w_in, w_out, rk_mu_rkv, rk_mu_x, rk_w0, rk_w1, rk_w2, rk_a0, rk_a1, rk_a2, rk_g1, rk_g2, rk_k_k, rk_k_a, rk_r_k, rk_ln_g, rk_ln_b, rk_mu_v, rk_v0, rk_v1, rk_v2, gla_a1, gla_a2, gla_ab, gla_ln_g, ffn_w_gate, ffn_w_up, ffn_w_down, norm_f_g):
    B, S, D = x.shape
    L = ada_w.shape[0]
    T = B * S
    assert S % PREP_TOKENS_PER_STEP == 0 and S % FFN_TOKENS_PER_STEP == 0
    assert S % (CHUNK * RWKV_CHUNKS_PER_STEP) == 0 and S % (CHUNK * GLARET_CHUNKS_PER_STEP) == 0

    mod = _adaln(c, ada_w, ada_b).reshape(L, B, 6, D)
    cos_t, sin_t = _rope_tables(S)
    dec, kdec, qdec, cdec = _retention_tables()
    obd = _block_mask(GROUP_W, GROUP_W, HEAD_DV, HEAD_DV).astype(BF16)
    RW = RWKV_WIDTH

    def zeros(*shape):
        return jnp.zeros(shape, F32)

    mu_v = jnp.concatenate([zeros(1, D), rk_mu_v], axis=0)
    v0 = jnp.concatenate([zeros(1, RW), rk_v0], axis=0)
    v1 = jnp.concatenate([zeros(1, D, rk_v1.shape[2]), rk_v1], axis=0)
    v2 = jnp.concatenate([zeros(1, rk_v2.shape[1], RW), rk_v2], axis=0)

    def on_h(mu, w):
        return (1.0 - mu)[:, :, None] * w

    def on_prev(mu, w):
        return mu[:, :, None] * w

    mu = rk_mu_x
    misc = jnp.concatenate([gla_a1, zeros(L, D, 16), on_h(mu_v, v1), on_prev(mu_v, v1), zeros(L, D, 32)],
                           axis=2)
    lora_h = jnp.concatenate([on_h(mu[:, 0], rk_w1), on_h(mu[:, 1], rk_a1), on_h(mu[:, 2], rk_g1)], axis=2)
    lora_s = jnp.concatenate([on_prev(mu[:, 0], rk_w1), on_prev(mu[:, 1], rk_a1), on_prev(mu[:, 2], rk_g1)],
                             axis=2)
    win = jnp.concatenate([misc, lora_h, lora_s, w_in], axis=2).astype(BF16)
    assert win.shape[2] == _WIN_W and misc.shape[2] == _W_LORA_H - _W_MISC
    w2 = jnp.concatenate([
        jnp.concatenate([rk_w2, zeros(L, 64, 2 * RW)], axis=2),
        jnp.concatenate([zeros(L, 64, RW), rk_a2, zeros(L, 64, RW)], axis=2),
        jnp.concatenate([zeros(L, 128, 2 * RW), rk_g2], axis=2)], axis=1).astype(BF16)
    ga2 = jnp.concatenate([gla_a2, zeros(L, 128 - gla_a2.shape[1], gla_a2.shape[2])], axis=1).astype(BF16)
    v2p = jnp.concatenate([zeros(L, 32, RW), v2, v2, zeros(L, 32, RW)], axis=1).astype(BF16)
    vec = jnp.stack([rk_mu_rkv[:, 0], rk_mu_rkv[:, 1], rk_mu_rkv[:, 2], rk_w0, rk_a0, v0, rk_k_k, rk_k_a],
                    axis=1)
    par = jnp.stack([rk_r_k.reshape(L, RW), rk_ln_g, rk_ln_b], axis=1)
    lng = jnp.tile(gla_ln_g, (1, GLA_HEADS))[:, None, :]
    wo, wg, wu, wd = (w.astype(BF16) for w in (w_out, ffn_w_gate, ffn_w_up, ffn_w_down))

    x2d = x.reshape(T, D)
    rwp_first = None
    for l in range(L):
        has_vres = l > 0
        extra = dict(v2=v2p, rwp_first=rwp_first) if has_vres else {}
        rwp, gr = _prep(x2d, mod, l, S, PREP_TOKENS_PER_STEP, has_vres, norm1_g[:, None, :], win, w2, ga2,
                        gla_ab[:, None, :], vec, obd, cos_t, sin_t, **extra)
        if l == 0:
            rwp_first = rwp
        ya = _rwkv(rwp, par[l], B, S, RWKV_CHUNKS_PER_STEP)
        ybc = _glaret(gr, lng[l], dec, kdec, qdec, cdec, B, S, GLARET_CHUNKS_PER_STEP)
        x2d = _outffn(x2d, ya, ybc, mod, l, S, FFN_TOKENS_PER_STEP, l == L - 1, norm2_g[:, None, :],
                      wo, wg, wu, wd, norm_f_g[None])
    return x2d.reshape(B, S, D)
```

```python
import functools

import numpy as np
import jax
import jax.numpy as jnp
from jax import lax
from jax.experimental import pallas as pl
from jax.experimental.pallas import tpu as pltpu

F32 = jnp.float32
BF16 = jnp.bfloat16

CHUNK = 64
EPS = 1e-6
HEAD_DV = 64
RWKV_HEADS = 8
RWKV_WIDTH = RWKV_HEADS * HEAD_DV
RWKV_GN_EPS = 64e-5
GLA_HEADS = 4
GLA_DK = 32
GLA_GATE_TAU = 16.0
RET_HEADS = 4
RET_DK = 32
ROPE_BASE = 10000.0
HEADS_PER_GROUP = 4
GROUP_W = HEADS_PER_GROUP * HEAD_DV
VMEM_LIMIT_BYTES = 56 * 1024 * 1024
RWKV_CHUNKS_PER_STEP = 8
GLARET_CHUNKS_PER_STEP = 8
PREP_TOKENS_PER_STEP = 256
FFN_TOKENS_PER_STEP = 512

_GQ, _GK, _GLA, _GV, _GG, _RQ, _RK, _RV, _RG, _GR_W = 0, 128, 256, 384, 640, 896, 1024, 1152, 1408, 1664
_W_MISC, _W_LORA_H, _W_LORA_S, _W_IN, _WIN_W = 0, 128, 384, 640, 3712


def _dot(a, b):
    return jnp.dot(a.astype(BF16), b.astype(BF16), preferred_element_type=F32)


def _dot_nt(a, b):
    return lax.dot_general(a.astype(BF16), b.astype(BF16), (((1,), (1,)), ((), ())),
                           preferred_element_type=F32)


def _dot_tn(a, b):
    return lax.dot_general(a.astype(BF16), b.astype(BF16), (((0,), (0,)), ((), ())),
                           preferred_element_type=F32)


def _group_sums(xs, ones_bd):
    rows = [x.shape[0] for x in xs]
    hi = [x.astype(BF16) for x in xs]
    lo = [(x - h.astype(F32)).astype(BF16) for x, h in zip(xs, hi)]
    s = jnp.dot(jnp.concatenate(hi + lo, axis=0), ones_bd.astype(BF16), preferred_element_type=F32)
    total = sum(rows)
    out, off = [], 0
    for n in rows:
        out.append(s[off:off + n] + s[total + off:total + off + n])
        off += n
    return out


def _cumsum_rows(x):
    n = x.shape[0]
    row = _iota(x.shape, 0)
    s = 1
    while s < n:
        x = x + jnp.where(row >= s, pltpu.roll(x, s, 0), 0.0)
        s *= 2
    return x


def _sigmoid(x):
    return 1.0 / (1.0 + jnp.exp(-x))


def _softplus(x):
    return jnp.maximum(x, 0.0) + jnp.log(1.0 + jnp.exp(-jnp.abs(x)))


def _iota(shape, axis):
    return lax.broadcasted_iota(jnp.int32, shape, axis)


def _bd_rows(x, groups):
    c, w = x.shape
    n = w // groups
    t = jnp.concatenate([x] * groups, axis=0)
    keep = (_iota(t.shape, 0) // c) == (_iota(t.shape, 1) // n)
    return jnp.where(keep, t, 0.0)


def _bd_dot(x, y, groups, nt=False):
    e = _bd_rows(y, groups)
    return _dot_nt(x, e) if nt else _dot(x, e)


def _block_mask(rows, cols, rblk, cblk):
    return (_iota((rows, cols), 0) // rblk) == (_iota((rows, cols), 1) // cblk)


def _const_spec(shape):
    nd = len(shape)
    return pl.BlockSpec(shape, lambda *_: (0,) * nd, pipeline_mode=pl.Buffered(1))


def _layer_spec(shape, layer):
    nd = len(shape)
    return pl.BlockSpec((None,) + tuple(shape[1:]), lambda *_: (layer,) + (0,) * (nd - 1),
                        pipeline_mode=pl.Buffered(1))


def _adaln_kernel(c_ref, w_ref, b_ref, o_ref):
    c = c_ref[...]
    cond = c * _sigmoid(c)
    o_ref[0] = _dot(cond, w_ref[0]) + b_ref[0]


def _adaln(c, ada_w, ada_b):
    L, D, D6 = ada_w.shape
    B = c.shape[0]
    tn = 1536
    return pl.pallas_call(
        _adaln_kernel,
        grid=(L, D6 // tn),
        in_specs=[pl.BlockSpec((B, D), lambda l, j: (0, 0)),
                  pl.BlockSpec((1, D, tn), lambda l, j: (l, 0, j)),
                  pl.BlockSpec((1, 1, tn), lambda l, j: (l, 0, j))],
        out_specs=pl.BlockSpec((1, B, tn), lambda l, j: (l, 0, j)),
        out_shape=jax.ShapeDtypeStruct((L, B, D6), F32),
        compiler_params=pltpu.CompilerParams(
            dimension_semantics=("arbitrary", "arbitrary"), vmem_limit_bytes=VMEM_LIMIT_BYTES),
        name="adaln_mod",
    )(c, ada_w, ada_b.reshape(L, 1, D6))


def _prep_kernel(tiles_per_seq, has_vres, *refs):
    if has_vres:
        (x_ref, xh_ref, mod_ref, n1g_ref, win_ref, w2_ref, ga2_ref, gab_ref, vec_ref, obd_ref,
         cos_ref, sin_ref, v2_ref, vf_ref, rwp_ref, gr_ref) = refs
    else:
        (x_ref, xh_ref, mod_ref, n1g_ref, win_ref, w2_ref, ga2_ref, gab_ref, vec_ref, obd_ref,
         cos_ref, sin_ref, rwp_ref, gr_ref) = refs
    tm = x_ref.shape[0]
    rw = RWKV_WIDTH
    first = (pl.program_id(0) % tiles_per_seq) == 0

    xe = jnp.concatenate([xh_ref[...], x_ref[...]], axis=0)
    mod = mod_ref[0]
    ms = jnp.mean(xe * xe, axis=-1, keepdims=True)
    he = xe * lax.rsqrt(ms + EPS) * (n1g_ref[...] * (1.0 + mod[1:2])) + mod[0:1]
    halo = jnp.logical_and(_iota((tm + 8, 1), 0) < 8, first)
    he = jnp.where(halo, 0.0, he)

    heb = he.astype(BF16)

    def project(lo, hi):
        return jnp.dot(heb, win_ref[:, lo:hi], preferred_element_type=F32)

    def prev_rows(t):
        return pltpu.roll(t, 1, 0)[8:]

    p_lora = project(_W_MISC, _W_IN)
    p_rkv = project(_W_IN, _W_IN + 3 * rw)
    vec = vec_ref[...]

    pre = p_lora[8:, _W_LORA_H:_W_LORA_S] + prev_rows(p_lora[:, _W_LORA_S:_W_IN])
    lane = _iota(pre.shape, 1)
    act = jnp.where(lane < 64, jnp.tanh(pre), jnp.where(lane < 128, pre, _sigmoid(pre)))
    second = _dot(act, w2_ref[...])
    p_gla = project(_W_IN + 3 * rw, _W_IN + 3 * rw + 768)
    lw = -jnp.exp(-_softplus(-(vec[3:4] + second[:, 0:rw])) - 0.5)
    a = _sigmoid(vec[4:5] + second[:, rw:2 * rw])
    g = second[:, 2 * rw:3 * rw]

    p = p_rkv[8:]
    ps = prev_rows(p_rkv)
    r = p[:, 0:rw]
    r = r + (ps[:, 0:rw] - r) * vec[0:1]
    k = p[:, rw:2 * rw]
    k = k + (ps[:, rw:2 * rw] - k) * vec[1:2]
    v = p[:, 2 * rw:3 * rw]
    v = v + (ps[:, 2 * rw:3 * rw] - v) * vec[2:3]

    kk = k * vec[6:7]
    kk2 = kk * kk
    ss = jnp.concatenate(_group_sums([kk2[:, :GROUP_W], kk2[:, GROUP_W:]], obd_ref[...]), axis=1)
    kk = kk / jnp.maximum(jnp.sqrt(ss), 1e-12)
    k = k * (1.0 + (a - 1.0) * vec[7:8])
    misc = p_lora[8:, _W_MISC:_W_LORA_H]
    if has_vres:
        mv = jnp.where(_iota(misc.shape, 1) < 64, misc, prev_rows(p_lora[:, _W_MISC:_W_LORA_H]))
        v = v + (vf_ref[0] - v) * _sigmoid(vec[5:6] + _dot(mv, v2_ref[...]))
    p_ret = project(_W_IN + 3 * rw + 768, _WIN_W)

    rwp_ref[0] = r
    rwp_ref[1] = lw
    rwp_ref[2] = k
    rwp_ref[3] = v
    rwp_ref[4] = kk
    rwp_ref[5] = kk * a
    rwp_ref[6] = g

    p = p_gla[8:]
    gr_ref[:, _GQ:_GK] = p[:, 0:128] * (GLA_DK ** -0.5)
    gr_ref[:, _GK:_GLA] = p[:, 128:256]
    la_pre = _dot(misc, ga2_ref[...]) + gab_ref[...]
    gr_ref[:, _GLA:_GV] = -_softplus(-la_pre) * (1.0 / GLA_GATE_TAU)
    gr_ref[:, _GV:_GG] = p[:, 256:512]
    gate = p[:, 512:768]
    gr_ref[:, _GG:_RQ] = gate * _sigmoid(gate)

    p = p_ret[8:]
    o = 0
    cos = cos_ref[...]
    sin = sin_ref[...]
    lo_half = (_iota((tm, 128), 1) % RET_DK) < (RET_DK // 2)

    def rope(t):
        swapped = jnp.where(lo_half, pltpu.roll(t, 128 - RET_DK // 2, 1), pltpu.roll(t, RET_DK // 2, 1))
        return t * cos + swapped * sin

    gr_ref[:, _RQ:_RK] = rope(p[:, o:o + 128]) * (RET_DK ** -0.5)
    gr_ref[:, _RK:_RV] = rope(p[:, o + 128:o + 256])
    gr_ref[:, _RV:_RG] = p[:, o + 256:o + 512]
    gate = p[:, o + 512:o + 768]
    gr_ref[:, _RG:_GR_W] = gate * _sigmoid(gate)


def _prep(x2d, mod, layer, seq, tm, has_vres, n1g, win, w2, ga2, gab, vec, obd, cos_t, sin_t,
          v2=None, rwp_first=None):
    T, D = x2d.shape
    tiles_per_seq = seq // tm
    n_tiles = T // tm
    per_layer = [n1g, win, w2, ga2, gab, vec]
    in_specs = [pl.BlockSpec((tm, D), lambda i: (i, 0)),
                pl.BlockSpec((8, D), lambda i: (jnp.maximum(i * (tm // 8) - 1, 0), 0)),
                pl.BlockSpec((None, 1, 6, D), lambda i: (layer, i // tiles_per_seq, 0, 0))]
    in_specs += [_layer_spec(a.shape, layer) for a in per_layer]
    in_specs += [_const_spec(obd.shape),
                 pl.BlockSpec((tm, 128), lambda i: (i % tiles_per_seq, 0)),
                 pl.BlockSpec((tm, 128), lambda i: (i % tiles_per_seq, 0))]
    args = [x2d, x2d, mod] + per_layer + [obd, cos_t, sin_t]
    if has_vres:
        in_specs += [_layer_spec(v2.shape, layer),
                     pl.BlockSpec((1, tm, RWKV_WIDTH), lambda i: (3, i, 0))]
        args += [v2, rwp_first]
    return pl.pallas_call(
        functools.partial(_prep_kernel, tiles_per_seq, has_vres),
        grid=(n_tiles,),
        in_specs=in_specs,
        out_specs=[pl.BlockSpec((7, tm, RWKV_WIDTH), lambda i: (0, i, 0)),
                   pl.BlockSpec((tm, _GR_W), lambda i: (i, 0))],
        out_shape=[jax.ShapeDtypeStruct((7, T, RWKV_WIDTH), F32),
                   jax.ShapeDtypeStruct((T, _GR_W), F32)],
        compiler_params=pltpu.CompilerParams(
            dimension_semantics=("parallel",), vmem_limit_bytes=VMEM_LIMIT_BYTES),
        name="proj_prep",
    )(*args)


def _chunk_masks(groups):
    C = CHUNK
    row = _iota((C, groups * C), 0)
    col = _iota((C, groups * C), 1) % C
    return row > col, row >= col, (row == col).astype(F32)


def _rwkv_chunk_factors(units, masks, bdm, eye):
    C, G = CHUNK, HEADS_PER_GROUP
    strict, incl, ident = masks
    n = range(len(units))
    r, lw, k, v, kk, kb = ([u[i] for u in units] for i in range(6))
    cl = [_cumsum_rows(lw[i]) for i in n]
    cle = [cl[i][C - 1:C] for i in n]
    e_neg = [jnp.exp(-cl[i]) for i in n]
    e_end = [jnp.exp(cle[i] - cl[i]) for i in n]
    at = [-kk[i] * jnp.exp(cl[i] - lw[i]) for i in n]
    rt = [r[i] * jnp.exp(cl[i]) for i in n]
    ar = [jnp.concatenate([at[i], rt[i]], axis=0) for i in n]
    sb = [_bd_dot(ar[i], kb[i] * e_neg[i], G, nt=True) for i in n]
    sk = [_bd_dot(ar[i], k[i] * e_neg[i], G, nt=True) for i in n]
    a_rb = [jnp.where(incl, sb[i][C:], 0.0) for i in n]
    a_ak = [jnp.where(strict, sk[i][:C], 0.0) for i in n]
    a_rk = [jnp.where(incl, sk[i][C:], 0.0) for i in n]

    a_ab = [jnp.where(strict, sb[i][:C], 0.0) for i in n]
    tm = [ident + a_ab[i] for i in n]
    xp = [_bd_dot(a_ab[i], a_ab[i], G) for i in n]
    for _ in range(4):
        rr = [_bd_dot(jnp.concatenate([tm[i], xp[i]], axis=0), xp[i], G) for i in n]
        tm = [tm[i] + rr[i][:C] for i in n]
        xp = [rr[i][C:] for i in n]
    tm = [tm[i] + _bd_dot(tm[i], xp[i], G) for i in n]

    vv = [_bd_dot(jnp.concatenate([a_ak[i], a_rk[i]], axis=0), v[i], G) for i in n]
    wt = [_bd_dot(tm[i], at[i], G) for i in n]
    ut = [_bd_dot(tm[i], vv[i][:C], G) for i in n]
    qh = [rt[i] + _bd_dot(a_rb[i], wt[i], G) for i in n]
    yh = [_bd_dot(a_rb[i], ut[i], G) + vv[i][C:] for i in n]
    zero = jnp.zeros((C, GROUP_W), F32)
    pz = [_dot_tn(jnp.concatenate([kb[i] * e_end[i], k[i] * e_end[i]], axis=0),
                  jnp.concatenate([jnp.concatenate([wt[i], ut[i]], axis=1),
                                   jnp.concatenate([zero, v[i]], axis=1)], axis=0)) for i in n]
    pm = [jnp.where(bdm, pz[i][:, :GROUP_W], 0.0) + jnp.where(eye, jnp.exp(cle[i]), 0.0) for i in n]
    zm = [jnp.where(bdm, pz[i][:, GROUP_W:], 0.0) for i in n]
    return qh, yh, pm, zm


def _rwkv_kernel(cb, rwp_ref, par_ref, o_ref, st_ref):
    C, W, N = CHUNK, GROUP_W, HEAD_DV
    ngroups = o_ref.shape[1] // W

    @pl.when(pl.program_id(1) == 0)
    def _():
        st_ref[...] = jnp.zeros_like(st_ref)

    masks = _chunk_masks(HEADS_PER_GROUP)
    bdm = _block_mask(W, W, N, N)
    eye = _iota((W, W), 0) == _iota((W, W), 1)
    ones_bd = bdm.astype(F32)

    where = [(slice(j * C, (j + 1) * C), slice(gi * W, (gi + 1) * W))
             for j in range(cb) for gi in range(ngroups)]
    qh, yh, pm, zm = _rwkv_chunk_factors(
        [[rwp_ref[i, rows, lanes] for i in range(6)] for rows, lanes in where], masks, bdm, eye)

    m = [st_ref[gi] for gi in range(ngroups)]
    y = []
    for j in range(cb):
        for gi in range(ngroups):
            u = j * ngroups + gi
            ym = _dot(jnp.concatenate([qh[u], pm[u]], axis=0), m[gi])
            y.append(ym[:C] + yh[u])
            m[gi] = ym[C:] + zm[u]
    for gi in range(ngroups):
        st_ref[gi] = m[gi]

    n = range(len(where))
    par = [par_ref[:, lanes] for _, lanes in where]
    rkr = [rwp_ref[0, rows, lanes] * rwp_ref[2, rows, lanes] * par[u][0:1]
           for u, (rows, lanes) in enumerate(where)]
    sums = _group_sums(y + rkr, ones_bd)
    yc = [y[u] - sums[u] * (1.0 / N) for u in n]
    var = _group_sums([yc[u] * yc[u] for u in n], ones_bd)
    for u, (rows, lanes) in enumerate(where):
        yn = yc[u] * lax.rsqrt(var[u] * (1.0 / N) + RWKV_GN_EPS) * par[u][1:2] + par[u][2:3]
        o_ref[rows, lanes] = (yn + sums[len(where) + u] * rwp_ref[3, rows, lanes]) * rwp_ref[6, rows, lanes]


def _rwkv(rwp, par, batch, seq, cb):
    _, T, RW = rwp.shape
    steps = seq // (CHUNK * cb)
    return pl.pallas_call(
        functools.partial(_rwkv_kernel, cb),
        grid=(batch, steps),
        in_specs=[pl.BlockSpec((7, cb * CHUNK, RW), lambda b, c: (0, b * steps + c, 0)),
                  _const_spec(par.shape)],
        out_specs=pl.BlockSpec((cb * CHUNK, RW), lambda b, c: (b * steps + c, 0)),
        out_shape=jax.ShapeDtypeStruct((T, RW), F32),
        scratch_shapes=[pltpu.VMEM((RW // GROUP_W, GROUP_W, GROUP_W), F32)],
        compiler_params=pltpu.CompilerParams(
            dimension_semantics=("parallel", "arbitrary"), vmem_limit_bytes=VMEM_LIMIT_BYTES),
        name="rwkv7_chunk",
    )(rwp, par)


def _glaret_kernel(cb, gr_ref, lng_ref, dec_ref, kdec_ref, qdec_ref, cdec_ref, o_ref, sg_ref, sr_ref):
    C, G, DV = CHUNK, GLA_HEADS, HEAD_DV
    WV = G * DV

    @pl.when(pl.program_id(1) == 0)
    def _():
        sg_ref[...] = jnp.zeros_like(sg_ref)
        sr_ref[...] = jnp.zeros_like(sr_ref)

    _, incl, _ = _chunk_masks(G)
    ones_bd = _block_mask(WV, WV, DV, DV).astype(F32)
    st_mask = _block_mask(WV, G * GLA_DK, DV, GLA_DK)
    lng = lng_ref[...]
    dec = dec_ref[...]
    kdec = kdec_ref[...]
    qdec = qdec_ref[...]
    cdec = cdec_ref[...]

    n = range(cb)
    rows = [slice(j * C, (j + 1) * C) for j in n]
    gq = [gr_ref[rows[j], _GQ:_GK] for j in n]
    gk = [gr_ref[rows[j], _GK:_GLA] for j in n]
    gv = [gr_ref[rows[j], _GV:_GG] for j in n]
    bc = [_cumsum_rows(gr_ref[rows[j], _GLA:_GV]) for j in n]
    be = [bc[j][C - 1:C] for j in n]
    mid = [bc[j][C // 2 - 1:C // 2] for j in n]
    ep = [jnp.exp(bc[j] - mid[j]) for j in n]
    en = [jnp.exp(mid[j] - bc[j]) for j in n]
    qp = [gq[j] * jnp.exp(bc[j]) for j in n]
    att_lo = [_dot_nt(gq[j] * ep[j], _bd_rows(gk[j] * en[j], G)) for j in n]
    att_hi = [_dot_nt(gq[j] * en[j], _bd_rows(gk[j] * ep[j], G)) for j in n]
    rq = [gr_ref[rows[j], _RQ:_RK] for j in n]
    rk = [gr_ref[rows[j], _RK:_RV] for j in n]
    rv = [gr_ref[rows[j], _RV:_RG] for j in n]
    sc = [_dot_nt(rq[j], _bd_rows(rk[j], G)) * dec for j in n]
    g_intra = [_bd_dot(jnp.where(incl, att_lo[j], att_hi[j]), gv[j], G) for j in n]
    r_intra = [_bd_dot(sc[j], rv[j], G) for j in n]
    g_kv = [jnp.where(st_mask, _dot_tn(gv[j], gk[j] * jnp.exp(be[j] - bc[j])), 0.0) for j in n]
    r_kv = [jnp.where(st_mask, _dot_tn(rv[j], rk[j] * kdec), 0.0) for j in n]

    sg = sg_ref[...]
    sr = sr_ref[...]
    g_o = []
    r_o = []
    for j in n:
        g_o.append(g_intra[j] + _dot_nt(qp[j], sg))
        sg = sg * jnp.exp(be[j]) + g_kv[j]
        r_o.append(r_intra[j] + _dot_nt(rq[j] * qdec, sr))
        sr = sr * cdec + r_kv[j]
    sg_ref[...] = sg
    sr_ref[...] = sr

    sums = _group_sums([g_o[j] * g_o[j] for j in n] + r_o, ones_bd)
    r_c = [r_o[j] - sums[cb + j] * (1.0 / DV) for j in n]
    r_var = _group_sums([r_c[j] * r_c[j] for j in n], ones_bd)
    for j in n:
        o_ref[rows[j], 0:WV] = (g_o[j] * lax.rsqrt(sums[j] * (1.0 / DV) + EPS) * lng
                                * gr_ref[rows[j], _GG:_RQ])
        o_ref[rows[j], WV:2 * WV] = (r_c[j] * lax.rsqrt(r_var[j] * (1.0 / DV) + EPS)
                                     * gr_ref[rows[j], _RG:_GR_W])


def _glaret(gr, lng, dec, kdec, qdec, cdec, batch, seq, cb):
    T = gr.shape[0]
    nc = seq // (CHUNK * cb)
    wv = GLA_HEADS * HEAD_DV
    consts = [lng, dec, kdec, qdec, cdec]
    return pl.pallas_call(
        functools.partial(_glaret_kernel, cb),
        grid=(batch, nc),
        in_specs=[pl.BlockSpec((cb * CHUNK, _GR_W), lambda b, c: (b * nc + c, 0))]
                 + [_const_spec(a.shape) for a in consts],
        out_specs=pl.BlockSpec((cb * CHUNK, 2 * wv), lambda b, c: (b * nc + c, 0)),
        out_shape=jax.ShapeDtypeStruct((T, 2 * wv), F32),
        scratch_shapes=[pltpu.VMEM((wv, GLA_HEADS * GLA_DK), F32),
                        pltpu.VMEM((wv, RET_HEADS * RET_DK), F32)],
        compiler_params=pltpu.CompilerParams(
            dimension_semantics=("parallel", "arbitrary"), vmem_limit_bytes=VMEM_LIMIT_BYTES),
        name="gla_retention_chunk",
    )(gr, *consts)


def _outffn_kernel(final_norm, x_ref, ya_ref, ybc_ref, mod_ref, n2g_ref, wo_ref, wg_ref, wu_ref,
                   wd_ref, nfg_ref, o_ref):
    mod = mod_ref[0]
    y = jnp.concatenate([ya_ref[...], ybc_ref[...]], axis=-1)
    x = x_ref[...] + mod[2:3] * _dot(y, wo_ref[...])
    ms = jnp.mean(x * x, axis=-1, keepdims=True)
    h = (x * lax.rsqrt(ms + EPS) * (n2g_ref[...] * (1.0 + mod[4:5])) + mod[3:4]).astype(BF16)
    gate = jnp.dot(h, wg_ref[...], preferred_element_type=F32)
    up = jnp.dot(h, wu_ref[...], preferred_element_type=F32)
    x = x + mod[5:6] * _dot(gate * _sigmoid(gate) * up, wd_ref[...])
    if final_norm:
        ms = jnp.mean(x * x, axis=-1, keepdims=True)
        x = x * lax.rsqrt(ms + EPS) * nfg_ref[...]
    o_ref[...] = x


def _outffn(x2d, ya, ybc, mod, layer, seq, tm, final_norm, n2g, wo, wg, wu, wd, nfg):
    T, D = x2d.shape
    tiles_per_seq = seq // tm
    per_layer = [n2g, wo, wg, wu, wd]
    return pl.pallas_call(
        functools.partial(_outffn_kernel, final_norm),
        grid=(T // tm,),
        in_specs=[pl.BlockSpec((tm, D), lambda i: (i, 0)),
                  pl.BlockSpec((tm, ya.shape[1]), lambda i: (i, 0)),
                  pl.BlockSpec((tm, ybc.shape[1]), lambda i: (i, 0)),
                  pl.BlockSpec((None, 1, 6, D), lambda i: (layer, i // tiles_per_seq, 0, 0))]
                 + [_layer_spec(a.shape, layer) for a in per_layer] + [_const_spec(nfg.shape)],
        out_specs=pl.BlockSpec((tm, D), lambda i: (i, 0)),
        out_shape=jax.ShapeDtypeStruct((T, D), F32),
        compiler_params=pltpu.CompilerParams(
            dimension_semantics=("parallel",), vmem_limit_bytes=VMEM_LIMIT_BYTES),
        name="outproj_swiglu",
    )(x2d, ya, ybc, mod, *per_layer, nfg)


def _rope_tables(seq):
    half = RET_DK // 2
    inv_freq = ROPE_BASE ** (-jnp.arange(half, dtype=F32) / half)
    ang = jnp.arange(seq, dtype=F32)[:, None] * inv_freq[None, :]
    cos, sin = jnp.cos(ang), jnp.sin(ang)
    cos_t = jnp.tile(jnp.concatenate([cos, cos], axis=-1), (1, RET_HEADS))
    sin_t = jnp.tile(jnp.concatenate([-sin, sin], axis=-1), (1, RET_HEADS))
    return cos_t, sin_t


def _retention_tables():
    H, C = RET_HEADS, CHUNK
    log_gamma = jnp.log1p(-(2.0 ** (-5.0 - jnp.arange(H, dtype=F32))))
    pos = jnp.arange(C, dtype=F32)
    intra = jnp.exp(log_gamma[:, None, None] * jnp.abs(pos[:, None] - pos[None, :]))
    dec = jnp.transpose(intra, (1, 0, 2)).reshape(C, H * C)
    k_dec = jnp.exp(log_gamma[None, :] * (C - 1.0 - pos)[:, None])
    q_dec = jnp.exp(log_gamma[None, :] * (pos + 1.0)[:, None])
    chunk_dec = jnp.exp(log_gamma * C)
    kdec = jnp.repeat(k_dec, RET_DK, axis=1)
    qdec = jnp.repeat(q_dec, RET_DK, axis=1)
    cdec = jnp.repeat(chunk_dec, RET_DK)[None, :]
    return dec, kdec, qdec, cdec


def kernel(x, c, ada_w, ada_b, norm1_g, norm2_g, w_in, w_out, rk_mu_rkv, rk_mu_x, rk_w0, rk_w1, rk_w2, rk_a0, rk_a1, rk_a2, rk_g1, rk_g2, rk_k_k, rk_k_a, rk_r_k, rk_ln_g, rk_ln_b, rk_mu_v, rk_v0, rk_v1, rk_v2, gla_a1, gla_a2, gla_ab, gla_ln_g, ffn_w_gate, ffn_w_up, ffn_w_down, norm_f_g):
    B, S, D = x.shape
    L = ada_w.shape[0]
    T = B * S
    assert S % PREP_TOKENS_PER_STEP == 0 and S % FFN_TOKENS_PER_STEP == 0
    assert S % (CHUNK * RWKV_CHUNKS_PER_STEP) == 0 and S % (CHUNK * GLARET_CHUNKS_PER_STEP) == 0

    mod = _adaln(c, ada_w, ada_b).reshape(L, B, 6, D)
    cos_t, sin_t = _rope_tables(S)
    dec, kdec, qdec, cdec = _retention_tables()
    obd = _block_mask(GROUP_W, GROUP_W, HEAD_DV, HEAD_DV).astype(BF16)
    RW = RWKV_WIDTH

    def zeros(*shape):
        return jnp.zeros(shape, F32)

    mu_v = jnp.concatenate([zeros(1, D), rk_mu_v], axis=0)
    v0 = jnp.concatenate([zeros(1, RW), rk_v0], axis=0)
    v1 = jnp.concatenate([zeros(1, D, rk_v1.shape[2]), rk_v1], axis=0)
    v2 = jnp.concatenate([zeros(1, rk_v2.shape[1], RW), rk_v2], axis=0)

    def on_h(mu, w):
        return (1.0 - mu)[:, :, None] * w

    def on_prev(mu, w):
        return mu[:, :, None] * w

    mu = rk_mu_x
    misc = jnp.concatenate([gla_a1, zeros(L, D, 16), on_h(mu_v, v1), on_prev(mu_v, v1), zeros(L, D, 32)],
                           axis=2)
    lora_h = jnp.concatenate([on_h(mu[:, 0], rk_w1), on_h(mu[:, 1], rk_a1), on_h(mu[:, 2], rk_g1)], axis=2)
    lora_s = jnp.concatenate([on_prev(mu[:, 0], rk_w1), on_prev(mu[:, 1], rk_a1), on_prev(mu[:, 2], rk_g1)],
                             axis=2)
    win = jnp.concatenate([t.astype(BF16) for t in (misc, lora_h, lora_s, w_in)], axis=2)
    assert win.shape[2] == _WIN_W and misc.shape[2] == _W_LORA_H - _W_MISC
    w2 = jnp.concatenate([
        jnp.concatenate([rk_w2, zeros(L, 64, 2 * RW)], axis=2),
        jnp.concatenate([zeros(L, 64, RW), rk_a2, zeros(L, 64, RW)], axis=2),
        jnp.concatenate([zeros(L, 128, 2 * RW), rk_g2], axis=2)], axis=1).astype(BF16)
    ga2 = jnp.concatenate([gla_a2, zeros(L, 128 - gla_a2.shape[1], gla_a2.shape[2])], axis=1).astype(BF16)
    v2p = jnp.concatenate([zeros(L, 32, RW), v2, v2, zeros(L, 32, RW)], axis=1).astype(BF16)
    vec = jnp.stack([rk_mu_rkv[:, 0], rk_mu_rkv[:, 1], rk_mu_rkv[:, 2], rk_w0, rk_a0, v0, rk_k_k, rk_k_a],
                    axis=1)
    par = jnp.stack([rk_r_k.reshape(L, RW), rk_ln_g, rk_ln_b], axis=1)
    lng = jnp.tile(gla_ln_g, (1, GLA_HEADS))[:, None, :]
    wo, wg, wu, wd = (w.astype(BF16) for w in (w_out, ffn_w_gate, ffn_w_up, ffn_w_down))

    x2d = x.reshape(T, D)
    rwp_first = None
    for l in range(L):
        has_vres = l > 0
        extra = dict(v2=v2p, rwp_first=rwp_first) if has_vres else {}
        rwp, gr = _prep(x2d, mod, l, S, PREP_TOKENS_PER_STEP, has_vres, norm1_g[:, None, :], win, w2, ga2,
                        gla_ab[:, None, :], vec, obd, cos_t, sin_t, **extra)
        if l == 0:
            rwp_first = rwp
        ya = _rwkv(rwp, par[l], B, S, RWKV_CHUNKS_PER_STEP)
        ybc = _glaret(gr, lng[l], dec, kdec, qdec, cdec, B, S, GLARET_CHUNKS_PER_STEP)
        x2d = _outffn(x2d, ya, ybc, mod, l, S, FFN_TOKENS_PER_STEP, l == L - 1, norm2_g[:, None, :],
                      wo, wg, wu, wd, norm_f_g[None])
    return x2d.reshape(B, S, D)
```

```python
import functools

import numpy as np
import jax
import jax.numpy as jnp
from jax import lax
from jax.experimental import pallas as pl
from jax.experimental.pallas import tpu as pltpu

F32 = jnp.float32
BF16 = jnp.bfloat16

CHUNK = 64
EPS = 1e-6
HEAD_DV = 64
RWKV_HEADS = 8
RWKV_WIDTH = RWKV_HEADS * HEAD_DV
RWKV_GN_EPS = 64e-5
GLA_HEADS = 4
GLA_DK = 32
GLA_GATE_TAU = 16.0
RET_HEADS = 4
RET_DK = 32
ROPE_BASE = 10000.0
HEADS_PER_GROUP = 4
GROUP_W = HEADS_PER_GROUP * HEAD_DV
VMEM_LIMIT_BYTES = 56 * 1024 * 1024
RWKV_CHUNKS_PER_STEP = 16
RWKV_WAVES_PER_STEP = 4
GLARET_CHUNKS_PER_STEP = 8
PREP_TOKENS_PER_STEP = 256
FFN_TOKENS_PER_STEP = 512

_GQ, _GK, _GLA, _GV, _GG, _RQ, _RK, _RV, _RG, _GR_W = 0, 128, 256, 384, 640, 896, 1024, 1152, 1408, 1664
_W_MISC, _W_LORA_H, _W_LORA_S, _W_IN, _WIN_W = 0, 128, 384, 640, 3712


def _dot(a, b):
    return jnp.dot(a.astype(BF16), b.astype(BF16), preferred_element_type=F32)


def _dot_nt(a, b):
    return lax.dot_general(a.astype(BF16), b.astype(BF16), (((1,), (1,)), ((), ())),
                           preferred_element_type=F32)


def _dot_tn(a, b):
    return lax.dot_general(a.astype(BF16), b.astype(BF16), (((0,), (0,)), ((), ())),
                           preferred_element_type=F32)


def _group_sums(xs, ones_bd):
    rows = [x.shape[0] for x in xs]
    hi = [x.astype(BF16) for x in xs]
    lo = [(x - h.astype(F32)).astype(BF16) for x, h in zip(xs, hi)]
    s = jnp.dot(jnp.concatenate(hi + lo, axis=0), ones_bd.astype(BF16), preferred_element_type=F32)
    total = sum(rows)
    out, off = [], 0
    for n in rows:
        out.append(s[off:off + n] + s[total + off:total + off + n])
        off += n
    return out


def _cumsum_rows(x):
    n = x.shape[0]
    row = _iota(x.shape, 0)
    s = 1
    while s < n:
        x = x + jnp.where(row >= s, pltpu.roll(x, s, 0), 0.0)
        s *= 2
    return x


def _sigmoid(x):
    return 1.0 / (1.0 + jnp.exp(-x))


def _softplus(x):
    return jnp.maximum(x, 0.0) + jnp.log(1.0 + jnp.exp(-jnp.abs(x)))


def _iota(shape, axis):
    return lax.broadcasted_iota(jnp.int32, shape, axis)


def _bd_rows(x, groups):
    c, w = x.shape
    n = w // groups
    t = jnp.concatenate([x] * groups, axis=0)
    keep = (_iota(t.shape, 0) // c) == (_iota(t.shape, 1) // n)
    return jnp.where(keep, t, 0.0)


def _bd_dot(x, y, groups, nt=False):
    e = _bd_rows(y, groups)
    return _dot_nt(x, e) if nt else _dot(x, e)


def _block_mask(rows, cols, rblk, cblk):
    return (_iota((rows, cols), 0) // rblk) == (_iota((rows, cols), 1) // cblk)


def _const_spec(shape):
    nd = len(shape)
    return pl.BlockSpec(shape, lambda *_: (0,) * nd, pipeline_mode=pl.Buffered(1))


def _layer_spec(shape, layer):
    nd = len(shape)
    return pl.BlockSpec((None,) + tuple(shape[1:]), lambda *_: (layer,) + (0,) * (nd - 1),
                        pipeline_mode=pl.Buffered(1))


def _adaln_kernel(c_ref, w_ref, b_ref, o_ref):
    c = c_ref[...]
    cond = c * _sigmoid(c)
    o_ref[0] = _dot(cond, w_ref[0]) + b_ref[0]


def _adaln(c, ada_w, ada_b):
    L, D, D6 = ada_w.shape
    B = c.shape[0]
    tn = 1536
    return pl.pallas_call(
        _adaln_kernel,
        grid=(L, D6 // tn),
        in_specs=[pl.BlockSpec((B, D), lambda l, j: (0, 0)),
                  pl.BlockSpec((1, D, tn), lambda l, j: (l, 0, j)),
                  pl.BlockSpec((1, 1, tn), lambda l, j: (l, 0, j))],
        out_specs=pl.BlockSpec((1, B, tn), lambda l, j: (l, 0, j)),
        out_shape=jax.ShapeDtypeStruct((L, B, D6), F32),
        compiler_params=pltpu.CompilerParams(
            dimension_semantics=("arbitrary", "arbitrary"), vmem_limit_bytes=VMEM_LIMIT_BYTES),
        name="adaln_mod",
    )(c, ada_w, ada_b.reshape(L, 1, D6))


def _prep_kernel(tiles_per_seq, has_vres, *refs):
    if has_vres:
        (x_ref, xh_ref, mod_ref, n1g_ref, win_ref, w2_ref, ga2_ref, gab_ref, vec_ref, obd_ref,
         cos_ref, sin_ref, v2_ref, vf_ref, rwp_ref, gr_ref) = refs
    else:
        (x_ref, xh_ref, mod_ref, n1g_ref, win_ref, w2_ref, ga2_ref, gab_ref, vec_ref, obd_ref,
         cos_ref, sin_ref, rwp_ref, gr_ref) = refs
    tm = x_ref.shape[0]
    rw = RWKV_WIDTH
    first = (pl.program_id(0) % tiles_per_seq) == 0

    xe = jnp.concatenate([xh_ref[...], x_ref[...]], axis=0)
    mod = mod_ref[0]
    ms = jnp.mean(xe * xe, axis=-1, keepdims=True)
    he = xe * lax.rsqrt(ms + EPS) * (n1g_ref[...] * (1.0 + mod[1:2])) + mod[0:1]
    halo = jnp.logical_and(_iota((tm + 8, 1), 0) < 8, first)
    he = jnp.where(halo, 0.0, he)

    heb = he.astype(BF16)

    def project(lo, hi):
        return jnp.dot(heb, win_ref[:, lo:hi], preferred_element_type=F32)

    def prev_rows(t):
        return pltpu.roll(t, 1, 0)[8:]

    p_lora = project(_W_MISC, _W_IN)
    p_rkv = project(_W_IN, _W_IN + 3 * rw)
    vec = vec_ref[...]

    pre = p_lora[8:, _W_LORA_H:_W_LORA_S] + prev_rows(p_lora[:, _W_LORA_S:_W_IN])
    lane = _iota(pre.shape, 1)
    act = jnp.where(lane < 64, jnp.tanh(pre), jnp.where(lane < 128, pre, _sigmoid(pre)))
    second = _dot(act, w2_ref[...])
    p_gla = project(_W_IN + 3 * rw, _W_IN + 3 * rw + 768)
    lw = -jnp.exp(-_softplus(-(vec[3:4] + second[:, 0:rw])) - 0.5)
    a = _sigmoid(vec[4:5] + second[:, rw:2 * rw])
    g = second[:, 2 * rw:3 * rw]

    p = p_rkv[8:]
    ps = prev_rows(p_rkv)
    r = p[:, 0:rw]
    r = r + (ps[:, 0:rw] - r) * vec[0:1]
    k = p[:, rw:2 * rw]
    k = k + (ps[:, rw:2 * rw] - k) * vec[1:2]
    v = p[:, 2 * rw:3 * rw]
    v = v + (ps[:, 2 * rw:3 * rw] - v) * vec[2:3]

    kk = k * vec[6:7]
    kk2 = kk * kk
    ss = jnp.concatenate(_group_sums([kk2[:, :GROUP_W], kk2[:, GROUP_W:]], obd_ref[...]), axis=1)
    kk = kk / jnp.maximum(jnp.sqrt(ss), 1e-12)
    k = k * (1.0 + (a - 1.0) * vec[7:8])
    misc = p_lora[8:, _W_MISC:_W_LORA_H]
    if has_vres:
        mv = jnp.where(_iota(misc.shape, 1) < 64, misc, prev_rows(p_lora[:, _W_MISC:_W_LORA_H]))
        v = v + (vf_ref[0] - v) * _sigmoid(vec[5:6] + _dot(mv, v2_ref[...]))
    p_ret = project(_W_IN + 3 * rw + 768, _WIN_W)

    rwp_ref[0] = r
    rwp_ref[1] = lw
    rwp_ref[2] = k
    rwp_ref[3] = v
    rwp_ref[4] = kk
    rwp_ref[5] = kk * a
    rwp_ref[6] = g

    p = p_gla[8:]
    gr_ref[:, _GQ:_GK] = p[:, 0:128] * (GLA_DK ** -0.5)
    gr_ref[:, _GK:_GLA] = p[:, 128:256]
    la_pre = _dot(misc, ga2_ref[...]) + gab_ref[...]
    gr_ref[:, _GLA:_GV] = -_softplus(-la_pre) * (1.0 / GLA_GATE_TAU)
    gr_ref[:, _GV:_GG] = p[:, 256:512]
    gate = p[:, 512:768]
    gr_ref[:, _GG:_RQ] = gate * _sigmoid(gate)

    p = p_ret[8:]
    o = 0
    cos = cos_ref[...]
    sin = sin_ref[...]
    lo_half = (_iota((tm, 128), 1) % RET_DK) < (RET_DK // 2)

    def rope(t):
        swapped = jnp.where(lo_half, pltpu.roll(t, 128 - RET_DK // 2, 1), pltpu.roll(t, RET_DK // 2, 1))
        return t * cos + swapped * sin

    gr_ref[:, _RQ:_RK] = rope(p[:, o:o + 128]) * (RET_DK ** -0.5)
    gr_ref[:, _RK:_RV] = rope(p[:, o + 128:o + 256])
    gr_ref[:, _RV:_RG] = p[:, o + 256:o + 512]
    gate = p[:, o + 512:o + 768]
    gr_ref[:, _RG:_GR_W] = gate * _sigmoid(gate)


def _prep(x2d, mod, layer, seq, tm, has_vres, n1g, win, w2, ga2, gab, vec, obd, cos_t, sin_t,
          v2=None, rwp_first=None):
    T, D = x2d.shape
    tiles_per_seq = seq // tm
    n_tiles = T // tm
    per_layer = [n1g, win, w2, ga2, gab, vec]
    in_specs = [pl.BlockSpec((tm, D), lambda i: (i, 0)),
                pl.BlockSpec((8, D), lambda i: (jnp.maximum(i * (tm // 8) - 1, 0), 0)),
                pl.BlockSpec((None, 1, 6, D), lambda i: (layer, i // tiles_per_seq, 0, 0))]
    in_specs += [_layer_spec(a.shape, layer) for a in per_layer]
    in_specs += [_const_spec(obd.shape),
                 pl.BlockSpec((tm, 128), lambda i: (i % tiles_per_seq, 0)),
                 pl.BlockSpec((tm, 128), lambda i: (i % tiles_per_seq, 0))]
    args = [x2d, x2d, mod] + per_layer + [obd, cos_t, sin_t]
    if has_vres:
        in_specs += [_layer_spec(v2.shape, layer),
                     pl.BlockSpec((1, tm, RWKV_WIDTH), lambda i: (3, i, 0))]
        args += [v2, rwp_first]
    return pl.pallas_call(
        functools.partial(_prep_kernel, tiles_per_seq, has_vres),
        grid=(n_tiles,),
        in_specs=in_specs,
        out_specs=[pl.BlockSpec((7, tm, RWKV_WIDTH), lambda i: (0, i, 0)),
                   pl.BlockSpec((tm, _GR_W), lambda i: (i, 0))],
        out_shape=[jax.ShapeDtypeStruct((7, T, RWKV_WIDTH), F32),
                   jax.ShapeDtypeStruct((T, _GR_W), F32)],
        compiler_params=pltpu.CompilerParams(
            dimension_semantics=("parallel",), vmem_limit_bytes=VMEM_LIMIT_BYTES),
        name="proj_prep",
    )(*args)


def _chunk_masks(groups):
    C = CHUNK
    row = _iota((C, groups * C), 0)
    col = _iota((C, groups * C), 1) % C
    return row > col, row >= col, (row == col).astype(F32)


def _rwkv_chunk_scaled(r, lw, k, v, kk, kb):
    C = CHUNK
    cl = _cumsum_rows(lw)
    cle = cl[C - 1:C]
    e_neg = jnp.exp(-cl)
    e_end = jnp.exp(cle - cl)
    at = -kk * jnp.exp(cl - lw)
    rt = r * jnp.exp(cl)
    return dict(at=at, rt=rt, ar=jnp.concatenate([at, rt], axis=0), bt=kb * e_neg, kt=k * e_neg,
                bk_end=jnp.concatenate([kb * e_end, k * e_end], axis=0), g_end=jnp.exp(cle), v=v)


def _rwkv_chunk_factors(units, masks, bdm, eye):
    C, G = CHUNK, HEADS_PER_GROUP
    strict, incl, ident = masks
    n = range(len(units))
    at, rt, ar, v = ([u[name] for u in units] for name in ("at", "rt", "ar", "v"))
    sb = [_bd_dot(ar[i], units[i]["bt"], G, nt=True) for i in n]
    sk = [_bd_dot(ar[i], units[i]["kt"], G, nt=True) for i in n]
    yield
    a_rb = [jnp.where(incl, sb[i][C:], 0.0) for i in n]
    a_ak = [jnp.where(strict, sk[i][:C], 0.0) for i in n]
    a_rk = [jnp.where(incl, sk[i][C:], 0.0) for i in n]

    a_ab = [jnp.where(strict, sb[i][:C], 0.0) for i in n]
    tm = [ident + a_ab[i] for i in n]
    xp = [_bd_dot(a_ab[i], a_ab[i], G) for i in n]
    yield
    for _ in range(4):
        rr = [_bd_dot(jnp.concatenate([tm[i], xp[i]], axis=0), xp[i], G) for i in n]
        tm = [tm[i] + rr[i][:C] for i in n]
        xp = [rr[i][C:] for i in n]
        yield
    tm = [tm[i] + _bd_dot(tm[i], xp[i], G) for i in n]
    yield

    vv = [_bd_dot(jnp.concatenate([a_ak[i], a_rk[i]], axis=0), v[i], G) for i in n]
    yield
    wt = [_bd_dot(tm[i], at[i], G) for i in n]
    yield
    ut = [_bd_dot(tm[i], vv[i][:C], G) for i in n]
    yield
    qh = [rt[i] + _bd_dot(a_rb[i], wt[i], G) for i in n]
    yield
    yh = [_bd_dot(a_rb[i], ut[i], G) + vv[i][C:] for i in n]
    yield
    zero = jnp.zeros((C, GROUP_W), F32)
    pz = [_dot_tn(units[i]["bk_end"],
                  jnp.concatenate([jnp.concatenate([wt[i], ut[i]], axis=1),
                                   jnp.concatenate([zero, v[i]], axis=1)], axis=0)) for i in n]
    pm = [jnp.where(bdm, pz[i][:, :GROUP_W], 0.0) + jnp.where(eye, units[i]["g_end"], 0.0) for i in n]
    zm = [jnp.where(bdm, pz[i][:, GROUP_W:], 0.0) for i in n]
    return qh, yh, pm, zm


def _interleave(main, side):
    side_done, side_out = False, None
    while True:
        try:
            next(main)
        except StopIteration as stop:
            main_out = stop.value
            break
        if not side_done:
            try:
                next(side)
            except StopIteration as stop:
                side_done, side_out = True, stop.value
    while not side_done:
        try:
            next(side)
        except StopIteration as stop:
            side_done, side_out = True, stop.value
    return main_out, side_out


def _rwkv_kernel(cb, rwp_ref, par_ref, o_ref, st_ref):
    C, W, N = CHUNK, GROUP_W, HEAD_DV
    ngroups = o_ref.shape[1] // W

    @pl.when(pl.program_id(1) == 0)
    def _():
        st_ref[...] = jnp.zeros_like(st_ref)

    masks = _chunk_masks(HEADS_PER_GROUP)
    bdm = _block_mask(W, W, N, N)
    eye = _iota((W, W), 0) == _iota((W, W), 1)
    ones_bd = bdm.astype(F32)
    m = [st_ref[gi] for gi in range(ngroups)]

    def wave_units(chunks):
        return [(slice(j * C, (j + 1) * C), slice(gi * W, (gi + 1) * W), gi)
                for j in chunks for gi in range(ngroups)]

    def factors(units):
        scaled = [_rwkv_chunk_scaled(*[rwp_ref[i, rows, lanes] for i in range(6)])
                  for rows, lanes, _ in units]
        return _rwkv_chunk_factors(scaled, masks, bdm, eye)

    def chain(units, fac):
        qh, yh, pm, zm = fac
        y = []
        for u, (_, _, gi) in enumerate(units):
            ym = _dot(jnp.concatenate([qh[u], pm[u]], axis=0), m[gi])
            y.append(ym[:C] + yh[u])
            m[gi] = ym[C:] + zm[u]
            yield
        return y

    def epilogue(units, y):
        n = range(len(units))
        par = [par_ref[:, lanes] for _, lanes, _ in units]
        rkr = [rwp_ref[0, rows, lanes] * rwp_ref[2, rows, lanes] * par[u][0:1]
               for u, (rows, lanes, _) in enumerate(units)]
        sums = _group_sums(y + rkr, ones_bd)
        yield
        yc = [y[u] - sums[u] * (1.0 / N) for u in n]
        var = _group_sums([yc[u] * yc[u] for u in n], ones_bd)
        yield
        for u, (rows, lanes, _) in enumerate(units):
            yn = yc[u] * lax.rsqrt(var[u] * (1.0 / N) + RWKV_GN_EPS) * par[u][1:2] + par[u][2:3]
            o_ref[rows, lanes] = ((yn + sums[len(units) + u] * rwp_ref[3, rows, lanes])
                                  * rwp_ref[6, rows, lanes])
            yield

    def in_turn(*gens):
        outs = []
        for g in gens:
            outs.append((yield from g))
        return outs

    nw = RWKV_WAVES_PER_STEP
    per_wave = cb // nw
    units = [wave_units(range(w * per_wave, (w + 1) * per_wave)) for w in range(nw)]
    fac, ys = [None] * nw, [None] * nw
    for k in range(nw):
        side = []
        if k >= 1:
            side.append(chain(units[k - 1], fac[k - 1]))
        if k >= 2:
            side.append(epilogue(units[k - 2], ys[k - 2]))
        fac[k], outs = _interleave(factors(units[k]), in_turn(*side))
        if k >= 1:
            ys[k - 1] = outs[0]
    drain = epilogue(units[nw - 2], ys[nw - 2]) if nw >= 2 else iter(())
    _, ys[nw - 1] = _interleave(drain, chain(units[nw - 1], fac[nw - 1]))
    _interleave(epilogue(units[nw - 1], ys[nw - 1]), iter(()))
    for gi in range(ngroups):
        st_ref[gi] = m[gi]


def _rwkv(rwp, par, batch, seq, cb):
    _, T, RW = rwp.shape
    steps = seq // (CHUNK * cb)
    return pl.pallas_call(
        functools.partial(_rwkv_kernel, cb),
        grid=(batch, steps),
        in_specs=[pl.BlockSpec((7, cb * CHUNK, RW), lambda b, c: (0, b * steps + c, 0)),
                  _const_spec(par.shape)],
        out_specs=pl.BlockSpec((cb * CHUNK, RW), lambda b, c: (b * steps + c, 0)),
        out_shape=jax.ShapeDtypeStruct((T, RW), F32),
        scratch_shapes=[pltpu.VMEM((RW // GROUP_W, GROUP_W, GROUP_W), F32)],
        compiler_params=pltpu.CompilerParams(
            dimension_semantics=("parallel", "arbitrary"), vmem_limit_bytes=VMEM_LIMIT_BYTES),
        name="rwkv7_chunk",
    )(rwp, par)


def _glaret_kernel(cb, gr_ref, lng_ref, dec_ref, kdec_ref, qdec_ref, cdec_ref, o_ref, sg_ref, sr_ref):
    C, G, DV = CHUNK, GLA_HEADS, HEAD_DV
    WV = G * DV

    @pl.when(pl.program_id(1) == 0)
    def _():
        sg_ref[...] = jnp.zeros_like(sg_ref)
        sr_ref[...] = jnp.zeros_like(sr_ref)

    _, incl, _ = _chunk_masks(G)
    ones_bd = _block_mask(WV, WV, DV, DV).astype(F32)
    st_mask = _block_mask(WV, G * GLA_DK, DV, GLA_DK)
    lng = lng_ref[...]
    dec = dec_ref[...]
    kdec = kdec_ref[...]
    qdec = qdec_ref[...]
    cdec = cdec_ref[...]

    n = range(cb)
    rows = [slice(j * C, (j + 1) * C) for j in n]
    gq = [gr_ref[rows[j], _GQ:_GK] for j in n]
    gk = [gr_ref[rows[j], _GK:_GLA] for j in n]
    gv = [gr_ref[rows[j], _GV:_GG] for j in n]
    bc = [_cumsum_rows(gr_ref[rows[j], _GLA:_GV]) for j in n]
    be = [bc[j][C - 1:C] for j in n]
    mid = [bc[j][C // 2 - 1:C // 2] for j in n]
    ep = [jnp.exp(bc[j] - mid[j]) for j in n]
    en = [jnp.exp(mid[j] - bc[j]) for j in n]
    qp = [gq[j] * jnp.exp(bc[j]) for j in n]
    att_lo = [_dot_nt(gq[j] * ep[j], _bd_rows(gk[j] * en[j], G)) for j in n]
    att_hi = [_dot_nt(gq[j] * en[j], _bd_rows(gk[j] * ep[j], G)) for j in n]
    rq = [gr_ref[rows[j], _RQ:_RK] for j in n]
    rk = [gr_ref[rows[j], _RK:_RV] for j in n]
    rv = [gr_ref[rows[j], _RV:_RG] for j in n]
    sc = [_dot_nt(rq[j], _bd_rows(rk[j], G)) * dec for j in n]
    g_intra = [_bd_dot(jnp.where(incl, att_lo[j], att_hi[j]), gv[j], G) for j in n]
    r_intra = [_bd_dot(sc[j], rv[j], G) for j in n]
    g_kv = [jnp.where(st_mask, _dot_tn(gv[j], gk[j] * jnp.exp(be[j] - bc[j])), 0.0) for j in n]
    r_kv = [jnp.where(st_mask, _dot_tn(rv[j], rk[j] * kdec), 0.0) for j in n]

    sg = sg_ref[...]
    sr = sr_ref[...]
    g_o = []
    r_o = []
    for j in n:
        g_o.append(g_intra[j] + _dot_nt(qp[j], sg))
        sg = sg * jnp.exp(be[j]) + g_kv[j]
        r_o.append(r_intra[j] + _dot_nt(rq[j] * qdec, sr))
        sr = sr * cdec + r_kv[j]
    sg_ref[...] = sg
    sr_ref[...] = sr

    sums = _group_sums([g_o[j] * g_o[j] for j in n] + r_o, ones_bd)
    r_c = [r_o[j] - sums[cb + j] * (1.0 / DV) for j in n]
    r_var = _group_sums([r_c[j] * r_c[j] for j in n], ones_bd)
    for j in n:
        o_ref[rows[j], 0:WV] = (g_o[j] * lax.rsqrt(sums[j] * (1.0 / DV) + EPS) * lng
                                * gr_ref[rows[j], _GG:_RQ])
        o_ref[rows[j], WV:2 * WV] = (r_c[j] * lax.rsqrt(r_var[j] * (1.0 / DV) + EPS)
                                     * gr_ref[rows[j], _RG:_GR_W])


def _glaret(gr, lng, dec, kdec, qdec, cdec, batch, seq, cb):
    T = gr.shape[0]
    nc = seq // (CHUNK * cb)
    wv = GLA_HEADS * HEAD_DV
    consts = [lng, dec, kdec, qdec, cdec]
    return pl.pallas_call(
        functools.partial(_glaret_kernel, cb),
        grid=(batch, nc),
        in_specs=[pl.BlockSpec((cb * CHUNK, _GR_W), lambda b, c: (b * nc + c, 0))]
                 + [_const_spec(a.shape) for a in consts],
        out_specs=pl.BlockSpec((cb * CHUNK, 2 * wv), lambda b, c: (b * nc + c, 0)),
        out_shape=jax.ShapeDtypeStruct((T, 2 * wv), F32),
        scratch_shapes=[pltpu.VMEM((wv, GLA_HEADS * GLA_DK), F32),
                        pltpu.VMEM((wv, RET_HEADS * RET_DK), F32)],
        compiler_params=pltpu.CompilerParams(
            dimension_semantics=("parallel", "arbitrary"), vmem_limit_bytes=VMEM_LIMIT_BYTES),
        name="gla_retention_chunk",
    )(gr, *consts)


def _outffn_kernel(final_norm, x_ref, ya_ref, ybc_ref, mod_ref, n2g_ref, wo_ref, wg_ref, wu_ref,
                   wd_ref, nfg_ref, o_ref):
    mod = mod_ref[0]
    y = jnp.concatenate([ya_ref[...], ybc_ref[...]], axis=-1)
    x = x_ref[...] + mod[2:3] * _dot(y, wo_ref[...])
    ms = jnp.mean(x * x, axis=-1, keepdims=True)
    h = (x * lax.rsqrt(ms + EPS) * (n2g_ref[...] * (1.0 + mod[4:5])) + mod[3:4]).astype(BF16)
    gate = jnp.dot(h, wg_ref[...], preferred_element_type=F32)
    up = jnp.dot(h, wu_ref[...], preferred_element_type=F32)
    x = x + mod[5:6] * _dot(gate * _sigmoid(gate) * up, wd_ref[...])
    if final_norm:
        ms = jnp.mean(x * x, axis=-1, keepdims=True)
        x = x * lax.rsqrt(ms + EPS) * nfg_ref[...]
    o_ref[...] = x


def _outffn(x2d, ya, ybc, mod, layer, seq, tm, final_norm, n2g, wo, wg, wu, wd, nfg):
    T, D = x2d.shape
    tiles_per_seq = seq // tm
    per_layer = [n2g, wo, wg, wu, wd]
    return pl.pallas_call(
        functools.partial(_outffn_kernel, final_norm),
        grid=(T // tm,),
        in_specs=[pl.BlockSpec((tm, D), lambda i: (i, 0)),
                  pl.BlockSpec((tm, ya.shape[1]), lambda i: (i, 0)),
                  pl.BlockSpec((tm, ybc.shape[1]), lambda i: (i, 0)),
                  pl.BlockSpec((None, 1, 6, D), lambda i: (layer, i // tiles_per_seq, 0, 0))]
                 + [_layer_spec(a.shape, layer) for a in per_layer] + [_const_spec(nfg.shape)],
        out_specs=pl.BlockSpec((tm, D), lambda i: (i, 0)),
        out_shape=jax.ShapeDtypeStruct((T, D), F32),
        compiler_params=pltpu.CompilerParams(
            dimension_semantics=("parallel",), vmem_limit_bytes=VMEM_LIMIT_BYTES),
        name="outproj_swiglu",
    )(x2d, ya, ybc, mod, *per_layer, nfg)


def _rope_tables(seq):
    half = RET_DK // 2
    inv_freq = ROPE_BASE ** (-jnp.arange(half, dtype=F32) / half)
    ang = jnp.arange(seq, dtype=F32)[:, None] * inv_freq[None, :]
    cos, sin = jnp.cos(ang), jnp.sin(ang)
    cos_t = jnp.tile(jnp.concatenate([cos, cos], axis=-1), (1, RET_HEADS))
    sin_t = jnp.tile(jnp.concatenate([-sin, sin], axis=-1), (1, RET_HEADS))
    return cos_t, sin_t


def _retention_tables():
    H, C = RET_HEADS, CHUNK
    log_gamma = jnp.log1p(-(2.0 ** (-5.0 - jnp.arange(H, dtype=F32))))
    pos = jnp.arange(C, dtype=F32)
    intra = jnp.exp(log_gamma[:, None, None] * jnp.abs(pos[:, None] - pos[None, :]))
    dec = jnp.transpose(intra, (1, 0, 2)).reshape(C, H * C)
    k_dec = jnp.exp(log_gamma[None, :] * (C - 1.0 - pos)[:, None])
    q_dec = jnp.exp(log_gamma[None, :] * (pos + 1.0)[:, None])
    chunk_dec = jnp.exp(log_gamma * C)
    kdec = jnp.repeat(k_dec, RET_DK, axis=1)
    qdec = jnp.repeat(q_dec, RET_DK, axis=1)
    cdec = jnp.repeat(chunk_dec, RET_DK)[None, :]
    return dec, kdec, qdec, cdec


def kernel(x, c, ada_w, ada_b, norm1_g, norm2_g, w_in, w_out, rk_mu_rkv, rk_mu_x, rk_w0, rk_w1, rk_w2, rk_a0, rk_a1, rk_a2, rk_g1, rk_g2, rk_k_k, rk_k_a, rk_r_k, rk_ln_g, rk_ln_b, rk_mu_v, rk_v0, rk_v1, rk_v2, gla_a1, gla_a2, gla_ab, gla_ln_g, ffn_w_gate, ffn_w_up, ffn_w_down, norm_f_g):
    B, S, D = x.shape
    L = ada_w.shape[0]
    T = B * S
    assert S % PREP_TOKENS_PER_STEP == 0 and S % FFN_TOKENS_PER_STEP == 0
    assert S % (CHUNK * RWKV_CHUNKS_PER_STEP) == 0 and S % (CHUNK * GLARET_CHUNKS_PER_STEP) == 0

    mod = _adaln(c, ada_w, ada_b).reshape(L, B, 6, D)
    cos_t, sin_t = _rope_tables(S)
    dec, kdec, qdec, cdec = _retention_tables()
    obd = _block_mask(GROUP_W, GROUP_W, HEAD_DV, HEAD_DV).astype(BF16)
    RW = RWKV_WIDTH

    def zeros(*shape):
        return jnp.zeros(shape, F32)

    mu_v = jnp.concatenate([zeros(1, D), rk_mu_v], axis=0)
    v0 = jnp.concatenate([zeros(1, RW), rk_v0], axis=0)
    v1 = jnp.concatenate([zeros(1, D, rk_v1.shape[2]), rk_v1], axis=0)
    v2 = jnp.concatenate([zeros(1, rk_v2.shape[1], RW), rk_v2], axis=0)

    def on_h(mu, w):
        return (1.0 - mu)[:, :, None] * w

    def on_prev(mu, w):
        return mu[:, :, None] * w

    mu = rk_mu_x
    misc = jnp.concatenate([gla_a1, zeros(L, D, 16), on_h(mu_v, v1), on_prev(mu_v, v1), zeros(L, D, 32)],
                           axis=2)
    lora_h = jnp.concatenate([on_h(mu[:, 0], rk_w1), on_h(mu[:, 1], rk_a1), on_h(mu[:, 2], rk_g1)], axis=2)
    lora_s = jnp.concatenate([on_prev(mu[:, 0], rk_w1), on_prev(mu[:, 1], rk_a1), on_prev(mu[:, 2], rk_g1)],
                             axis=2)
    win = jnp.concatenate([t.astype(BF16) for t in (misc, lora_h, lora_s, w_in)], axis=2)
    assert win.shape[2] == _WIN_W and misc.shape[2] == _W_LORA_H - _W_MISC
    w2 = jnp.concatenate([
        jnp.concatenate([rk_w2, zeros(L, 64, 2 * RW)], axis=2),
        jnp.concatenate([zeros(L, 64, RW), rk_a2, zeros(L, 64, RW)], axis=2),
        jnp.concatenate([zeros(L, 128, 2 * RW), rk_g2], axis=2)], axis=1).astype(BF16)
    ga2 = jnp.concatenate([gla_a2, zeros(L, 128 - gla_a2.shape[1], gla_a2.shape[2])], axis=1).astype(BF16)
    v2p = jnp.concatenate([zeros(L, 32, RW), v2, v2, zeros(L, 32, RW)], axis=1).astype(BF16)
    vec = jnp.stack([rk_mu_rkv[:, 0], rk_mu_rkv[:, 1], rk_mu_rkv[:, 2], rk_w0, rk_a0, v0, rk_k_k, rk_k_a],
                    axis=1)
    par = jnp.stack([rk_r_k.reshape(L, RW), rk_ln_g, rk_ln_b], axis=1)
    lng = jnp.tile(gla_ln_g, (1, GLA_HEADS))[:, None, :]
    wo, wg, wu, wd = (w.astype(BF16) for w in (w_out, ffn_w_gate, ffn_w_up, ffn_w_down))

    x2d = x.reshape(T, D)
    rwp_first = None
    for l in range(L):
        has_vres = l > 0
        extra = dict(v2=v2p, rwp_first=rwp_first) if has_vres else {}
        rwp, gr = _prep(x2d, mod, l, S, PREP_TOKENS_PER_STEP, has_vres, norm1_g[:, None, :], win, w2, ga2,
                        gla_ab[:, None, :], vec, obd, cos_t, sin_t, **extra)
        if l == 0:
            rwp_first = rwp
        ya = _rwkv(rwp, par[l], B, S, RWKV_CHUNKS_PER_STEP)
        ybc = _glaret(gr, lng[l], dec, kdec, qdec, cdec, B, S, GLARET_CHUNKS_PER_STEP)
        x2d = _outffn(x2d, ya, ybc, mod, l, S, FFN_TOKENS_PER_STEP, l == L - 1, norm2_g[:, None, :],
                      wo, wg, wu, wd, norm_f_g[None])
    return x2d.reshape(B, S, D)
```

```python
import functools

import numpy as np
import jax
import jax.numpy as jnp
from jax import lax
from jax.experimental import pallas as pl
from jax.experimental.pallas import tpu as pltpu

F32 = jnp.float32
BF16 = jnp.bfloat16

CHUNK = 64
EPS = 1e-6
HEAD_DV = 64
RWKV_HEADS = 8
RWKV_WIDTH = RWKV_HEADS * HEAD_DV
RWKV_GN_EPS = 64e-5
GLA_HEADS = 4
GLA_DK = 32
GLA_GATE_TAU = 16.0
RET_HEADS = 4
RET_DK = 32
ROPE_BASE = 10000.0
HEADS_PER_GROUP = 4
GROUP_W = HEADS_PER_GROUP * HEAD_DV
VMEM_LIMIT_BYTES = 56 * 1024 * 1024
RWKV_CHUNKS_PER_STEP = 16
RWKV_WAVES_PER_STEP = 4
GLARET_CHUNKS_PER_STEP = 16
PREP_TOKENS_PER_STEP = 256
FFN_TOKENS_PER_STEP = 512

_GQ, _GK, _GLA, _GV, _GG, _RQ, _RK, _RV, _RG, _GR_W = 0, 128, 256, 384, 640, 896, 1024, 1152, 1408, 1664
_W_MISC, _W_LORA_H, _W_LORA_S, _W_IN, _WIN_W = 0, 128, 384, 640, 3712


def _dot(a, b):
    return jnp.dot(a.astype(BF16), b.astype(BF16), preferred_element_type=F32)


def _dot_nt(a, b):
    return lax.dot_general(a.astype(BF16), b.astype(BF16), (((1,), (1,)), ((), ())),
                           preferred_element_type=F32)


def _dot_tn(a, b):
    return lax.dot_general(a.astype(BF16), b.astype(BF16), (((0,), (0,)), ((), ())),
                           preferred_element_type=F32)


def _group_sums(xs, ones_bd, coarse=()):
    hi = [x.astype(BF16) for x in xs]
    lo = [(x - h.astype(F32)).astype(BF16) for x, h in zip(xs, hi)]
    terms = hi + [x.astype(BF16) for x in coarse] + lo
    s = jnp.dot(jnp.concatenate(terms, axis=0), ones_bd.astype(BF16), preferred_element_type=F32)
    offs = np.cumsum([0] + [t.shape[0] for t in terms])
    nx, nc = len(xs), len(coarse)
    out = [s[offs[i]:offs[i + 1]] + s[offs[nx + nc + i]:offs[nx + nc + i + 1]] for i in range(nx)]
    return out + [s[offs[nx + i]:offs[nx + i + 1]] for i in range(nc)]


def _cumsum_rows(x):
    n = x.shape[0]
    row = _iota(x.shape, 0)
    s = 1
    while s < n:
        x = x + jnp.where(row >= s, pltpu.roll(x, s, 0), 0.0)
        s *= 2
    return x


def _sigmoid(x):
    return 1.0 / (1.0 + jnp.exp(-x))


def _softplus(x):
    return jnp.maximum(x, 0.0) + jnp.log(1.0 + jnp.exp(-jnp.abs(x)))


def _iota(shape, axis):
    return lax.broadcasted_iota(jnp.int32, shape, axis)


def _bd_rows(x, groups):
    c, w = x.shape
    n = w // groups
    t = jnp.concatenate([x] * groups, axis=0)
    keep = (_iota(t.shape, 0) // c) == (_iota(t.shape, 1) // n)
    return jnp.where(keep, t, 0.0)


def _bd_dot(x, y, groups, nt=False):
    e = _bd_rows(y, groups)
    return _dot_nt(x, e) if nt else _dot(x, e)


def _block_mask(rows, cols, rblk, cblk):
    return (_iota((rows, cols), 0) // rblk) == (_iota((rows, cols), 1) // cblk)


def _const_spec(shape):
    nd = len(shape)
    return pl.BlockSpec(shape, lambda *_: (0,) * nd, pipeline_mode=pl.Buffered(1))


def _layer_spec(shape, layer):
    nd = len(shape)
    return pl.BlockSpec((None,) + tuple(shape[1:]), lambda *_: (layer,) + (0,) * (nd - 1),
                        pipeline_mode=pl.Buffered(1))


def _adaln_kernel(c_ref, w_ref, b_ref, o_ref):
    c = c_ref[...]
    cond = c * _sigmoid(c)
    o_ref[0] = _dot(cond, w_ref[0]) + b_ref[0]


def _adaln(c, ada_w, ada_b):
    L, D, D6 = ada_w.shape
    B = c.shape[0]
    tn = 1536
    return pl.pallas_call(
        _adaln_kernel,
        grid=(L, D6 // tn),
        in_specs=[pl.BlockSpec((B, D), lambda l, j: (0, 0)),
                  pl.BlockSpec((1, D, tn), lambda l, j: (l, 0, j)),
                  pl.BlockSpec((1, 1, tn), lambda l, j: (l, 0, j))],
        out_specs=pl.BlockSpec((1, B, tn), lambda l, j: (l, 0, j)),
        out_shape=jax.ShapeDtypeStruct((L, B, D6), F32),
        compiler_params=pltpu.CompilerParams(
            dimension_semantics=("arbitrary", "arbitrary"), vmem_limit_bytes=VMEM_LIMIT_BYTES),
        name="adaln_mod",
    )(c, ada_w, ada_b.reshape(L, 1, D6))


def _prep_kernel(tiles_per_seq, has_vres, *refs):
    if has_vres:
        (x_ref, xh_ref, mod_ref, n1g_ref, win_ref, w2_ref, ga2_ref, gab_ref, vec_ref, obd_ref,
         cos_ref, sin_ref, v2_ref, vf_ref, rwp_ref, gr_ref) = refs
    else:
        (x_ref, xh_ref, mod_ref, n1g_ref, win_ref, w2_ref, ga2_ref, gab_ref, vec_ref, obd_ref,
         cos_ref, sin_ref, rwp_ref, gr_ref) = refs
    tm = x_ref.shape[0]
    rw = RWKV_WIDTH
    first = (pl.program_id(0) % tiles_per_seq) == 0

    xe = jnp.concatenate([xh_ref[...], x_ref[...]], axis=0)
    mod = mod_ref[0]
    ms = jnp.mean(xe * xe, axis=-1, keepdims=True)
    he = xe * lax.rsqrt(ms + EPS) * (n1g_ref[...] * (1.0 + mod[1:2])) + mod[0:1]
    he = jnp.concatenate([jnp.where(first, 0.0, he[:8]), he[8:]], axis=0)

    heb = he.astype(BF16)

    def project(lo, hi):
        return jnp.dot(heb, win_ref[:, lo:hi], preferred_element_type=F32)

    def prev_rows(t):
        return pltpu.roll(t, 1, 0)[8:]

    p_lora = project(_W_MISC, _W_IN)
    p_rkv = project(_W_IN, _W_IN + 3 * rw)
    vec = vec_ref[...]

    pre = p_lora[8:, _W_LORA_H:_W_LORA_S] + prev_rows(p_lora[:, _W_LORA_S:_W_IN])
    lane = _iota(pre.shape, 1)
    act = jnp.where(lane < 64, jnp.tanh(pre), jnp.where(lane < 128, pre, _sigmoid(pre)))
    second = _dot(act, w2_ref[...])
    p_gla = project(_W_IN + 3 * rw, _W_IN + 3 * rw + 768)
    lw = -jnp.exp(-_softplus(-(vec[3:4] + second[:, 0:rw])) - 0.5)
    a = _sigmoid(vec[4:5] + second[:, rw:2 * rw])
    g = second[:, 2 * rw:3 * rw]

    p = p_rkv[8:]
    ps = prev_rows(p_rkv)
    r = p[:, 0:rw]
    r = r + (ps[:, 0:rw] - r) * vec[0:1]
    k = p[:, rw:2 * rw]
    k = k + (ps[:, rw:2 * rw] - k) * vec[1:2]
    v = p[:, 2 * rw:3 * rw]
    v = v + (ps[:, 2 * rw:3 * rw] - v) * vec[2:3]

    kk = k * vec[6:7]
    kk2 = kk * kk
    ss = jnp.concatenate(_group_sums([], obd_ref[...], coarse=[kk2[:, :GROUP_W], kk2[:, GROUP_W:]]), axis=1)
    kk = kk / jnp.maximum(jnp.sqrt(ss), 1e-12)
    k = k * (1.0 + (a - 1.0) * vec[7:8])
    misc = p_lora[8:, _W_MISC:_W_LORA_H]
    if has_vres:
        mv = jnp.where(_iota(misc.shape, 1) < 64, misc, prev_rows(p_lora[:, _W_MISC:_W_LORA_H]))
        v = v + (vf_ref[0] - v) * _sigmoid(vec[5:6] + _dot(mv, v2_ref[...]))
    p_ret = project(_W_IN + 3 * rw + 768, _WIN_W)

    rwp_ref[0] = r
    rwp_ref[1] = lw
    rwp_ref[2] = k
    rwp_ref[3] = v
    rwp_ref[4] = kk
    rwp_ref[5] = kk * a
    rwp_ref[6] = g

    p = p_gla[8:]
    gr_ref[:, _GQ:_GK] = p[:, 0:128] * (GLA_DK ** -0.5)
    gr_ref[:, _GK:_GLA] = p[:, 128:256]
    la_pre = _dot(misc, ga2_ref[...]) + gab_ref[...]
    gr_ref[:, _GLA:_GV] = -_softplus(-la_pre) * (1.0 / GLA_GATE_TAU)
    gr_ref[:, _GV:_GG] = p[:, 256:512]
    gate = p[:, 512:768]
    gr_ref[:, _GG:_RQ] = gate * _sigmoid(gate)

    p = p_ret[8:]
    o = 0
    cos = cos_ref[...]
    sin = sin_ref[...]
    lo_half = (_iota((tm, 128), 1) % RET_DK) < (RET_DK // 2)

    def rope(t):
        swapped = jnp.where(lo_half, pltpu.roll(t, 128 - RET_DK // 2, 1), pltpu.roll(t, RET_DK // 2, 1))
        return t * cos + swapped * sin

    gr_ref[:, _RQ:_RK] = rope(p[:, o:o + 128]) * (RET_DK ** -0.5)
    gr_ref[:, _RK:_RV] = rope(p[:, o + 128:o + 256])
    gr_ref[:, _RV:_RG] = p[:, o + 256:o + 512]
    gate = p[:, o + 512:o + 768]
    gr_ref[:, _RG:_GR_W] = gate * _sigmoid(gate)


def _prep(x2d, mod, layer, seq, tm, has_vres, n1g, win, w2, ga2, gab, vec, obd, cos_t, sin_t,
          v2=None, rwp_first=None):
    T, D = x2d.shape
    tiles_per_seq = seq // tm
    n_tiles = T // tm
    per_layer = [n1g, win, w2, ga2, gab, vec]
    in_specs = [pl.BlockSpec((tm, D), lambda i: (i, 0)),
                pl.BlockSpec((8, D), lambda i: (jnp.maximum(i * (tm // 8) - 1, 0), 0)),
                pl.BlockSpec((None, 1, 6, D), lambda i: (layer, i // tiles_per_seq, 0, 0))]
    in_specs += [_layer_spec(a.shape, layer) for a in per_layer]
    in_specs += [_const_spec(obd.shape),
                 pl.BlockSpec((tm, 128), lambda i: (i % tiles_per_seq, 0)),
                 pl.BlockSpec((tm, 128), lambda i: (i % tiles_per_seq, 0))]
    args = [x2d, x2d, mod] + per_layer + [obd, cos_t, sin_t]
    if has_vres:
        in_specs += [_layer_spec(v2.shape, layer),
                     pl.BlockSpec((1, tm, RWKV_WIDTH), lambda i: (3, i, 0))]
        args += [v2, rwp_first]
    return pl.pallas_call(
        functools.partial(_prep_kernel, tiles_per_seq, has_vres),
        grid=(n_tiles,),
        in_specs=in_specs,
        out_specs=[pl.BlockSpec((7, tm, RWKV_WIDTH), lambda i: (0, i, 0)),
                   pl.BlockSpec((tm, _GR_W), lambda i: (i, 0))],
        out_shape=[jax.ShapeDtypeStruct((7, T, RWKV_WIDTH), F32),
                   jax.ShapeDtypeStruct((T, _GR_W), F32)],
        compiler_params=pltpu.CompilerParams(
            dimension_semantics=("parallel",), vmem_limit_bytes=VMEM_LIMIT_BYTES),
        name="proj_prep",
    )(*args)


def _chunk_masks(groups):
    C = CHUNK
    row = _iota((C, groups * C), 0)
    col = _iota((C, groups * C), 1) % C
    return row > col, row >= col, (row == col).astype(F32)


def _rwkv_chunk_scaled(r, lw, k, v, kk, kb):
    C = CHUNK
    cl = _cumsum_rows(lw)
    cle = cl[C - 1:C]
    e_neg = jnp.exp(-cl)
    e_end = jnp.exp(cle - cl)
    at = -kk * jnp.exp(cl - lw)
    rt = r * jnp.exp(cl)
    return dict(at=at, rt=rt, ar=jnp.concatenate([at, rt], axis=0), bt=kb * e_neg, kt=k * e_neg,
                bk_end=jnp.concatenate([kb * e_end, k * e_end], axis=0), g_end=jnp.exp(cle), v=v)


def _rwkv_chunk_factors(units, masks):
    C, G = CHUNK, HEADS_PER_GROUP
    strict, incl, ident = masks
    n = range(len(units))
    at, rt, ar, v = ([u[name] for u in units] for name in ("at", "rt", "ar", "v"))
    sb = [_bd_dot(ar[i], units[i]["bt"], G, nt=True) for i in n]
    sk = [_bd_dot(ar[i], units[i]["kt"], G, nt=True) for i in n]
    yield
    a_rb = [jnp.where(incl, sb[i][C:], 0.0) for i in n]
    a_ak = [jnp.where(strict, sk[i][:C], 0.0) for i in n]
    a_rk = [jnp.where(incl, sk[i][C:], 0.0) for i in n]

    a_ab = [jnp.where(strict, sb[i][:C], 0.0) for i in n]
    tm = [ident + a_ab[i] for i in n]
    xp = [_bd_dot(a_ab[i], a_ab[i], G) for i in n]
    yield
    for _ in range(4):
        rr = [_bd_dot(jnp.concatenate([tm[i], xp[i]], axis=0), xp[i], G) for i in n]
        tm = [tm[i] + rr[i][:C] for i in n]
        xp = [rr[i][C:] for i in n]
        yield
    tm = [tm[i] + _bd_dot(tm[i], xp[i], G) for i in n]
    yield

    vv = [_bd_dot(jnp.concatenate([a_ak[i], a_rk[i]], axis=0), v[i], G) for i in n]
    yield
    wt = [_bd_dot(tm[i], at[i], G) for i in n]
    yield
    ut = [_bd_dot(tm[i], vv[i][:C], G) for i in n]
    yield
    qh = [rt[i] + _bd_dot(a_rb[i], wt[i], G) for i in n]
    yield
    yh = [_bd_dot(a_rb[i], ut[i], G) + vv[i][C:] for i in n]
    yield
    zero = jnp.zeros((C, 128), F32)
    left = _iota((C, 128), 1) < HEAD_DV
    pm, zm = [[] for _ in n], [[] for _ in n]
    for s in range(GROUP_W // 128):
        sl = slice(128 * s, 128 * (s + 1))
        pz = [_dot_tn(units[i]["bk_end"][:, sl],
                      jnp.concatenate([jnp.concatenate([wt[i][:, sl], ut[i][:, sl]], axis=1),
                                       jnp.concatenate([zero, v[i][:, sl]], axis=1)], axis=0)) for i in n]
        for i in n:
            pm[i].append(jnp.where(left, pz[i][:C, :128], pz[i][C:, :128]))
            zm[i].append(jnp.where(left, pz[i][:C, 128:], pz[i][C:, 128:]))
        yield
    pm = [jnp.concatenate(pm[i], axis=1) + jnp.where(ident > 0, units[i]["g_end"], 0.0) for i in n]
    zm = [jnp.concatenate(zm[i], axis=1) for i in n]
    return qh, yh, pm, zm


def _interleave(main, side):
    side_done, side_out = False, None
    while True:
        try:
            next(main)
        except StopIteration as stop:
            main_out = stop.value
            break
        if not side_done:
            try:
                next(side)
            except StopIteration as stop:
                side_done, side_out = True, stop.value
    while not side_done:
        try:
            next(side)
        except StopIteration as stop:
            side_done, side_out = True, stop.value
    return main_out, side_out


def _rwkv_kernel(cb, rwp_ref, par_ref, o_ref, st_ref):
    C, W, N = CHUNK, GROUP_W, HEAD_DV
    ngroups = o_ref.shape[1] // W

    @pl.when(pl.program_id(1) == 0)
    def _():
        st_ref[...] = jnp.zeros_like(st_ref)

    masks = _chunk_masks(HEADS_PER_GROUP)
    ones_bd = _block_mask(W, W, N, N).astype(F32)
    m = [st_ref[gi] for gi in range(ngroups)]

    def wave_units(chunks):
        return [(slice(j * C, (j + 1) * C), slice(gi * W, (gi + 1) * W), gi)
                for j in chunks for gi in range(ngroups)]

    def factors(units):
        scaled = [_rwkv_chunk_scaled(*[rwp_ref[i, rows, lanes] for i in range(6)])
                  for rows, lanes, _ in units]
        return _rwkv_chunk_factors(scaled, masks)

    def chain(units, fac):
        qh, yh, pm, zm = fac
        y = []
        for u, (_, _, gi) in enumerate(units):
            ym = _dot(jnp.concatenate([qh[u], pm[u]], axis=0), _bd_rows(m[gi], HEADS_PER_GROUP))
            y.append(ym[:C] + yh[u])
            m[gi] = ym[C:] + zm[u]
            yield
        return y

    def epilogue(units, y):
        n = range(len(units))
        par = [par_ref[:, lanes] for _, lanes, _ in units]
        rkr = [rwp_ref[0, rows, lanes] * rwp_ref[2, rows, lanes] * par[u][0:1]
               for u, (rows, lanes, _) in enumerate(units)]
        sums = _group_sums(y + rkr, ones_bd)
        yield
        yc = [y[u] - sums[u] * (1.0 / N) for u in n]
        var = _group_sums([], ones_bd, coarse=[yc[u] * yc[u] for u in n])
        yield
        for u, (rows, lanes, _) in enumerate(units):
            yn = yc[u] * lax.rsqrt(var[u] * (1.0 / N) + RWKV_GN_EPS) * par[u][1:2] + par[u][2:3]
            o_ref[rows, lanes] = ((yn + sums[len(units) + u] * rwp_ref[3, rows, lanes])
                                  * rwp_ref[6, rows, lanes])
            yield

    def in_turn(*gens):
        outs = []
        for g in gens:
            outs.append((yield from g))
        return outs

    nw = RWKV_WAVES_PER_STEP
    per_wave = cb // nw
    units = [wave_units(range(w * per_wave, (w + 1) * per_wave)) for w in range(nw)]
    fac, ys = [None] * nw, [None] * nw
    for k in range(nw):
        side = []
        if k >= 1:
            side.append(chain(units[k - 1], fac[k - 1]))
        if k >= 2:
            side.append(epilogue(units[k - 2], ys[k - 2]))
        fac[k], outs = _interleave(factors(units[k]), in_turn(*side))
        if k >= 1:
            ys[k - 1] = outs[0]
    drain = epilogue(units[nw - 2], ys[nw - 2]) if nw >= 2 else iter(())
    _, ys[nw - 1] = _interleave(drain, chain(units[nw - 1], fac[nw - 1]))
    _interleave(epilogue(units[nw - 1], ys[nw - 1]), iter(()))
    for gi in range(ngroups):
        st_ref[gi] = m[gi]


def _rwkv(rwp, par, batch, seq, cb):
    _, T, RW = rwp.shape
    steps = seq // (CHUNK * cb)
    return pl.pallas_call(
        functools.partial(_rwkv_kernel, cb),
        grid=(batch, steps),
        in_specs=[pl.BlockSpec((7, cb * CHUNK, RW), lambda b, c: (0, b * steps + c, 0)),
                  _const_spec(par.shape)],
        out_specs=pl.BlockSpec((cb * CHUNK, RW), lambda b, c: (b * steps + c, 0)),
        out_shape=jax.ShapeDtypeStruct((T, RW), F32),
        scratch_shapes=[pltpu.VMEM((RW // GROUP_W, HEAD_DV, GROUP_W), F32)],
        compiler_params=pltpu.CompilerParams(
            dimension_semantics=("parallel", "arbitrary"), vmem_limit_bytes=VMEM_LIMIT_BYTES),
        name="rwkv7_chunk",
    )(rwp, par)


def _glaret_kernel(cb, gr_ref, lng_ref, dec_ref, kdec_ref, qdec_ref, cdec_ref, o_ref, sg_ref, sr_ref):
    C, G, DV = CHUNK, GLA_HEADS, HEAD_DV
    WV = G * DV

    @pl.when(pl.program_id(1) == 0)
    def _():
        sg_ref[...] = jnp.zeros_like(sg_ref)
        sr_ref[...] = jnp.zeros_like(sr_ref)

    _, incl, _ = _chunk_masks(G)
    ones_bd = _block_mask(WV, WV, DV, DV).astype(F32)
    st_mask = _block_mask(WV, G * GLA_DK, DV, GLA_DK)
    lng = lng_ref[...]
    dec = dec_ref[...]
    kdec = kdec_ref[...]
    qdec = qdec_ref[...]
    cdec = cdec_ref[...]

    n = range(cb)
    rows = [slice(j * C, (j + 1) * C) for j in n]
    gq = [gr_ref[rows[j], _GQ:_GK] for j in n]
    gk = [gr_ref[rows[j], _GK:_GLA] for j in n]
    gv = [gr_ref[rows[j], _GV:_GG] for j in n]
    bc = [_cumsum_rows(gr_ref[rows[j], _GLA:_GV]) for j in n]
    be = [bc[j][C - 1:C] for j in n]
    mid = [bc[j][C // 2 - 1:C // 2] for j in n]
    ep = [jnp.exp(bc[j] - mid[j]) for j in n]
    en = [jnp.exp(mid[j] - bc[j]) for j in n]
    qp = [gq[j] * jnp.exp(bc[j]) for j in n]
    att_lo = [_dot_nt(gq[j] * ep[j], _bd_rows(gk[j] * en[j], G)) for j in n]
    att_hi = [_dot_nt(gq[j] * en[j], _bd_rows(gk[j] * ep[j], G)) for j in n]
    rq = [gr_ref[rows[j], _RQ:_RK] for j in n]
    rk = [gr_ref[rows[j], _RK:_RV] for j in n]
    rv = [gr_ref[rows[j], _RV:_RG] for j in n]
    sc = [_dot_nt(rq[j], _bd_rows(rk[j], G)) * dec for j in n]
    g_intra = [_bd_dot(jnp.where(incl, att_lo[j], att_hi[j]), gv[j], G) for j in n]
    r_intra = [_bd_dot(sc[j], rv[j], G) for j in n]
    g_kv = [jnp.where(st_mask, _dot_tn(gv[j], gk[j] * jnp.exp(be[j] - bc[j])), 0.0) for j in n]
    r_kv = [jnp.where(st_mask, _dot_tn(rv[j], rk[j] * kdec), 0.0) for j in n]

    sg = sg_ref[...]
    sr = sr_ref[...]
    g_o = []
    r_o = []
    for j in n:
        g_o.append(g_intra[j] + _dot_nt(qp[j], sg))
        sg = sg * jnp.exp(be[j]) + g_kv[j]
        r_o.append(r_intra[j] + _dot_nt(rq[j] * qdec, sr))
        sr = sr * cdec + r_kv[j]
    sg_ref[...] = sg
    sr_ref[...] = sr

    sums = _group_sums(r_o, ones_bd, coarse=[g_o[j] * g_o[j] for j in n])
    r_c = [r_o[j] - sums[j] * (1.0 / DV) for j in n]
    r_var = _group_sums([], ones_bd, coarse=[r_c[j] * r_c[j] for j in n])
    for j in n:
        o_ref[rows[j], 0:WV] = (g_o[j] * lax.rsqrt(sums[cb + j] * (1.0 / DV) + EPS) * lng
                                * gr_ref[rows[j], _GG:_RQ])
        o_ref[rows[j], WV:2 * WV] = (r_c[j] * lax.rsqrt(r_var[j] * (1.0 / DV) + EPS)
                                     * gr_ref[rows[j], _RG:_GR_W])


def _glaret(gr, lng, dec, kdec, qdec, cdec, batch, seq, cb):
    T = gr.shape[0]
    nc = seq // (CHUNK * cb)
    wv = GLA_HEADS * HEAD_DV
    consts = [lng, dec, kdec, qdec, cdec]
    return pl.pallas_call(
        functools.partial(_glaret_kernel, cb),
        grid=(batch, nc),
        in_specs=[pl.BlockSpec((cb * CHUNK, _GR_W), lambda b, c: (b * nc + c, 0))]
                 + [_const_spec(a.shape) for a in consts],
        out_specs=pl.BlockSpec((cb * CHUNK, 2 * wv), lambda b, c: (b * nc + c, 0)),
        out_shape=jax.ShapeDtypeStruct((T, 2 * wv), F32),
        scratch_shapes=[pltpu.VMEM((wv, GLA_HEADS * GLA_DK), F32),
                        pltpu.VMEM((wv, RET_HEADS * RET_DK), F32)],
        compiler_params=pltpu.CompilerParams(
            dimension_semantics=("parallel", "arbitrary"), vmem_limit_bytes=VMEM_LIMIT_BYTES),
        name="gla_retention_chunk",
    )(gr, *consts)


def _outffn_kernel(final_norm, x_ref, ya_ref, ybc_ref, mod_ref, n2g_ref, wo_ref, wg_ref, wu_ref,
                   wd_ref, nfg_ref, o_ref):
    mod = mod_ref[0]
    y = jnp.concatenate([ya_ref[...], ybc_ref[...]], axis=-1)
    x = x_ref[...] + mod[2:3] * _dot(y, wo_ref[...])
    ms = jnp.mean(x * x, axis=-1, keepdims=True)
    h = (x * lax.rsqrt(ms + EPS) * (n2g_ref[...] * (1.0 + mod[4:5])) + mod[3:4]).astype(BF16)
    gate = jnp.dot(h, wg_ref[...], preferred_element_type=F32)
    up = jnp.dot(h, wu_ref[...], preferred_element_type=F32)
    x = x + mod[5:6] * _dot(gate * _sigmoid(gate) * up, wd_ref[...])
    if final_norm:
        ms = jnp.mean(x * x, axis=-1, keepdims=True)
        x = x * lax.rsqrt(ms + EPS) * nfg_ref[...]
    o_ref[...] = x


def _outffn(x2d, ya, ybc, mod, layer, seq, tm, final_norm, n2g, wo, wg, wu, wd, nfg):
    T, D = x2d.shape
    tiles_per_seq = seq // tm
    per_layer = [n2g, wo, wg, wu, wd]
    return pl.pallas_call(
        functools.partial(_outffn_kernel, final_norm),
        grid=(T // tm,),
        in_specs=[pl.BlockSpec((tm, D), lambda i: (i, 0)),
                  pl.BlockSpec((tm, ya.shape[1]), lambda i: (i, 0)),
                  pl.BlockSpec((tm, ybc.shape[1]), lambda i: (i, 0)),
                  pl.BlockSpec((None, 1, 6, D), lambda i: (layer, i // tiles_per_seq, 0, 0))]
                 + [_layer_spec(a.shape, layer) for a in per_layer] + [_const_spec(nfg.shape)],
        out_specs=pl.BlockSpec((tm, D), lambda i: (i, 0)),
        out_shape=jax.ShapeDtypeStruct((T, D), F32),
        compiler_params=pltpu.CompilerParams(
            dimension_semantics=("parallel",), vmem_limit_bytes=VMEM_LIMIT_BYTES),
        name="outproj_swiglu",
    )(x2d, ya, ybc, mod, *per_layer, nfg)


def _rope_tables(seq):
    half = RET_DK // 2
    inv_freq = ROPE_BASE ** (-jnp.arange(half, dtype=F32) / half)
    ang = jnp.arange(seq, dtype=F32)[:, None] * inv_freq[None, :]
    cos, sin = jnp.cos(ang), jnp.sin(ang)
    cos_t = jnp.tile(jnp.concatenate([cos, cos], axis=-1), (1, RET_HEADS))
    sin_t = jnp.tile(jnp.concatenate([-sin, sin], axis=-1), (1, RET_HEADS))
    return cos_t, sin_t


def _retention_tables():
    H, C = RET_HEADS, CHUNK
    log_gamma = jnp.log1p(-(2.0 ** (-5.0 - jnp.arange(H, dtype=F32))))
    pos = jnp.arange(C, dtype=F32)
    intra = jnp.exp(log_gamma[:, None, None] * jnp.abs(pos[:, None] - pos[None, :]))
    dec = jnp.transpose(intra, (1, 0, 2)).reshape(C, H * C)
    k_dec = jnp.exp(log_gamma[None, :] * (C - 1.0 - pos)[:, None])
    q_dec = jnp.exp(log_gamma[None, :] * (pos + 1.0)[:, None])
    chunk_dec = jnp.exp(log_gamma * C)
    kdec = jnp.repeat(k_dec, RET_DK, axis=1)
    qdec = jnp.repeat(q_dec, RET_DK, axis=1)
    cdec = jnp.repeat(chunk_dec, RET_DK)[None, :]
    return dec, kdec, qdec, cdec


def kernel(x, c, ada_w, ada_b, norm1_g, norm2_g, w_in, w_out, rk_mu_rkv, rk_mu_x, rk_w0, rk_w1, rk_w2, rk_a0, rk_a1, rk_a2, rk_g1, rk_g2, rk_k_k, rk_k_a, rk_r_k, rk_ln_g, rk_ln_b, rk_mu_v, rk_v0, rk_v1, rk_v2, gla_a1, gla_a2, gla_ab, gla_ln_g, ffn_w_gate, ffn_w_up, ffn_w_down, norm_f_g):
    B, S, D = x.shape
    L = ada_w.shape[0]
    T = B * S
    assert S % PREP_TOKENS_PER_STEP == 0 and S % FFN_TOKENS_PER_STEP == 0
    assert S % (CHUNK * RWKV_CHUNKS_PER_STEP) == 0 and S % (CHUNK * GLARET_CHUNKS_PER_STEP) == 0

    mod = _adaln(c, ada_w, ada_b).reshape(L, B, 6, D)
    cos_t, sin_t = _rope_tables(S)
    dec, kdec, qdec, cdec = _retention_tables()
    obd = _block_mask(GROUP_W, GROUP_W, HEAD_DV, HEAD_DV).astype(BF16)
    RW = RWKV_WIDTH

    def zeros(*shape):
        return jnp.zeros(shape, F32)

    mu_v = jnp.concatenate([zeros(1, D), rk_mu_v], axis=0)
    v0 = jnp.concatenate([zeros(1, RW), rk_v0], axis=0)
    v1 = jnp.concatenate([zeros(1, D, rk_v1.shape[2]), rk_v1], axis=0)
    v2 = jnp.concatenate([zeros(1, rk_v2.shape[1], RW), rk_v2], axis=0)

    def on_h(mu, w):
        return (1.0 - mu)[:, :, None] * w

    def on_prev(mu, w):
        return mu[:, :, None] * w

    mu = rk_mu_x
    misc = jnp.concatenate([gla_a1, zeros(L, D, 16), on_h(mu_v, v1), on_prev(mu_v, v1), zeros(L, D, 32)],
                           axis=2)
    lora_h = jnp.concatenate([on_h(mu[:, 0], rk_w1), on_h(mu[:, 1], rk_a1), on_h(mu[:, 2], rk_g1)], axis=2)
    lora_s = jnp.concatenate([on_prev(mu[:, 0], rk_w1), on_prev(mu[:, 1], rk_a1), on_prev(mu[:, 2], rk_g1)],
                             axis=2)
    win = jnp.concatenate([t.astype(BF16) for t in (misc, lora_h, lora_s, w_in)], axis=2)
    assert win.shape[2] == _WIN_W and misc.shape[2] == _W_LORA_H - _W_MISC
    w2 = jnp.concatenate([
        jnp.concatenate([rk_w2, zeros(L, 64, 2 * RW)], axis=2),
        jnp.concatenate([zeros(L, 64, RW), rk_a2, zeros(L, 64, RW)], axis=2),
        jnp.concatenate([zeros(L, 128, 2 * RW), rk_g2], axis=2)], axis=1).astype(BF16)
    ga2 = jnp.concatenate([gla_a2, zeros(L, 128 - gla_a2.shape[1], gla_a2.shape[2])], axis=1).astype(BF16)
    v2p = jnp.concatenate([zeros(L, 32, RW), v2, v2, zeros(L, 32, RW)], axis=1).astype(BF16)
    vec = jnp.stack([rk_mu_rkv[:, 0], rk_mu_rkv[:, 1], rk_mu_rkv[:, 2], rk_w0, rk_a0, v0, rk_k_k, rk_k_a],
                    axis=1)
    par = jnp.stack([rk_r_k.reshape(L, RW), rk_ln_g, rk_ln_b], axis=1)
    lng = jnp.tile(gla_ln_g, (1, GLA_HEADS))[:, None, :]
    wo, wg, wu, wd = (w.astype(BF16) for w in (w_out, ffn_w_gate, ffn_w_up, ffn_w_down))

    x2d = x.reshape(T, D)
    rwp_first = None
    for l in range(L):
        has_vres = l > 0
        extra = dict(v2=v2p, rwp_first=rwp_first) if has_vres else {}
        rwp, gr = _prep(x2d, mod, l, S, PREP_TOKENS_PER_STEP, has_vres, norm1_g[:, None, :], win, w2, ga2,
                        gla_ab[:, None, :], vec, obd, cos_t, sin_t, **extra)
        if l == 0:
            rwp_first = rwp
        ya = _rwkv(rwp, par[l], B, S, RWKV_CHUNKS_PER_STEP)
        ybc = _glaret(gr, lng[l], dec, kdec, qdec, cdec, B, S, GLARET_CHUNKS_PER_STEP)
        x2d = _outffn(x2d, ya, ybc, mod, l, S, FFN_TOKENS_PER_STEP, l == L - 1, norm2_g[:, None, :],
                      wo, wg, wu, wd, norm_f_g[None])
    return x2d.reshape(B, S, D)
```

```python
import functools

import numpy as np
import jax
import jax.numpy as jnp
from jax import lax
from jax.experimental import pallas as pl
from jax.experimental.pallas import tpu as pltpu

F32 = jnp.float32
BF16 = jnp.bfloat16

CHUNK = 64
EPS = 1e-6
HEAD_DV = 64
RWKV_HEADS = 8
RWKV_WIDTH = RWKV_HEADS * HEAD_DV
RWKV_GN_EPS = 64e-5
GLA_HEADS = 4
GLA_DK = 32
GLA_GATE_TAU = 16.0
RET_HEADS = 4
RET_DK = 32
ROPE_BASE = 10000.0
HEADS_PER_GROUP = 4
GROUP_W = HEADS_PER_GROUP * HEAD_DV
VMEM_LIMIT_BYTES = 56 * 1024 * 1024
RWKV_CHUNKS_PER_STEP = 16
RWKV_WAVES_PER_STEP = 4
GLARET_CHUNKS_PER_STEP = 16
PREP_TOKENS_PER_STEP = 512
FFN_TOKENS_PER_STEP = 512

_GQ, _GK, _GLA, _GV, _GG, _RQ, _RK, _RV, _RG, _GR_W = 0, 128, 256, 384, 640, 896, 1024, 1152, 1408, 1664
_W_MISC, _W_LORA_H, _W_LORA_S, _W_IN, _WIN_W = 0, 128, 384, 640, 3712


def _dot(a, b):
    return jnp.dot(a.astype(BF16), b.astype(BF16), preferred_element_type=F32)


def _dot_nt(a, b):
    return lax.dot_general(a.astype(BF16), b.astype(BF16), (((1,), (1,)), ((), ())),
                           preferred_element_type=F32)


def _dot_tn(a, b):
    return lax.dot_general(a.astype(BF16), b.astype(BF16), (((0,), (0,)), ((), ())),
                           preferred_element_type=F32)


def _group_sums(xs, ones_bd, coarse=()):
    hi = [x.astype(BF16) for x in xs]
    lo = [(x - h.astype(F32)).astype(BF16) for x, h in zip(xs, hi)]
    terms = hi + [x.astype(BF16) for x in coarse] + lo
    s = jnp.dot(jnp.concatenate(terms, axis=0), ones_bd.astype(BF16), preferred_element_type=F32)
    offs = np.cumsum([0] + [t.shape[0] for t in terms])
    nx, nc = len(xs), len(coarse)
    out = [s[offs[i]:offs[i + 1]] + s[offs[nx + nc + i]:offs[nx + nc + i + 1]] for i in range(nx)]
    return out + [s[offs[nx + i]:offs[nx + i + 1]] for i in range(nc)]


def _cumsum_rows(x):
    n = x.shape[0]
    row = _iota(x.shape, 0)
    s = 1
    while s < n:
        x = x + jnp.where(row >= s, pltpu.roll(x, s, 0), 0.0)
        s *= 2
    return x


def _sigmoid(x):
    return 1.0 / (1.0 + jnp.exp(-x))


def _softplus(x):
    return jnp.maximum(x, 0.0) + jnp.log(1.0 + jnp.exp(-jnp.abs(x)))


def _iota(shape, axis):
    return lax.broadcasted_iota(jnp.int32, shape, axis)


def _bd_rows(x, groups):
    c, w = x.shape
    n = w // groups
    t = jnp.concatenate([x] * groups, axis=0)
    keep = (_iota(t.shape, 0) // c) == (_iota(t.shape, 1) // n)
    return jnp.where(keep, t, 0.0)


def _bd_dot(x, y, groups, nt=False):
    e = _bd_rows(y, groups)
    return _dot_nt(x, e) if nt else _dot(x, e)


def _block_mask(rows, cols, rblk, cblk):
    return (_iota((rows, cols), 0) // rblk) == (_iota((rows, cols), 1) // cblk)


def _const_spec(shape):
    nd = len(shape)
    return pl.BlockSpec(shape, lambda *_: (0,) * nd, pipeline_mode=pl.Buffered(1))


def _layer_spec(shape, layer):
    nd = len(shape)
    return pl.BlockSpec((None,) + tuple(shape[1:]), lambda *_: (layer,) + (0,) * (nd - 1),
                        pipeline_mode=pl.Buffered(1))


def _adaln_kernel(c_ref, w_ref, b_ref, o_ref):
    c = c_ref[...]
    cond = c * _sigmoid(c)
    o_ref[0] = _dot(cond, w_ref[0]) + b_ref[0]


def _adaln(c, ada_w, ada_b):
    L, D, D6 = ada_w.shape
    B = c.shape[0]
    tn = 1536
    return pl.pallas_call(
        _adaln_kernel,
        grid=(L, D6 // tn),
        in_specs=[pl.BlockSpec((B, D), lambda l, j: (0, 0)),
                  pl.BlockSpec((1, D, tn), lambda l, j: (l, 0, j)),
                  pl.BlockSpec((1, 1, tn), lambda l, j: (l, 0, j))],
        out_specs=pl.BlockSpec((1, B, tn), lambda l, j: (l, 0, j)),
        out_shape=jax.ShapeDtypeStruct((L, B, D6), F32),
        compiler_params=pltpu.CompilerParams(
            dimension_semantics=("arbitrary", "arbitrary"), vmem_limit_bytes=VMEM_LIMIT_BYTES),
        name="adaln_mod",
    )(c, ada_w, ada_b.reshape(L, 1, D6))


def _prep_kernel(tiles_per_seq, has_vres, *refs):
    if has_vres:
        (x_ref, xh_ref, mod_ref, n1g_ref, win_ref, w2_ref, ga2_ref, gab_ref, vec_ref, obd_ref,
         cos_ref, sin_ref, v2_ref, vf_ref, rwp_ref, gr_ref) = refs
    else:
        (x_ref, xh_ref, mod_ref, n1g_ref, win_ref, w2_ref, ga2_ref, gab_ref, vec_ref, obd_ref,
         cos_ref, sin_ref, rwp_ref, gr_ref) = refs
    tm = x_ref.shape[0]
    rw = RWKV_WIDTH
    first = (pl.program_id(0) % tiles_per_seq) == 0

    xe = jnp.concatenate([xh_ref[...], x_ref[...]], axis=0)
    mod = mod_ref[0]
    ms = jnp.mean(xe * xe, axis=-1, keepdims=True)
    he = xe * lax.rsqrt(ms + EPS) * (n1g_ref[...] * (1.0 + mod[1:2])) + mod[0:1]
    he = jnp.concatenate([jnp.where(first, 0.0, he[:8]), he[8:]], axis=0)

    heb = he.astype(BF16)

    def project(lo, hi):
        return jnp.dot(heb, win_ref[:, lo:hi], preferred_element_type=F32)

    def prev_rows(t):
        return pltpu.roll(t, 1, 0)[8:]

    p_lora = project(_W_MISC, _W_IN)
    p_rkv = project(_W_IN, _W_IN + 3 * rw)
    vec = vec_ref[...]

    pre = p_lora[8:, _W_LORA_H:_W_LORA_S] + prev_rows(p_lora[:, _W_LORA_S:_W_IN])
    lane = _iota(pre.shape, 1)
    act = jnp.where(lane < 64, jnp.tanh(pre), jnp.where(lane < 128, pre, _sigmoid(pre)))
    second = _dot(act, w2_ref[...])
    p_gla = project(_W_IN + 3 * rw, _W_IN + 3 * rw + 768)
    lw = -jnp.exp(-_softplus(-(vec[3:4] + second[:, 0:rw])) - 0.5)
    a = _sigmoid(vec[4:5] + second[:, rw:2 * rw])
    g = second[:, 2 * rw:3 * rw]

    p = p_rkv[8:]
    ps = prev_rows(p_rkv)
    r = p[:, 0:rw]
    r = r + (ps[:, 0:rw] - r) * vec[0:1]
    k = p[:, rw:2 * rw]
    k = k + (ps[:, rw:2 * rw] - k) * vec[1:2]
    v = p[:, 2 * rw:3 * rw]
    v = v + (ps[:, 2 * rw:3 * rw] - v) * vec[2:3]

    kk = k * vec[6:7]
    kk2 = kk * kk
    ss = jnp.concatenate(_group_sums([], obd_ref[...], coarse=[kk2[:, :GROUP_W], kk2[:, GROUP_W:]]), axis=1)
    kk = kk / jnp.maximum(jnp.sqrt(ss), 1e-12)
    k = k * (1.0 + (a - 1.0) * vec[7:8])
    misc = p_lora[8:, _W_MISC:_W_LORA_H]
    if has_vres:
        mv = jnp.where(_iota(misc.shape, 1) < 64, misc, prev_rows(p_lora[:, _W_MISC:_W_LORA_H]))
        v = v + (vf_ref[0] - v) * _sigmoid(vec[5:6] + _dot(mv, v2_ref[...]))
    p_ret = project(_W_IN + 3 * rw + 768, _WIN_W)

    rwp_ref[0] = r
    rwp_ref[1] = lw
    rwp_ref[2] = k
    rwp_ref[3] = v
    rwp_ref[4] = kk
    rwp_ref[5] = kk * a
    rwp_ref[6] = g

    p = p_gla[8:]
    gr_ref[:, _GQ:_GK] = p[:, 0:128] * (GLA_DK ** -0.5)
    gr_ref[:, _GK:_GLA] = p[:, 128:256]
    la_pre = _dot(misc, ga2_ref[...]) + gab_ref[...]
    gr_ref[:, _GLA:_GV] = -_softplus(-la_pre) * (1.0 / GLA_GATE_TAU)
    gr_ref[:, _GV:_GG] = p[:, 256:512]
    gate = p[:, 512:768]
    gr_ref[:, _GG:_RQ] = gate * _sigmoid(gate)

    p = p_ret[8:]
    o = 0
    cos = cos_ref[...]
    sin = sin_ref[...]
    lo_half = (_iota((tm, 128), 1) % RET_DK) < (RET_DK // 2)

    def rope(t):
        swapped = jnp.where(lo_half, pltpu.roll(t, 128 - RET_DK // 2, 1), pltpu.roll(t, RET_DK // 2, 1))
        return t * cos + swapped * sin

    gr_ref[:, _RQ:_RK] = rope(p[:, o:o + 128]) * (RET_DK ** -0.5)
    gr_ref[:, _RK:_RV] = rope(p[:, o + 128:o + 256])
    gr_ref[:, _RV:_RG] = p[:, o + 256:o + 512]
    gate = p[:, o + 512:o + 768]
    gr_ref[:, _RG:_GR_W] = gate * _sigmoid(gate)


def _prep(x2d, mod, layer, seq, tm, has_vres, n1g, win, w2, ga2, gab, vec, obd, cos_t, sin_t,
          v2=None, rwp_first=None):
    T, D = x2d.shape
    tiles_per_seq = seq // tm
    n_tiles = T // tm
    per_layer = [n1g, win, w2, ga2, gab, vec]
    in_specs = [pl.BlockSpec((tm, D), lambda i: (i, 0)),
                pl.BlockSpec((8, D), lambda i: (jnp.maximum(i * (tm // 8) - 1, 0), 0)),
                pl.BlockSpec((None, 1, 6, D), lambda i: (layer, i // tiles_per_seq, 0, 0))]
    in_specs += [_layer_spec(a.shape, layer) for a in per_layer]
    in_specs += [_const_spec(obd.shape),
                 pl.BlockSpec((tm, 128), lambda i: (i % tiles_per_seq, 0)),
                 pl.BlockSpec((tm, 128), lambda i: (i % tiles_per_seq, 0))]
    args = [x2d, x2d, mod] + per_layer + [obd, cos_t, sin_t]
    if has_vres:
        in_specs += [_layer_spec(v2.shape, layer),
                     pl.BlockSpec((1, tm, RWKV_WIDTH), lambda i: (3, i, 0))]
        args += [v2, rwp_first]
    return pl.pallas_call(
        functools.partial(_prep_kernel, tiles_per_seq, has_vres),
        grid=(n_tiles,),
        in_specs=in_specs,
        out_specs=[pl.BlockSpec((7, tm, RWKV_WIDTH), lambda i: (0, i, 0)),
                   pl.BlockSpec((tm, _GR_W), lambda i: (i, 0))],
        out_shape=[jax.ShapeDtypeStruct((7, T, RWKV_WIDTH), F32),
                   jax.ShapeDtypeStruct((T, _GR_W), F32)],
        compiler_params=pltpu.CompilerParams(
            dimension_semantics=("parallel",), vmem_limit_bytes=VMEM_LIMIT_BYTES),
        name="proj_prep",
    )(*args)


def _chunk_masks(groups):
    C = CHUNK
    row = _iota((C, groups * C), 0)
    col = _iota((C, groups * C), 1) % C
    return row > col, row >= col, (row == col).astype(F32)


def _rwkv_chunk_scaled(r, lw, k, v, kk, kb):
    C = CHUNK
    cl = _cumsum_rows(lw)
    cle = cl[C - 1:C]
    e_neg = jnp.exp(-cl)
    e_end = jnp.exp(cle - cl)
    at = -kk * jnp.exp(cl - lw)
    rt = r * jnp.exp(cl)
    return dict(at=at, rt=rt, ar=jnp.concatenate([at, rt], axis=0), bt=kb * e_neg, kt=k * e_neg,
                bk_end=jnp.concatenate([kb * e_end, k * e_end], axis=0), g_end=jnp.exp(cle), v=v)


def _rwkv_chunk_factors(units, masks):
    C, G = CHUNK, HEADS_PER_GROUP
    strict, incl, ident = masks
    n = range(len(units))
    at, rt, ar, v = ([u[name] for u in units] for name in ("at", "rt", "ar", "v"))
    sb = [_bd_dot(ar[i], units[i]["bt"], G, nt=True) for i in n]
    sk = [_bd_dot(ar[i], units[i]["kt"], G, nt=True) for i in n]
    yield
    a_rb = [jnp.where(incl, sb[i][C:], 0.0) for i in n]
    a_ak = [jnp.where(strict, sk[i][:C], 0.0) for i in n]
    a_rk = [jnp.where(incl, sk[i][C:], 0.0) for i in n]

    a_ab = [jnp.where(strict, sb[i][:C], 0.0) for i in n]
    tm = [ident + a_ab[i] for i in n]
    xp = [_bd_dot(a_ab[i], a_ab[i], G) for i in n]
    yield
    for _ in range(4):
        rr = [_bd_dot(jnp.concatenate([tm[i], xp[i]], axis=0), xp[i], G) for i in n]
        tm = [tm[i] + rr[i][:C] for i in n]
        xp = [rr[i][C:] for i in n]
        yield
    tm = [tm[i] + _bd_dot(tm[i], xp[i], G) for i in n]
    yield

    vv = [_bd_dot(jnp.concatenate([a_ak[i], a_rk[i]], axis=0), v[i], G) for i in n]
    yield
    wt = [_bd_dot(tm[i], at[i], G) for i in n]
    yield
    ut = [_bd_dot(tm[i], vv[i][:C], G) for i in n]
    yield
    qh = [rt[i] + _bd_dot(a_rb[i], wt[i], G) for i in n]
    yield
    yh = [_bd_dot(a_rb[i], ut[i], G) + vv[i][C:] for i in n]
    yield
    zero = jnp.zeros((C, 128), F32)
    left = _iota((C, 128), 1) < HEAD_DV
    pm, zm = [[] for _ in n], [[] for _ in n]
    for s in range(GROUP_W // 128):
        sl = slice(128 * s, 128 * (s + 1))
        pz = [_dot_tn(units[i]["bk_end"][:, sl],
                      jnp.concatenate([jnp.concatenate([wt[i][:, sl], ut[i][:, sl]], axis=1),
                                       jnp.concatenate([zero, v[i][:, sl]], axis=1)], axis=0)) for i in n]
        for i in n:
            pm[i].append(jnp.where(left, pz[i][:C, :128], pz[i][C:, :128]))
            zm[i].append(jnp.where(left, pz[i][:C, 128:], pz[i][C:, 128:]))
        yield
    pm = [jnp.concatenate(pm[i], axis=1) + jnp.where(ident > 0, units[i]["g_end"], 0.0) for i in n]
    zm = [jnp.concatenate(zm[i], axis=1) for i in n]
    return qh, yh, pm, zm


def _interleave(main, side):
    side_done, side_out = False, None
    while True:
        try:
            next(main)
        except StopIteration as stop:
            main_out = stop.value
            break
        if not side_done:
            try:
                next(side)
            except StopIteration as stop:
                side_done, side_out = True, stop.value
    while not side_done:
        try:
            next(side)
        except StopIteration as stop:
            side_done, side_out = True, stop.value
    return main_out, side_out


def _rwkv_kernel(cb, rwp_ref, par_ref, o_ref, st_ref):
    C, W, N = CHUNK, GROUP_W, HEAD_DV
    ngroups = o_ref.shape[1] // W

    @pl.when(pl.program_id(1) == 0)
    def _():
        st_ref[...] = jnp.zeros_like(st_ref)

    masks = _chunk_masks(HEADS_PER_GROUP)
    ones_bd = _block_mask(W, W, N, N).astype(F32)
    m = [st_ref[gi] for gi in range(ngroups)]

    def wave_units(chunks):
        return [(slice(j * C, (j + 1) * C), slice(gi * W, (gi + 1) * W), gi)
                for j in chunks for gi in range(ngroups)]

    def factors(units):
        scaled = [_rwkv_chunk_scaled(*[rwp_ref[i, rows, lanes] for i in range(6)])
                  for rows, lanes, _ in units]
        return _rwkv_chunk_factors(scaled, masks)

    def chain(units, fac):
        qh, yh, pm, zm = fac
        y = []
        for u, (_, _, gi) in enumerate(units):
            ym = _dot(jnp.concatenate([qh[u], pm[u]], axis=0), _bd_rows(m[gi], HEADS_PER_GROUP))
            y.append(ym[:C] + yh[u])
            m[gi] = ym[C:] + zm[u]
            yield
        return y

    def epilogue(units, y):
        n = range(len(units))
        par = [par_ref[:, lanes] for _, lanes, _ in units]
        rkr = [rwp_ref[0, rows, lanes] * rwp_ref[2, rows, lanes] * par[u][0:1]
               for u, (rows, lanes, _) in enumerate(units)]
        sums = _group_sums(y + rkr, ones_bd)
        yield
        yc = [y[u] - sums[u] * (1.0 / N) for u in n]
        var = _group_sums([], ones_bd, coarse=[yc[u] * yc[u] for u in n])
        yield
        for u, (rows, lanes, _) in enumerate(units):
            yn = yc[u] * lax.rsqrt(var[u] * (1.0 / N) + RWKV_GN_EPS) * par[u][1:2] + par[u][2:3]
            o_ref[rows, lanes] = ((yn + sums[len(units) + u] * rwp_ref[3, rows, lanes])
                                  * rwp_ref[6, rows, lanes])
            yield

    def in_turn(*gens):
        outs = []
        for g in gens:
            outs.append((yield from g))
        return outs

    nw = RWKV_WAVES_PER_STEP
    per_wave = cb // nw
    units = [wave_units(range(w * per_wave, (w + 1) * per_wave)) for w in range(nw)]
    fac, ys = [None] * nw, [None] * nw
    for k in range(nw):
        side = []
        if k >= 1:
            side.append(chain(units[k - 1], fac[k - 1]))
        if k >= 2:
            side.append(epilogue(units[k - 2], ys[k - 2]))
        fac[k], outs = _interleave(factors(units[k]), in_turn(*side))
        if k >= 1:
            ys[k - 1] = outs[0]
    drain = epilogue(units[nw - 2], ys[nw - 2]) if nw >= 2 else iter(())
    _, ys[nw - 1] = _interleave(drain, chain(units[nw - 1], fac[nw - 1]))
    _interleave(epilogue(units[nw - 1], ys[nw - 1]), iter(()))
    for gi in range(ngroups):
        st_ref[gi] = m[gi]


def _rwkv(rwp, par, batch, seq, cb):
    _, T, RW = rwp.shape
    steps = seq // (CHUNK * cb)
    return pl.pallas_call(
        functools.partial(_rwkv_kernel, cb),
        grid=(batch, steps),
        in_specs=[pl.BlockSpec((7, cb * CHUNK, RW), lambda b, c: (0, b * steps + c, 0)),
                  _const_spec(par.shape)],
        out_specs=pl.BlockSpec((cb * CHUNK, RW), lambda b, c: (b * steps + c, 0)),
        out_shape=jax.ShapeDtypeStruct((T, RW), F32),
        scratch_shapes=[pltpu.VMEM((RW // GROUP_W, HEAD_DV, GROUP_W), F32)],
        compiler_params=pltpu.CompilerParams(
            dimension_semantics=("parallel", "arbitrary"), vmem_limit_bytes=VMEM_LIMIT_BYTES),
        name="rwkv7_chunk",
    )(rwp, par)


def _glaret_kernel(cb, gr_ref, lng_ref, dec_ref, kdec_ref, qdec_ref, cdec_ref, o_ref, sg_ref, sr_ref):
    C, G, DV = CHUNK, GLA_HEADS, HEAD_DV
    WV = G * DV

    @pl.when(pl.program_id(1) == 0)
    def _():
        sg_ref[...] = jnp.zeros_like(sg_ref)
        sr_ref[...] = jnp.zeros_like(sr_ref)

    _, incl, _ = _chunk_masks(G)
    ones_bd = _block_mask(WV, WV, DV, DV).astype(F32)
    st_mask = _block_mask(WV, G * GLA_DK, DV, GLA_DK)
    lng = lng_ref[...]
    dec = dec_ref[...]
    kdec = kdec_ref[...]
    qdec = qdec_ref[...]
    cdec = cdec_ref[...]

    n = range(cb)
    rows = [slice(j * C, (j + 1) * C) for j in n]
    gq = [gr_ref[rows[j], _GQ:_GK] for j in n]
    gk = [gr_ref[rows[j], _GK:_GLA] for j in n]
    gv = [gr_ref[rows[j], _GV:_GG] for j in n]
    bc = [_cumsum_rows(gr_ref[rows[j], _GLA:_GV]) for j in n]
    be = [bc[j][C - 1:C] for j in n]
    mid = [bc[j][C // 2 - 1:C // 2] for j in n]
    ep = [jnp.exp(bc[j] - mid[j]) for j in n]
    en = [jnp.exp(mid[j] - bc[j]) for j in n]
    qp = [gq[j] * jnp.exp(bc[j]) for j in n]
    att_lo = [_dot_nt(gq[j] * ep[j], _bd_rows(gk[j] * en[j], G)) for j in n]
    att_hi = [_dot_nt(gq[j] * en[j], _bd_rows(gk[j] * ep[j], G)) for j in n]
    rq = [gr_ref[rows[j], _RQ:_RK] for j in n]
    rk = [gr_ref[rows[j], _RK:_RV] for j in n]
    rv = [gr_ref[rows[j], _RV:_RG] for j in n]
    sc = [_dot_nt(rq[j], _bd_rows(rk[j], G)) * dec for j in n]
    g_intra = [_bd_dot(jnp.where(incl, att_lo[j], att_hi[j]), gv[j], G) for j in n]
    r_intra = [_bd_dot(sc[j], rv[j], G) for j in n]
    g_kv = [jnp.where(st_mask, _dot_tn(gv[j], gk[j] * jnp.exp(be[j] - bc[j])), 0.0) for j in n]
    r_kv = [jnp.where(st_mask, _dot_tn(rv[j], rk[j] * kdec), 0.0) for j in n]

    sg = sg_ref[...]
    sr = sr_ref[...]
    g_o = []
    r_o = []
    for j in n:
        g_o.append(g_intra[j] + _dot_nt(qp[j], sg))
        sg = sg * jnp.exp(be[j]) + g_kv[j]
        r_o.append(r_intra[j] + _dot_nt(rq[j] * qdec, sr))
        sr = sr * cdec + r_kv[j]
    sg_ref[...] = sg
    sr_ref[...] = sr

    sums = _group_sums(r_o, ones_bd, coarse=[g_o[j] * g_o[j] for j in n])
    r_c = [r_o[j] - sums[j] * (1.0 / DV) for j in n]
    r_var = _group_sums([], ones_bd, coarse=[r_c[j] * r_c[j] for j in n])
    for j in n:
        o_ref[rows[j], 0:WV] = (g_o[j] * lax.rsqrt(sums[cb + j] * (1.0 / DV) + EPS) * lng
                                * gr_ref[rows[j], _GG:_RQ])
        o_ref[rows[j], WV:2 * WV] = (r_c[j] * lax.rsqrt(r_var[j] * (1.0 / DV) + EPS)
                                     * gr_ref[rows[j], _RG:_GR_W])


def _glaret(gr, lng, dec, kdec, qdec, cdec, batch, seq, cb):
    T = gr.shape[0]
    nc = seq // (CHUNK * cb)
    wv = GLA_HEADS * HEAD_DV
    consts = [lng, dec, kdec, qdec, cdec]
    return pl.pallas_call(
        functools.partial(_glaret_kernel, cb),
        grid=(batch, nc),
        in_specs=[pl.BlockSpec((cb * CHUNK, _GR_W), lambda b, c: (b * nc + c, 0))]
                 + [_const_spec(a.shape) for a in consts],
        out_specs=pl.BlockSpec((cb * CHUNK, 2 * wv), lambda b, c: (b * nc + c, 0)),
        out_shape=jax.ShapeDtypeStruct((T, 2 * wv), F32),
        scratch_shapes=[pltpu.VMEM((wv, GLA_HEADS * GLA_DK), F32),
                        pltpu.VMEM((wv, RET_HEADS * RET_DK), F32)],
        compiler_params=pltpu.CompilerParams(
            dimension_semantics=("parallel", "arbitrary"), vmem_limit_bytes=VMEM_LIMIT_BYTES),
        name="gla_retention_chunk",
    )(gr, *consts)


def _outffn_kernel(final_norm, x_ref, ya_ref, ybc_ref, mod_ref, n2g_ref, wo_ref, wg_ref, wu_ref,
                   wd_ref, nfg_ref, o_ref):
    mod = mod_ref[0]
    y = jnp.concatenate([ya_ref[...], ybc_ref[...]], axis=-1)
    x = x_ref[...] + mod[2:3] * _dot(y, wo_ref[...])
    ms = jnp.mean(x * x, axis=-1, keepdims=True)
    h = (x * lax.rsqrt(ms + EPS) * (n2g_ref[...] * (1.0 + mod[4:5])) + mod[3:4]).astype(BF16)
    gate = jnp.dot(h, wg_ref[...], preferred_element_type=F32)
    up = jnp.dot(h, wu_ref[...], preferred_element_type=F32)
    x = x + mod[5:6] * _dot(gate * _sigmoid(gate) * up, wd_ref[...])
    if final_norm:
        ms = jnp.mean(x * x, axis=-1, keepdims=True)
        x = x * lax.rsqrt(ms + EPS) * nfg_ref[...]
    o_ref[...] = x


def _outffn(x2d, ya, ybc, mod, layer, seq, tm, final_norm, n2g, wo, wg, wu, wd, nfg):
    T, D = x2d.shape
    tiles_per_seq = seq // tm
    per_layer = [n2g, wo, wg, wu, wd]
    return pl.pallas_call(
        functools.partial(_outffn_kernel, final_norm),
        grid=(T // tm,),
        in_specs=[pl.BlockSpec((tm, D), lambda i: (i, 0)),
                  pl.BlockSpec((tm, ya.shape[1]), lambda i: (i, 0)),
                  pl.BlockSpec((tm, ybc.shape[1]), lambda i: (i, 0)),
                  pl.BlockSpec((None, 1, 6, D), lambda i: (layer, i // tiles_per_seq, 0, 0))]
                 + [_layer_spec(a.shape, layer) for a in per_layer] + [_const_spec(nfg.shape)],
        out_specs=pl.BlockSpec((tm, D), lambda i: (i, 0)),
        out_shape=jax.ShapeDtypeStruct((T, D), F32),
        compiler_params=pltpu.CompilerParams(
            dimension_semantics=("parallel",), vmem_limit_bytes=VMEM_LIMIT_BYTES),
        name="outproj_swiglu",
    )(x2d, ya, ybc, mod, *per_layer, nfg)


def _rope_tables(seq):
    half = RET_DK // 2
    inv_freq = ROPE_BASE ** (-jnp.arange(half, dtype=F32) / half)
    ang = jnp.arange(seq, dtype=F32)[:, None] * inv_freq[None, :]
    cos, sin = jnp.cos(ang), jnp.sin(ang)
    cos_t = jnp.tile(jnp.concatenate([cos, cos], axis=-1), (1, RET_HEADS))
    sin_t = jnp.tile(jnp.concatenate([-sin, sin], axis=-1), (1, RET_HEADS))
    return cos_t, sin_t


def _retention_tables():
    H, C = RET_HEADS, CHUNK
    log_gamma = jnp.log1p(-(2.0 ** (-5.0 - jnp.arange(H, dtype=F32))))
    pos = jnp.arange(C, dtype=F32)
    intra = jnp.exp(log_gamma[:, None, None] * jnp.abs(pos[:, None] - pos[None, :]))
    dec = jnp.transpose(intra, (1, 0, 2)).reshape(C, H * C)
    k_dec = jnp.exp(log_gamma[None, :] * (C - 1.0 - pos)[:, None])
    q_dec = jnp.exp(log_gamma[None, :] * (pos + 1.0)[:, None])
    chunk_dec = jnp.exp(log_gamma * C)
    kdec = jnp.repeat(k_dec, RET_DK, axis=1)
    qdec = jnp.repeat(q_dec, RET_DK, axis=1)
    cdec = jnp.repeat(chunk_dec, RET_DK)[None, :]
    return dec, kdec, qdec, cdec


def kernel(x, c, ada_w, ada_b, norm1_g, norm2_g, w_in, w_out, rk_mu_rkv, rk_mu_x, rk_w0, rk_w1, rk_w2, rk_a0, rk_a1, rk_a2, rk_g1, rk_g2, rk_k_k, rk_k_a, rk_r_k, rk_ln_g, rk_ln_b, rk_mu_v, rk_v0, rk_v1, rk_v2, gla_a1, gla_a2, gla_ab, gla_ln_g, ffn_w_gate, ffn_w_up, ffn_w_down, norm_f_g):
    B, S, D = x.shape
    L = ada_w.shape[0]
    T = B * S
    assert S % PREP_TOKENS_PER_STEP == 0 and S % FFN_TOKENS_PER_STEP == 0
    assert S % (CHUNK * RWKV_CHUNKS_PER_STEP) == 0 and S % (CHUNK * GLARET_CHUNKS_PER_STEP) == 0

    mod = _adaln(c, ada_w, ada_b).reshape(L, B, 6, D)
    cos_t, sin_t = _rope_tables(S)
    dec, kdec, qdec, cdec = _retention_tables()
    obd = _block_mask(GROUP_W, GROUP_W, HEAD_DV, HEAD_DV).astype(BF16)
    RW = RWKV_WIDTH

    def zeros(*shape):
        return jnp.zeros(shape, F32)

    mu_v = jnp.concatenate([zeros(1, D), rk_mu_v], axis=0)
    v0 = jnp.concatenate([zeros(1, RW), rk_v0], axis=0)
    v1 = jnp.concatenate([zeros(1, D, rk_v1.shape[2]), rk_v1], axis=0)
    v2 = jnp.concatenate([zeros(1, rk_v2.shape[1], RW), rk_v2], axis=0)

    def on_h(mu, w):
        return (1.0 - mu)[:, :, None] * w

    def on_prev(mu, w):
        return mu[:, :, None] * w

    mu = rk_mu_x
    misc = jnp.concatenate([gla_a1, zeros(L, D, 16), on_h(mu_v, v1), on_prev(mu_v, v1), zeros(L, D, 32)],
                           axis=2)
    lora_h = jnp.concatenate([on_h(mu[:, 0], rk_w1), on_h(mu[:, 1], rk_a1), on_h(mu[:, 2], rk_g1)], axis=2)
    lora_s = jnp.concatenate([on_prev(mu[:, 0], rk_w1), on_prev(mu[:, 1], rk_a1), on_prev(mu[:, 2], rk_g1)],
                             axis=2)
    win = jnp.concatenate([t.astype(BF16) for t in (misc, lora_h, lora_s, w_in)], axis=2)
    assert win.shape[2] == _WIN_W and misc.shape[2] == _W_LORA_H - _W_MISC
    w2 = jnp.concatenate([
        jnp.concatenate([rk_w2, zeros(L, 64, 2 * RW)], axis=2),
        jnp.concatenate([zeros(L, 64, RW), rk_a2, zeros(L, 64, RW)], axis=2),
        jnp.concatenate([zeros(L, 128, 2 * RW), rk_g2], axis=2)], axis=1).astype(BF16)
    ga2 = jnp.concatenate([gla_a2, zeros(L, 128 - gla_a2.shape[1], gla_a2.shape[2])], axis=1).astype(BF16)
    v2p = jnp.concatenate([zeros(L, 32, RW), v2, v2, zeros(L, 32, RW)], axis=1).astype(BF16)
    vec = jnp.stack([rk_mu_rkv[:, 0], rk_mu_rkv[:, 1], rk_mu_rkv[:, 2], rk_w0, rk_a0, v0, rk_k_k, rk_k_a],
                    axis=1)
    par = jnp.stack([rk_r_k.reshape(L, RW), rk_ln_g, rk_ln_b], axis=1)
    lng = jnp.tile(gla_ln_g, (1, GLA_HEADS))[:, None, :]
    wo, wg, wu, wd = (w.astype(BF16) for w in (w_out, ffn_w_gate, ffn_w_up, ffn_w_down))

    x2d = x.reshape(T, D)
    rwp_first = None
    for l in range(L):
        has_vres = l > 0
        extra = dict(v2=v2p, rwp_first=rwp_first) if has_vres else {}
        rwp, gr = _prep(x2d, mod, l, S, PREP_TOKENS_PER_STEP, has_vres, norm1_g[:, None, :], win, w2, ga2,
                        gla_ab[:, None, :], vec, obd, cos_t, sin_t, **extra)
        if l == 0:
            rwp_first = rwp
        ya = _rwkv(rwp, par[l], B, S, RWKV_CHUNKS_PER_STEP)
        ybc = _glaret(gr, lng[l], dec, kdec, qdec, cdec, B, S, GLARET_CHUNKS_PER_STEP)
        x2d = _outffn(x2d, ya, ybc, mod, l, S, FFN_TOKENS_PER_STEP, l == L - 1, norm2_g[:, None, :],
                      wo, wg, wu, wd, norm_f_g[None])
    return x2d.reshape(B, S, D)
```

```python
import functools

import numpy as np
import jax
import jax.numpy as jnp
from jax import lax
from jax.experimental import pallas as pl
from jax.experimental.pallas import tpu as pltpu

F32 = jnp.float32
BF16 = jnp.bfloat16

CHUNK = 64
EPS = 1e-6
HEAD_DV = 64
RWKV_HEADS = 8
RWKV_WIDTH = RWKV_HEADS * HEAD_DV
RWKV_GN_EPS = 64e-5
GLA_HEADS = 4
GLA_DK = 32
GLA_GATE_TAU = 16.0
RET_HEADS = 4
RET_DK = 32
ROPE_BASE = 10000.0
HEADS_PER_GROUP = 4
GROUP_W = HEADS_PER_GROUP * HEAD_DV
VMEM_LIMIT_BYTES = 56 * 1024 * 1024
RWKV_CHUNKS_PER_STEP = 16
RWKV_WAVES_PER_STEP = 4
PREP_TOKENS_PER_STEP = 512
FFN_TOKENS_PER_STEP = 512
FFN_COLUMN_CHUNK = 256

_GQ, _GK, _GLA, _GV, _GG, _RQ, _RK, _RV, _RG, _GR_W = 0, 128, 256, 384, 640, 896, 1024, 1152, 1408, 1664
_W_MISC, _W_LORA_H, _W_LORA_S, _W_IN, _WIN_W = 0, 128, 384, 640, 3712


def _dot(a, b):
    return jnp.dot(a.astype(BF16), b.astype(BF16), preferred_element_type=F32)


def _dot_nt(a, b):
    return lax.dot_general(a.astype(BF16), b.astype(BF16), (((1,), (1,)), ((), ())),
                           preferred_element_type=F32)


def _dot_tn(a, b):
    return lax.dot_general(a.astype(BF16), b.astype(BF16), (((0,), (0,)), ((), ())),
                           preferred_element_type=F32)


def _group_sums(xs, ones_bd, coarse=()):
    hi = [x.astype(BF16) for x in xs]
    lo = [(x - h.astype(F32)).astype(BF16) for x, h in zip(xs, hi)]
    terms = hi + [x.astype(BF16) for x in coarse] + lo
    s = jnp.dot(jnp.concatenate(terms, axis=0), ones_bd.astype(BF16), preferred_element_type=F32)
    offs = np.cumsum([0] + [t.shape[0] for t in terms])
    nx, nc = len(xs), len(coarse)
    out = [s[offs[i]:offs[i + 1]] + s[offs[nx + nc + i]:offs[nx + nc + i + 1]] for i in range(nx)]
    return out + [s[offs[nx + i]:offs[nx + i + 1]] for i in range(nc)]


def _cumsum_rows(x):
    n = x.shape[0]
    row = _iota(x.shape, 0)
    s = 1
    while s < n:
        x = x + jnp.where(row >= s, pltpu.roll(x, s, 0), 0.0)
        s *= 2
    return x


def _sigmoid(x):
    return 1.0 / (1.0 + jnp.exp(-x))


def _softplus(x):
    return jnp.maximum(x, 0.0) + jnp.log(1.0 + jnp.exp(-jnp.abs(x)))


def _iota(shape, axis):
    return lax.broadcasted_iota(jnp.int32, shape, axis)


def _bd_rows(x, groups):
    c, w = x.shape
    n = w // groups
    t = jnp.concatenate([x] * groups, axis=0)
    keep = (_iota(t.shape, 0) // c) == (_iota(t.shape, 1) // n)
    return jnp.where(keep, t, 0.0)


def _bd_dot(x, y, groups, nt=False):
    e = _bd_rows(y, groups)
    return _dot_nt(x, e) if nt else _dot(x, e)


def _block_mask(rows, cols, rblk, cblk):
    return (_iota((rows, cols), 0) // rblk) == (_iota((rows, cols), 1) // cblk)


def _const_spec(shape):
    nd = len(shape)
    return pl.BlockSpec(shape, lambda *_: (0,) * nd, pipeline_mode=pl.Buffered(1))


def _layer_spec(shape, layer):
    nd = len(shape)
    return pl.BlockSpec((None,) + tuple(shape[1:]), lambda *_: (layer,) + (0,) * (nd - 1),
                        pipeline_mode=pl.Buffered(1))


def _adaln_kernel(c_ref, w_ref, b_ref, o_ref):
    c = c_ref[...]
    cond = c * _sigmoid(c)
    o_ref[0] = _dot(cond, w_ref[0]) + b_ref[0]


def _adaln(c, ada_w, ada_b):
    L, D, D6 = ada_w.shape
    B = c.shape[0]
    tn = 1536
    return pl.pallas_call(
        _adaln_kernel,
        grid=(L, D6 // tn),
        in_specs=[pl.BlockSpec((B, D), lambda l, j: (0, 0)),
                  pl.BlockSpec((1, D, tn), lambda l, j: (l, 0, j)),
                  pl.BlockSpec((1, 1, tn), lambda l, j: (l, 0, j))],
        out_specs=pl.BlockSpec((1, B, tn), lambda l, j: (l, 0, j)),
        out_shape=jax.ShapeDtypeStruct((L, B, D6), F32),
        compiler_params=pltpu.CompilerParams(
            dimension_semantics=("arbitrary", "arbitrary"), vmem_limit_bytes=VMEM_LIMIT_BYTES),
        name="adaln_mod",
    )(c, ada_w, ada_b.reshape(L, 1, D6))


def _prep_kernel(tiles_per_seq, has_vres, *refs):
    if has_vres:
        (x_ref, xh_ref, mod_ref, n1g_ref, win_ref, w2_ref, ga2_ref, gab_ref, vec_ref, obd_ref,
         cos_ref, sin_ref, v2_ref, vf_ref, rwp_ref, gr_ref) = refs
    else:
        (x_ref, xh_ref, mod_ref, n1g_ref, win_ref, w2_ref, ga2_ref, gab_ref, vec_ref, obd_ref,
         cos_ref, sin_ref, rwp_ref, gr_ref) = refs
    tm = x_ref.shape[0]
    rw = RWKV_WIDTH
    first = (pl.program_id(0) % tiles_per_seq) == 0

    xe = jnp.concatenate([xh_ref[...], x_ref[...]], axis=0)
    mod = mod_ref[0]
    ms = jnp.mean(xe * xe, axis=-1, keepdims=True)
    he = xe * lax.rsqrt(ms + EPS) * (n1g_ref[...] * (1.0 + mod[1:2])) + mod[0:1]
    he = jnp.concatenate([jnp.where(first, 0.0, he[:8]), he[8:]], axis=0)

    heb = he.astype(BF16)

    def project(lo, hi):
        return jnp.dot(heb, win_ref[:, lo:hi], preferred_element_type=F32)

    def prev_rows(t):
        return pltpu.roll(t, 1, 0)[8:]

    p_lora = project(_W_MISC, _W_IN)
    p_rkv = project(_W_IN, _W_IN + 3 * rw)
    vec = vec_ref[...]

    pre = p_lora[8:, _W_LORA_H:_W_LORA_S] + prev_rows(p_lora[:, _W_LORA_S:_W_IN])
    lane = _iota(pre.shape, 1)
    act = jnp.where(lane < 64, jnp.tanh(pre), jnp.where(lane < 128, pre, _sigmoid(pre)))
    second = _dot(act, w2_ref[...])
    p_gla = project(_W_IN + 3 * rw, _W_IN + 3 * rw + 768)
    lw = -jnp.exp(-_softplus(-(vec[3:4] + second[:, 0:rw])) - 0.5)
    a = _sigmoid(vec[4:5] + second[:, rw:2 * rw])
    g = second[:, 2 * rw:3 * rw]

    p = p_rkv[8:]
    ps = prev_rows(p_rkv)
    r = p[:, 0:rw]
    r = r + (ps[:, 0:rw] - r) * vec[0:1]
    k = p[:, rw:2 * rw]
    k = k + (ps[:, rw:2 * rw] - k) * vec[1:2]
    v = p[:, 2 * rw:3 * rw]
    v = v + (ps[:, 2 * rw:3 * rw] - v) * vec[2:3]

    kk = k * vec[6:7]
    kk2 = kk * kk
    ss = jnp.concatenate(_group_sums([], obd_ref[...], coarse=[kk2[:, :GROUP_W], kk2[:, GROUP_W:]]), axis=1)
    kk = kk / jnp.maximum(jnp.sqrt(ss), 1e-12)
    k = k * (1.0 + (a - 1.0) * vec[7:8])
    misc = p_lora[8:, _W_MISC:_W_LORA_H]
    if has_vres:
        mv = jnp.where(_iota(misc.shape, 1) < 64, misc, prev_rows(p_lora[:, _W_MISC:_W_LORA_H]))
        v = v + (vf_ref[0] - v) * _sigmoid(vec[5:6] + _dot(mv, v2_ref[...]))
    p_ret = project(_W_IN + 3 * rw + 768, _WIN_W)

    rwp_ref[0] = r
    rwp_ref[1] = lw
    rwp_ref[2] = k
    rwp_ref[3] = v
    rwp_ref[4] = kk
    rwp_ref[5] = kk * a
    rwp_ref[6] = g

    p = p_gla[8:]
    gr_ref[:, _GQ:_GK] = p[:, 0:128] * (GLA_DK ** -0.5)
    gr_ref[:, _GK:_GLA] = p[:, 128:256]
    la_pre = _dot(misc, ga2_ref[...]) + gab_ref[...]
    gr_ref[:, _GLA:_GV] = -_softplus(-la_pre) * (1.0 / GLA_GATE_TAU)
    gr_ref[:, _GV:_GG] = p[:, 256:512]
    gate = p[:, 512:768]
    gr_ref[:, _GG:_RQ] = gate * _sigmoid(gate)

    p = p_ret[8:]
    o = 0
    cos = cos_ref[...]
    sin = sin_ref[...]
    lo_half = (_iota((tm, 128), 1) % RET_DK) < (RET_DK // 2)

    def rope(t):
        swapped = jnp.where(lo_half, pltpu.roll(t, 128 - RET_DK // 2, 1), pltpu.roll(t, RET_DK // 2, 1))
        return t * cos + swapped * sin

    gr_ref[:, _RQ:_RK] = rope(p[:, o:o + 128]) * (RET_DK ** -0.5)
    gr_ref[:, _RK:_RV] = rope(p[:, o + 128:o + 256])
    gr_ref[:, _RV:_RG] = p[:, o + 256:o + 512]
    gate = p[:, o + 512:o + 768]
    gr_ref[:, _RG:_GR_W] = gate * _sigmoid(gate)


def _prep(x2d, mod, layer, seq, tm, has_vres, n1g, win, w2, ga2, gab, vec, obd, cos_t, sin_t,
          v2=None, rwp_first=None):
    T, D = x2d.shape
    tiles_per_seq = seq // tm
    n_tiles = T // tm
    per_layer = [n1g, win, w2, ga2, gab, vec]
    in_specs = [pl.BlockSpec((tm, D), lambda i: (i, 0)),
                pl.BlockSpec((8, D), lambda i: (jnp.maximum(i * (tm // 8) - 1, 0), 0)),
                pl.BlockSpec((None, 1, 6, D), lambda i: (layer, i // tiles_per_seq, 0, 0))]
    in_specs += [_layer_spec(a.shape, layer) for a in per_layer]
    in_specs += [_const_spec(obd.shape),
                 pl.BlockSpec((tm, 128), lambda i: (i % tiles_per_seq, 0)),
                 pl.BlockSpec((tm, 128), lambda i: (i % tiles_per_seq, 0))]
    args = [x2d, x2d, mod] + per_layer + [obd, cos_t, sin_t]
    if has_vres:
        in_specs += [_layer_spec(v2.shape, layer),
                     pl.BlockSpec((1, tm, RWKV_WIDTH), lambda i: (3, i, 0))]
        args += [v2, rwp_first]
    return pl.pallas_call(
        functools.partial(_prep_kernel, tiles_per_seq, has_vres),
        grid=(n_tiles,),
        in_specs=in_specs,
        out_specs=[pl.BlockSpec((7, tm, RWKV_WIDTH), lambda i: (0, i, 0)),
                   pl.BlockSpec((tm, _GR_W), lambda i: (i, 0))],
        out_shape=[jax.ShapeDtypeStruct((7, T, RWKV_WIDTH), F32),
                   jax.ShapeDtypeStruct((T, _GR_W), F32)],
        compiler_params=pltpu.CompilerParams(
            dimension_semantics=("parallel",), vmem_limit_bytes=VMEM_LIMIT_BYTES),
        name="proj_prep",
    )(*args)


def _chunk_masks(groups):
    C = CHUNK
    row = _iota((C, groups * C), 0)
    col = _iota((C, groups * C), 1) % C
    return row > col, row >= col, (row == col).astype(F32)


def _rwkv_chunk_scaled(r, lw, k, v, kk, kb):
    C = CHUNK
    cl = _cumsum_rows(lw)
    cle = cl[C - 1:C]
    e_neg = jnp.exp(-cl)
    e_end = jnp.exp(cle - cl)
    at = -kk * jnp.exp(cl - lw)
    rt = r * jnp.exp(cl)
    return dict(at=at, rt=rt, ar=jnp.concatenate([at, rt], axis=0), bt=kb * e_neg, kt=k * e_neg,
                bk_end=jnp.concatenate([kb * e_end, k * e_end], axis=0), g_end=jnp.exp(cle), v=v)


def _rwkv_chunk_factors(units, masks):
    C, G = CHUNK, HEADS_PER_GROUP
    strict, incl, ident = masks
    n = range(len(units))
    at, rt, ar, v = ([u[name] for u in units] for name in ("at", "rt", "ar", "v"))
    sb = [_bd_dot(ar[i], units[i]["bt"], G, nt=True) for i in n]
    sk = [_bd_dot(ar[i], units[i]["kt"], G, nt=True) for i in n]
    yield
    a_rb = [jnp.where(incl, sb[i][C:], 0.0) for i in n]
    a_ak = [jnp.where(strict, sk[i][:C], 0.0) for i in n]
    a_rk = [jnp.where(incl, sk[i][C:], 0.0) for i in n]

    a_ab = [jnp.where(strict, sb[i][:C], 0.0) for i in n]
    tm = [ident + a_ab[i] for i in n]
    xp = [_bd_dot(a_ab[i], a_ab[i], G) for i in n]
    yield
    for _ in range(4):
        rr = [_bd_dot(jnp.concatenate([tm[i], xp[i]], axis=0), xp[i], G) for i in n]
        tm = [tm[i] + rr[i][:C] for i in n]
        xp = [rr[i][C:] for i in n]
        yield
    tm = [tm[i] + _bd_dot(tm[i], xp[i], G) for i in n]
    yield

    vv = [_bd_dot(jnp.concatenate([a_ak[i], a_rk[i]], axis=0), v[i], G) for i in n]
    yield
    wt = [_bd_dot(tm[i], at[i], G) for i in n]
    yield
    ut = [_bd_dot(tm[i], vv[i][:C], G) for i in n]
    yield
    qh = [rt[i] + _bd_dot(a_rb[i], wt[i], G) for i in n]
    yield
    yh = [_bd_dot(a_rb[i], ut[i], G) + vv[i][C:] for i in n]
    yield
    zero = jnp.zeros((C, 128), F32)
    left = _iota((C, 128), 1) < HEAD_DV
    pm, zm = [[] for _ in n], [[] for _ in n]
    for s in range(GROUP_W // 128):
        sl = slice(128 * s, 128 * (s + 1))
        pz = [_dot_tn(units[i]["bk_end"][:, sl],
                      jnp.concatenate([jnp.concatenate([wt[i][:, sl], ut[i][:, sl]], axis=1),
                                       jnp.concatenate([zero, v[i][:, sl]], axis=1)], axis=0)) for i in n]
        for i in n:
            pm[i].append(jnp.where(left, pz[i][:C, :128], pz[i][C:, :128]))
            zm[i].append(jnp.where(left, pz[i][:C, 128:], pz[i][C:, 128:]))
        yield
    pm = [jnp.concatenate(pm[i], axis=1) + jnp.where(ident > 0, units[i]["g_end"], 0.0) for i in n]
    zm = [jnp.concatenate(zm[i], axis=1) for i in n]
    return qh, yh, pm, zm


def _interleave(main, side):
    side_done, side_out = False, None
    while True:
        try:
            next(main)
        except StopIteration as stop:
            main_out = stop.value
            break
        if not side_done:
            try:
                next(side)
            except StopIteration as stop:
                side_done, side_out = True, stop.value
    while not side_done:
        try:
            next(side)
        except StopIteration as stop:
            side_done, side_out = True, stop.value
    return main_out, side_out


def _rwkv_kernel(cb, rwp_ref, par_ref, o_ref, st_ref):
    C, W, N = CHUNK, GROUP_W, HEAD_DV
    ngroups = o_ref.shape[1] // W

    @pl.when(pl.program_id(1) == 0)
    def _():
        st_ref[...] = jnp.zeros_like(st_ref)

    masks = _chunk_masks(HEADS_PER_GROUP)
    ones_bd = _block_mask(W, W, N, N).astype(F32)
    m = [st_ref[gi] for gi in range(ngroups)]

    def wave_units(chunks):
        return [(slice(j * C, (j + 1) * C), slice(gi * W, (gi + 1) * W), gi)
                for j in chunks for gi in range(ngroups)]

    def factors(units):
        scaled = [_rwkv_chunk_scaled(*[rwp_ref[i, rows, lanes] for i in range(6)])
                  for rows, lanes, _ in units]
        return _rwkv_chunk_factors(scaled, masks)

    def chain(units, fac):
        qh, yh, pm, zm = fac
        y = []
        for u, (_, _, gi) in enumerate(units):
            ym = _dot(jnp.concatenate([qh[u], pm[u]], axis=0), _bd_rows(m[gi], HEADS_PER_GROUP))
            y.append(ym[:C] + yh[u])
            m[gi] = ym[C:] + zm[u]
            yield
        return y

    def epilogue(units, y):
        n = range(len(units))
        par = [par_ref[:, lanes] for _, lanes, _ in units]
        rkr = [rwp_ref[0, rows, lanes] * rwp_ref[2, rows, lanes] * par[u][0:1]
               for u, (rows, lanes, _) in enumerate(units)]
        sums = _group_sums(y + rkr, ones_bd)
        yield
        yc = [y[u] - sums[u] * (1.0 / N) for u in n]
        var = _group_sums([], ones_bd, coarse=[yc[u] * yc[u] for u in n])
        yield
        for u, (rows, lanes, _) in enumerate(units):
            yn = yc[u] * lax.rsqrt(var[u] * (1.0 / N) + RWKV_GN_EPS) * par[u][1:2] + par[u][2:3]
            o_ref[rows, lanes] = ((yn + sums[len(units) + u] * rwp_ref[3, rows, lanes])
                                  * rwp_ref[6, rows, lanes])
            yield

    def in_turn(*gens):
        outs = []
        for g in gens:
            outs.append((yield from g))
        return outs

    nw = RWKV_WAVES_PER_STEP
    per_wave = cb // nw
    units = [wave_units(range(w * per_wave, (w + 1) * per_wave)) for w in range(nw)]
    fac, ys = [None] * nw, [None] * nw
    for k in range(nw):
        side = []
        if k >= 1:
            side.append(chain(units[k - 1], fac[k - 1]))
        if k >= 2:
            side.append(epilogue(units[k - 2], ys[k - 2]))
        fac[k], outs = _interleave(factors(units[k]), in_turn(*side))
        if k >= 1:
            ys[k - 1] = outs[0]
    drain = epilogue(units[nw - 2], ys[nw - 2]) if nw >= 2 else iter(())
    _, ys[nw - 1] = _interleave(drain, chain(units[nw - 1], fac[nw - 1]))
    _interleave(epilogue(units[nw - 1], ys[nw - 1]), iter(()))
    for gi in range(ngroups):
        st_ref[gi] = m[gi]


def _rwkv(rwp, par, batch, seq, cb):
    _, T, RW = rwp.shape
    steps = seq // (CHUNK * cb)
    return pl.pallas_call(
        functools.partial(_rwkv_kernel, cb),
        grid=(batch, steps),
        in_specs=[pl.BlockSpec((7, cb * CHUNK, RW), lambda b, c: (0, b * steps + c, 0)),
                  _const_spec(par.shape)],
        out_specs=pl.BlockSpec((cb * CHUNK, RW), lambda b, c: (b * steps + c, 0)),
        out_shape=jax.ShapeDtypeStruct((T, RW), F32),
        scratch_shapes=[pltpu.VMEM((RW // GROUP_W, HEAD_DV, GROUP_W), F32)],
        compiler_params=pltpu.CompilerParams(
            dimension_semantics=("parallel", "arbitrary"), vmem_limit_bytes=VMEM_LIMIT_BYTES),
        name="rwkv7_chunk",
    )(rwp, par)


def _glaret_stages(cb, gr_ref, tables, sg_ref, sr_ref, reset, o_ref):
    C, G, DV = CHUNK, GLA_HEADS, HEAD_DV
    WV = G * DV
    lng, dec, kdec, qdec, cdec = tables
    _, incl, _ = _chunk_masks(G)
    ones_bd = _block_mask(WV, WV, DV, DV).astype(F32)
    st_mask = _block_mask(WV, G * GLA_DK, DV, GLA_DK)

    n = range(cb)
    rows = [slice(j * C, (j + 1) * C) for j in n]
    gq = [gr_ref[rows[j], _GQ:_GK] for j in n]
    gk = [gr_ref[rows[j], _GK:_GLA] for j in n]
    gv = [gr_ref[rows[j], _GV:_GG] for j in n]
    bc = [_cumsum_rows(gr_ref[rows[j], _GLA:_GV]) for j in n]
    be = [bc[j][C - 1:C] for j in n]
    mid = [bc[j][C // 2 - 1:C // 2] for j in n]
    ep = [jnp.exp(bc[j] - mid[j]) for j in n]
    en = [jnp.exp(mid[j] - bc[j]) for j in n]
    qp = [gq[j] * jnp.exp(bc[j]) for j in n]
    att_lo = [_dot_nt(gq[j] * ep[j], _bd_rows(gk[j] * en[j], G)) for j in n]
    yield
    att_hi = [_dot_nt(gq[j] * en[j], _bd_rows(gk[j] * ep[j], G)) for j in n]
    yield
    rq = [gr_ref[rows[j], _RQ:_RK] for j in n]
    rk = [gr_ref[rows[j], _RK:_RV] for j in n]
    rv = [gr_ref[rows[j], _RV:_RG] for j in n]
    sc = [_dot_nt(rq[j], _bd_rows(rk[j], G)) * dec for j in n]
    yield
    g_intra = [_bd_dot(jnp.where(incl, att_lo[j], att_hi[j]), gv[j], G) for j in n]
    yield
    r_intra = [_bd_dot(sc[j], rv[j], G) for j in n]
    yield
    g_kv = [jnp.where(st_mask, _dot_tn(gv[j], gk[j] * jnp.exp(be[j] - bc[j])), 0.0) for j in n]
    yield
    r_kv = [jnp.where(st_mask, _dot_tn(rv[j], rk[j] * kdec), 0.0) for j in n]
    yield

    if reset is True:
        sg, sr = jnp.zeros(sg_ref.shape, F32), jnp.zeros(sr_ref.shape, F32)
    else:
        sg = jnp.where(reset, 0.0, sg_ref[...])
        sr = jnp.where(reset, 0.0, sr_ref[...])
    g_o = []
    r_o = []
    for j in n:
        g_o.append(g_intra[j] + _dot_nt(qp[j], sg))
        sg = sg * jnp.exp(be[j]) + g_kv[j]
        r_o.append(r_intra[j] + _dot_nt(rq[j] * qdec, sr))
        sr = sr * cdec + r_kv[j]
        yield
    sg_ref[...] = sg
    sr_ref[...] = sr

    sums = _group_sums(r_o, ones_bd, coarse=[g_o[j] * g_o[j] for j in n])
    yield
    r_c = [r_o[j] - sums[j] * (1.0 / DV) for j in n]
    r_var = _group_sums([], ones_bd, coarse=[r_c[j] * r_c[j] for j in n])
    yield
    for j in n:
        o_ref[rows[j], 0:WV] = (g_o[j] * lax.rsqrt(sums[cb + j] * (1.0 / DV) + EPS) * lng
                                * gr_ref[rows[j], _GG:_RQ])
        o_ref[rows[j], WV:2 * WV] = (r_c[j] * lax.rsqrt(r_var[j] * (1.0 / DV) + EPS)
                                     * gr_ref[rows[j], _RG:_GR_W])
        yield


def _ffn_stages(final_norm, x_ref, ya_ref, ybc_ref, mod_ref, n2g_ref, wo_ref, wg_ref, wu_ref, wd_ref,
                nfg_ref, o_ref):
    mod = mod_ref[0]
    y = jnp.concatenate([ya_ref[...], ybc_ref[...]], axis=-1)
    x = x_ref[...] + mod[2:3] * _dot(y, wo_ref[...])
    yield
    ms = jnp.mean(x * x, axis=-1, keepdims=True)
    h = (x * lax.rsqrt(ms + EPS) * (n2g_ref[...] * (1.0 + mod[4:5])) + mod[3:4]).astype(BF16)
    acts = []
    for c in range(0, wg_ref.shape[1], FFN_COLUMN_CHUNK):
        gate = jnp.dot(h, wg_ref[:, c:c + FFN_COLUMN_CHUNK], preferred_element_type=F32)
        yield
        up = jnp.dot(h, wu_ref[:, c:c + FFN_COLUMN_CHUNK], preferred_element_type=F32)
        acts.append((gate * _sigmoid(gate) * up).astype(BF16))
        yield
    x = x + mod[5:6] * jnp.dot(jnp.concatenate(acts, axis=1), wd_ref[...], preferred_element_type=F32)
    if final_norm:
        ms = jnp.mean(x * x, axis=-1, keepdims=True)
        x = x * lax.rsqrt(ms + EPS) * nfg_ref[...]
    o_ref[...] = x


def _ffn_glaret_kernel(final_norm, tiles_per_seq, x_ref, ya_ref, gr_next_ref, gr_first_ref, mod_ref, n2g_ref,
                       wo_ref, wg_ref, wu_ref, wd_ref, nfg_ref, lng_ref, dec_ref, kdec_ref, qdec_ref, cdec_ref,
                       o_ref, ybc_ref, ybc_next_ref, sg_ref, sr_ref):
    i = pl.program_id(0)
    cb = x_ref.shape[0] // CHUNK
    tables = (lng_ref[...], dec_ref[...], kdec_ref[...], qdec_ref[...], cdec_ref[...])

    @pl.when(i == 0)
    def _():
        for _ in _glaret_stages(cb, gr_first_ref, tables, sg_ref, sr_ref, True, ybc_ref):
            pass

    _interleave(
        _ffn_stages(final_norm, x_ref, ya_ref, ybc_ref, mod_ref, n2g_ref, wo_ref, wg_ref, wu_ref, wd_ref,
                    nfg_ref, o_ref),
        _glaret_stages(cb, gr_next_ref, tables, sg_ref, sr_ref, (i + 1) % tiles_per_seq == 0, ybc_next_ref))
    ybc_ref[...] = ybc_next_ref[...]


def _ffn_glaret(x2d, ya, gr, mod, layer, seq, tm, final_norm, n2g, wo, wg, wu, wd, nfg, lng, dec, kdec,
                qdec, cdec):
    T, D = x2d.shape
    tiles_per_seq = seq // tm
    n_tiles = T // tm
    per_layer = [n2g, wo, wg, wu, wd]
    tables = [lng, dec, kdec, qdec, cdec]
    wv2 = 2 * GLA_HEADS * HEAD_DV
    return pl.pallas_call(
        functools.partial(_ffn_glaret_kernel, final_norm, tiles_per_seq),
        grid=(n_tiles,),
        in_specs=[pl.BlockSpec((tm, D), lambda i: (i, 0)),
                  pl.BlockSpec((tm, ya.shape[1]), lambda i: (i, 0)),
                  pl.BlockSpec((tm, _GR_W), lambda i: (jnp.minimum(i + 1, n_tiles - 1), 0)),
                  pl.BlockSpec((tm, _GR_W), lambda i: (0, 0), pipeline_mode=pl.Buffered(1)),
                  pl.BlockSpec((None, 1, 6, D), lambda i: (layer, i // tiles_per_seq, 0, 0))]
                 + [_layer_spec(a.shape, layer) for a in per_layer]
                 + [_const_spec(a.shape) for a in [nfg] + tables],
        out_specs=pl.BlockSpec((tm, D), lambda i: (i, 0)),
        out_shape=jax.ShapeDtypeStruct((T, D), F32),
        scratch_shapes=[pltpu.VMEM((tm, wv2), F32),
                        pltpu.VMEM((tm, wv2), F32),
                        pltpu.VMEM((wv2 // 2, GLA_HEADS * GLA_DK), F32),
                        pltpu.VMEM((wv2 // 2, RET_HEADS * RET_DK), F32)],
        compiler_params=pltpu.CompilerParams(
            dimension_semantics=("arbitrary",), vmem_limit_bytes=VMEM_LIMIT_BYTES),
        name="outproj_swiglu_gla_ret",
    )(x2d, ya, gr, gr, mod, *per_layer, nfg, *tables)


def _rope_tables(seq):
    half = RET_DK // 2
    inv_freq = ROPE_BASE ** (-jnp.arange(half, dtype=F32) / half)
    ang = jnp.arange(seq, dtype=F32)[:, None] * inv_freq[None, :]
    cos, sin = jnp.cos(ang), jnp.sin(ang)
    cos_t = jnp.tile(jnp.concatenate([cos, cos], axis=-1), (1, RET_HEADS))
    sin_t = jnp.tile(jnp.concatenate([-sin, sin], axis=-1), (1, RET_HEADS))
    return cos_t, sin_t


def _retention_tables():
    H, C = RET_HEADS, CHUNK
    log_gamma = jnp.log1p(-(2.0 ** (-5.0 - jnp.arange(H, dtype=F32))))
    pos = jnp.arange(C, dtype=F32)
    intra = jnp.exp(log_gamma[:, None, None] * jnp.abs(pos[:, None] - pos[None, :]))
    dec = jnp.transpose(intra, (1, 0, 2)).reshape(C, H * C)
    k_dec = jnp.exp(log_gamma[None, :] * (C - 1.0 - pos)[:, None])
    q_dec = jnp.exp(log_gamma[None, :] * (pos + 1.0)[:, None])
    chunk_dec = jnp.exp(log_gamma * C)
    kdec = jnp.repeat(k_dec, RET_DK, axis=1)
    qdec = jnp.repeat(q_dec, RET_DK, axis=1)
    cdec = jnp.repeat(chunk_dec, RET_DK)[None, :]
    return dec, kdec, qdec, cdec


def kernel(x, c, ada_w, ada_b, norm1_g, norm2_g, w_in, w_out, rk_mu_rkv, rk_mu_x, rk_w0, rk_w1, rk_w2, rk_a0, rk_a1, rk_a2, rk_g1, rk_g2, rk_k_k, rk_k_a, rk_r_k, rk_ln_g, rk_ln_b, rk_mu_v, rk_v0, rk_v1, rk_v2, gla_a1, gla_a2, gla_ab, gla_ln_g, ffn_w_gate, ffn_w_up, ffn_w_down, norm_f_g):
    B, S, D = x.shape
    L = ada_w.shape[0]
    T = B * S
    assert S % PREP_TOKENS_PER_STEP == 0 and S % FFN_TOKENS_PER_STEP == 0
    assert S % (CHUNK * RWKV_CHUNKS_PER_STEP) == 0 and FFN_TOKENS_PER_STEP % CHUNK == 0

    mod = _adaln(c, ada_w, ada_b).reshape(L, B, 6, D)
    cos_t, sin_t = _rope_tables(S)
    dec, kdec, qdec, cdec = _retention_tables()
    obd = _block_mask(GROUP_W, GROUP_W, HEAD_DV, HEAD_DV).astype(BF16)
    RW = RWKV_WIDTH

    def zeros(*shape):
        return jnp.zeros(shape, F32)

    mu_v = jnp.concatenate([zeros(1, D), rk_mu_v], axis=0)
    v0 = jnp.concatenate([zeros(1, RW), rk_v0], axis=0)
    v1 = jnp.concatenate([zeros(1, D, rk_v1.shape[2]), rk_v1], axis=0)
    v2 = jnp.concatenate([zeros(1, rk_v2.shape[1], RW), rk_v2], axis=0)

    def on_h(mu, w):
        return (1.0 - mu)[:, :, None] * w

    def on_prev(mu, w):
        return mu[:, :, None] * w

    mu = rk_mu_x
    misc = jnp.concatenate([gla_a1, zeros(L, D, 16), on_h(mu_v, v1), on_prev(mu_v, v1), zeros(L, D, 32)],
                           axis=2)
    lora_h = jnp.concatenate([on_h(mu[:, 0], rk_w1), on_h(mu[:, 1], rk_a1), on_h(mu[:, 2], rk_g1)], axis=2)
    lora_s = jnp.concatenate([on_prev(mu[:, 0], rk_w1), on_prev(mu[:, 1], rk_a1), on_prev(mu[:, 2], rk_g1)],
                             axis=2)
    win = jnp.concatenate([t.astype(BF16) for t in (misc, lora_h, lora_s, w_in)], axis=2)
    assert win.shape[2] == _WIN_W and misc.shape[2] == _W_LORA_H - _W_MISC
    w2 = jnp.concatenate([
        jnp.concatenate([rk_w2, zeros(L, 64, 2 * RW)], axis=2),
        jnp.concatenate([zeros(L, 64, RW), rk_a2, zeros(L, 64, RW)], axis=2),
        jnp.concatenate([zeros(L, 128, 2 * RW), rk_g2], axis=2)], axis=1).astype(BF16)
    ga2 = jnp.concatenate([gla_a2, zeros(L, 128 - gla_a2.shape[1], gla_a2.shape[2])], axis=1).astype(BF16)
    v2p = jnp.concatenate([zeros(L, 32, RW), v2, v2, zeros(L, 32, RW)], axis=1).astype(BF16)
    vec = jnp.stack([rk_mu_rkv[:, 0], rk_mu_rkv[:, 1], rk_mu_rkv[:, 2], rk_w0, rk_a0, v0, rk_k_k, rk_k_a],
                    axis=1)
    par = jnp.stack([rk_r_k.reshape(L, RW), rk_ln_g, rk_ln_b], axis=1)
    lng = jnp.tile(gla_ln_g, (1, GLA_HEADS))[:, None, :]
    wo, wg, wu, wd = (w.astype(BF16) for w in (w_out, ffn_w_gate, ffn_w_up, ffn_w_down))

    x2d = x.reshape(T, D)
    rwp_first = None
    for l in range(L):
        has_vres = l > 0
        extra = dict(v2=v2p, rwp_first=rwp_first) if has_vres else {}
        rwp, gr = _prep(x2d, mod, l, S, PREP_TOKENS_PER_STEP, has_vres, norm1_g[:, None, :], win, w2, ga2,
                        gla_ab[:, None, :], vec, obd, cos_t, sin_t, **extra)
        if l == 0:
            rwp_first = rwp
        ya = _rwkv(rwp, par[l], B, S, RWKV_CHUNKS_PER_STEP)
        x2d = _ffn_glaret(x2d, ya, gr, mod, l, S, FFN_TOKENS_PER_STEP, l == L - 1, norm2_g[:, None, :],
                          wo, wg, wu, wd, norm_f_g[None], lng[l], dec, kdec, qdec, cdec)
    return x2d.reshape(B, S, D)
```

```python
import functools

import numpy as np
import jax
import jax.numpy as jnp
from jax import lax
from jax.experimental import pallas as pl
from jax.experimental.pallas import tpu as pltpu

F32 = jnp.float32
BF16 = jnp.bfloat16

CHUNK = 64
EPS = 1e-6
HEAD_DV = 64
RWKV_HEADS = 8
RWKV_WIDTH = RWKV_HEADS * HEAD_DV
RWKV_GN_EPS = 64e-5
GLA_HEADS = 4
GLA_DK = 32
GLA_GATE_TAU = 16.0
RET_HEADS = 4
RET_DK = 32
ROPE_BASE = 10000.0
HEADS_PER_GROUP = 4
GROUP_W = HEADS_PER_GROUP * HEAD_DV
VMEM_LIMIT_BYTES = 56 * 1024 * 1024
RWKV_CHUNKS_PER_STEP = 16
INV_BASE_BLOCK = 8
RWKV_WAVES_PER_STEP = 4
GLARET_CHUNKS_PER_STEP = 16
PREP_TOKENS_PER_STEP = 512
FFN_TOKENS_PER_STEP = 512

_GQ, _GK, _GLA, _GV, _GG, _RQ, _RK, _RV, _RG, _GR_W = 0, 128, 256, 384, 640, 896, 1024, 1152, 1408, 1664
_W_MISC, _W_LORA_H, _W_LORA_S, _W_IN, _WIN_W = 0, 128, 384, 640, 3712


def _dot(a, b):
    return jnp.dot(a.astype(BF16), b.astype(BF16), preferred_element_type=F32)


def _dot_nt(a, b):
    return lax.dot_general(a.astype(BF16), b.astype(BF16), (((1,), (1,)), ((), ())),
                           preferred_element_type=F32)


def _dot_tn(a, b):
    return lax.dot_general(a.astype(BF16), b.astype(BF16), (((0,), (0,)), ((), ())),
                           preferred_element_type=F32)


def _group_sums(xs, ones_bd, coarse=()):
    hi = [x.astype(BF16) for x in xs]
    lo = [(x - h.astype(F32)).astype(BF16) for x, h in zip(xs, hi)]
    terms = hi + [x.astype(BF16) for x in coarse] + lo
    s = jnp.dot(jnp.concatenate(terms, axis=0), ones_bd.astype(BF16), preferred_element_type=F32)
    offs = np.cumsum([0] + [t.shape[0] for t in terms])
    nx, nc = len(xs), len(coarse)
    out = [s[offs[i]:offs[i + 1]] + s[offs[nx + nc + i]:offs[nx + nc + i + 1]] for i in range(nx)]
    return out + [s[offs[nx + i]:offs[nx + i + 1]] for i in range(nc)]


def _cumsum_rows(x):
    n = x.shape[0]
    row = _iota(x.shape, 0)
    s = 1
    while s < n:
        x = x + jnp.where(row >= s, pltpu.roll(x, s, 0), 0.0)
        s *= 2
    return x


def _sigmoid(x):
    return 1.0 / (1.0 + jnp.exp(-x))


def _softplus(x):
    return jnp.maximum(x, 0.0) + jnp.log(1.0 + jnp.exp(-jnp.abs(x)))


def _iota(shape, axis):
    return lax.broadcasted_iota(jnp.int32, shape, axis)


def _bd_rows(x, groups):
    c, w = x.shape
    n = w // groups
    t = jnp.concatenate([x] * groups, axis=0)
    keep = (_iota(t.shape, 0) // c) == (_iota(t.shape, 1) // n)
    return jnp.where(keep, t, 0.0)


def _bd_dot(x, y, groups, nt=False):
    e = _bd_rows(y, groups)
    return _dot_nt(x, e) if nt else _dot(x, e)


def _block_mask(rows, cols, rblk, cblk):
    return (_iota((rows, cols), 0) // rblk) == (_iota((rows, cols), 1) // cblk)


def _const_spec(shape):
    nd = len(shape)
    return pl.BlockSpec(shape, lambda *_: (0,) * nd, pipeline_mode=pl.Buffered(1))


def _layer_spec(shape, layer):
    nd = len(shape)
    return pl.BlockSpec((None,) + tuple(shape[1:]), lambda *_: (layer,) + (0,) * (nd - 1),
                        pipeline_mode=pl.Buffered(1))


def _adaln_kernel(c_ref, w_ref, b_ref, o_ref):
    c = c_ref[...]
    cond = c * _sigmoid(c)
    o_ref[0] = _dot(cond, w_ref[0]) + b_ref[0]


def _adaln(c, ada_w, ada_b):
    L, D, D6 = ada_w.shape
    B = c.shape[0]
    tn = 1536
    return pl.pallas_call(
        _adaln_kernel,
        grid=(L, D6 // tn),
        in_specs=[pl.BlockSpec((B, D), lambda l, j: (0, 0)),
                  pl.BlockSpec((1, D, tn), lambda l, j: (l, 0, j)),
                  pl.BlockSpec((1, 1, tn), lambda l, j: (l, 0, j))],
        out_specs=pl.BlockSpec((1, B, tn), lambda l, j: (l, 0, j)),
        out_shape=jax.ShapeDtypeStruct((L, B, D6), F32),
        compiler_params=pltpu.CompilerParams(
            dimension_semantics=("arbitrary", "arbitrary"), vmem_limit_bytes=VMEM_LIMIT_BYTES),
        name="adaln_mod",
    )(c, ada_w, ada_b.reshape(L, 1, D6))


def _prep_kernel(tiles_per_seq, has_vres, *refs):
    if has_vres:
        (x_ref, xh_ref, mod_ref, n1g_ref, win_ref, w2_ref, ga2_ref, gab_ref, vec_ref, obd_ref,
         cos_ref, sin_ref, v2_ref, vf_ref, rwp_ref, gr_ref) = refs
    else:
        (x_ref, xh_ref, mod_ref, n1g_ref, win_ref, w2_ref, ga2_ref, gab_ref, vec_ref, obd_ref,
         cos_ref, sin_ref, rwp_ref, gr_ref) = refs
    tm = x_ref.shape[0]
    rw = RWKV_WIDTH
    first = (pl.program_id(0) % tiles_per_seq) == 0

    xe = jnp.concatenate([xh_ref[...], x_ref[...]], axis=0)
    mod = mod_ref[0]
    ms = jnp.mean(xe * xe, axis=-1, keepdims=True)
    he = xe * lax.rsqrt(ms + EPS) * (n1g_ref[...] * (1.0 + mod[1:2])) + mod[0:1]
    he = jnp.concatenate([jnp.where(first, 0.0, he[:8]), he[8:]], axis=0)

    heb = he.astype(BF16)

    def project(lo, hi):
        return jnp.dot(heb, win_ref[:, lo:hi], preferred_element_type=F32)

    def prev_rows(t):
        return pltpu.roll(t, 1, 0)[8:]

    p_lora = project(_W_MISC, _W_IN)
    p_rkv = project(_W_IN, _W_IN + 3 * rw)
    vec = vec_ref[...]

    pre = p_lora[8:, _W_LORA_H:_W_LORA_S] + prev_rows(p_lora[:, _W_LORA_S:_W_IN])
    lane = _iota(pre.shape, 1)
    act = jnp.where(lane < 64, jnp.tanh(pre), jnp.where(lane < 128, pre, _sigmoid(pre)))
    second = _dot(act, w2_ref[...])
    p_gla = project(_W_IN + 3 * rw, _W_IN + 3 * rw + 768)
    lw = -jnp.exp(-_softplus(-(vec[3:4] + second[:, 0:rw])) - 0.5)
    a = _sigmoid(vec[4:5] + second[:, rw:2 * rw])
    g = second[:, 2 * rw:3 * rw]

    p = p_rkv[8:]
    ps = prev_rows(p_rkv)
    r = p[:, 0:rw]
    r = r + (ps[:, 0:rw] - r) * vec[0:1]
    k = p[:, rw:2 * rw]
    k = k + (ps[:, rw:2 * rw] - k) * vec[1:2]
    v = p[:, 2 * rw:3 * rw]
    v = v + (ps[:, 2 * rw:3 * rw] - v) * vec[2:3]

    kk = k * vec[6:7]
    kk2 = kk * kk
    ss = jnp.concatenate(_group_sums([], obd_ref[...], coarse=[kk2[:, :GROUP_W], kk2[:, GROUP_W:]]), axis=1)
    kk = kk / jnp.maximum(jnp.sqrt(ss), 1e-12)
    k = k * (1.0 + (a - 1.0) * vec[7:8])
    misc = p_lora[8:, _W_MISC:_W_LORA_H]
    if has_vres:
        mv = jnp.where(_iota(misc.shape, 1) < 64, misc, prev_rows(p_lora[:, _W_MISC:_W_LORA_H]))
        v = v + (vf_ref[0] - v) * _sigmoid(vec[5:6] + _dot(mv, v2_ref[...]))
    p_ret = project(_W_IN + 3 * rw + 768, _WIN_W)

    rwp_ref[0] = r
    rwp_ref[1] = lw
    rwp_ref[2] = k
    rwp_ref[3] = v
    rwp_ref[4] = kk
    rwp_ref[5] = kk * a
    rwp_ref[6] = g

    p = p_gla[8:]
    gr_ref[:, _GQ:_GK] = p[:, 0:128] * (GLA_DK ** -0.5)
    gr_ref[:, _GK:_GLA] = p[:, 128:256]
    la_pre = _dot(misc, ga2_ref[...]) + gab_ref[...]
    gr_ref[:, _GLA:_GV] = -_softplus(-la_pre) * (1.0 / GLA_GATE_TAU)
    gr_ref[:, _GV:_GG] = p[:, 256:512]
    gate = p[:, 512:768]
    gr_ref[:, _GG:_RQ] = gate * _sigmoid(gate)

    p = p_ret[8:]
    cos = cos_ref[...]
    sin = sin_ref[...]
    lo_half = (_iota((tm, 128), 1) % RET_DK) < (RET_DK // 2)

    def rope(t):
        swapped = jnp.where(lo_half, pltpu.roll(t, 128 - RET_DK // 2, 1), pltpu.roll(t, RET_DK // 2, 1))
        return t * cos + swapped * sin

    gr_ref[:, _RQ:_RK] = rope(p[:, 0:128]) * (RET_DK ** -0.5)
    gr_ref[:, _RK:_RV] = rope(p[:, 128:256])
    gr_ref[:, _RV:_RG] = p[:, 256:512]
    gate = p[:, 512:768]
    gr_ref[:, _RG:_GR_W] = gate * _sigmoid(gate)


def _prep(x2d, mod, layer, seq, tm, has_vres, n1g, win, w2, ga2, gab, vec, obd, cos_t, sin_t,
          v2=None, rwp_first=None):
    T, D = x2d.shape
    tiles_per_seq = seq // tm
    n_tiles = T // tm
    per_layer = [n1g, win, w2, ga2, gab, vec]
    in_specs = [pl.BlockSpec((tm, D), lambda i: (i, 0)),
                pl.BlockSpec((8, D), lambda i: (jnp.maximum(i * (tm // 8) - 1, 0), 0)),
                pl.BlockSpec((None, 1, 6, D), lambda i: (layer, i // tiles_per_seq, 0, 0))]
    in_specs += [_layer_spec(a.shape, layer) for a in per_layer]
    in_specs += [_const_spec(obd.shape),
                 pl.BlockSpec((tm, 128), lambda i: (i % tiles_per_seq, 0)),
                 pl.BlockSpec((tm, 128), lambda i: (i % tiles_per_seq, 0))]
    args = [x2d, x2d, mod] + per_layer + [obd, cos_t, sin_t]
    if has_vres:
        in_specs += [_layer_spec(v2.shape, layer),
                     pl.BlockSpec((1, tm, RWKV_WIDTH), lambda i: (3, i, 0))]
        args += [v2, rwp_first]
    return pl.pallas_call(
        functools.partial(_prep_kernel, tiles_per_seq, has_vres),
        grid=(n_tiles,),
        in_specs=in_specs,
        out_specs=[pl.BlockSpec((7, tm, RWKV_WIDTH), lambda i: (0, i, 0)),
                   pl.BlockSpec((tm, _GR_W), lambda i: (i, 0))],
        out_shape=[jax.ShapeDtypeStruct((7, T, RWKV_WIDTH), F32),
                   jax.ShapeDtypeStruct((T, _GR_W), F32)],
        compiler_params=pltpu.CompilerParams(
            dimension_semantics=("parallel",), vmem_limit_bytes=VMEM_LIMIT_BYTES),
        name="proj_prep",
    )(*args)


def _chunk_masks(groups):
    C = CHUNK
    row = _iota((C, groups * C), 0)
    col = _iota((C, groups * C), 1) % C
    return row > col, row >= col, (row == col).astype(F32)


def _rwkv_chunk_scaled(r, lw, k, v, kk, kb):
    C = CHUNK
    cl = _cumsum_rows(lw)
    cle = cl[C - 1:C]
    e_neg = jnp.exp(-cl)
    e_end = jnp.exp(cle - cl)
    at = -kk * jnp.exp(cl - lw)
    rt = r * jnp.exp(cl)
    return dict(at=at, rt=rt, ar=jnp.concatenate([at, rt], axis=0), bt=kb * e_neg, kt=k * e_neg,
                bk_end=jnp.concatenate([kb * e_end, k * e_end], axis=0), g_end=jnp.exp(cle), v=v)


def _rwkv_chunk_factors(units, masks):
    C, G = CHUNK, HEADS_PER_GROUP
    strict, incl, ident = masks
    n = range(len(units))
    at, rt, ar, v = ([u[name] for u in units] for name in ("at", "rt", "ar", "v"))
    sb = [_bd_dot(ar[i], units[i]["bt"], G, nt=True) for i in n]
    sk = [_bd_dot(ar[i], units[i]["kt"], G, nt=True) for i in n]
    yield
    a_rb = [jnp.where(incl, sb[i][C:], 0.0) for i in n]
    a_ak = [jnp.where(strict, sk[i][:C], 0.0) for i in n]
    a_rk = [jnp.where(incl, sk[i][C:], 0.0) for i in n]

    a_ab = [jnp.where(strict, sb[i][:C], 0.0) for i in n]
    row = _iota((C, G * C), 0)
    col = _iota((C, G * C), 1) % C
    bb = [jnp.where(row // INV_BASE_BLOCK == col // INV_BASE_BLOCK, a_ab[i], 0.0) for i in n]
    tm = [ident + bb[i] for i in n]
    xp = [_bd_dot(bb[i], bb[i], G) for i in n]
    yield
    rr = [_bd_dot(jnp.concatenate([tm[i], xp[i]], axis=0), xp[i], G) for i in n]
    tm = [tm[i] + rr[i][:C] for i in n]
    yield
    tm = [tm[i] + _bd_dot(tm[i], rr[i][C:], G) for i in n]
    yield
    xs = []
    s = INV_BASE_BLOCK
    while s < C:
        sub = (row // (2 * s) == col // (2 * s)) & (row % (2 * s) >= s) & (col % (2 * s) < s)
        xs.append([jnp.where(sub, a_ab[i], 0.0) for i in n])
        s *= 2
    rr = [_bd_dot(jnp.concatenate([x[i] for x in xs], axis=0), tm[i], G) for i in n]
    yield
    xt = [[rr[i][k * C:(k + 1) * C] for i in n] for k in range(len(xs))]
    while xt:
        rr = [_bd_dot(jnp.concatenate([tm[i]] + [z[i] for z in xt[1:]], axis=0), xt[0][i], G) for i in n]
        yield
        tm = [tm[i] + rr[i][:C] for i in n]
        xt = [[z[i] + rr[i][(k + 1) * C:(k + 2) * C] for i in n] for k, z in enumerate(xt[1:])]

    vv = [_bd_dot(jnp.concatenate([a_ak[i], a_rk[i]], axis=0), v[i], G) for i in n]
    yield
    wt = [_bd_dot(tm[i], at[i], G) for i in n]
    yield
    ut = [_bd_dot(tm[i], vv[i][:C], G) for i in n]
    yield
    qh = [rt[i] + _bd_dot(a_rb[i], wt[i], G) for i in n]
    yield
    yh = [_bd_dot(a_rb[i], ut[i], G) + vv[i][C:] for i in n]
    yield
    zero = jnp.zeros((C, 128), F32)
    left = _iota((C, 128), 1) < HEAD_DV
    pm, zm = [[] for _ in n], [[] for _ in n]
    for s in range(GROUP_W // 128):
        sl = slice(128 * s, 128 * (s + 1))
        pz = [_dot_tn(units[i]["bk_end"][:, sl],
                      jnp.concatenate([jnp.concatenate([wt[i][:, sl], ut[i][:, sl]], axis=1),
                                       jnp.concatenate([zero, v[i][:, sl]], axis=1)], axis=0)) for i in n]
        for i in n:
            pm[i].append(jnp.where(left, pz[i][:C, :128], pz[i][C:, :128]))
            zm[i].append(jnp.where(left, pz[i][:C, 128:], pz[i][C:, 128:]))
        yield
    pm = [jnp.concatenate(pm[i], axis=1) + jnp.where(ident > 0, units[i]["g_end"], 0.0) for i in n]
    zm = [jnp.concatenate(zm[i], axis=1) for i in n]
    return qh, yh, pm, zm


def _interleave(main, side):
    side_done, side_out = False, None
    while True:
        try:
            next(main)
        except StopIteration as stop:
            main_out = stop.value
            break
        if not side_done:
            try:
                next(side)
            except StopIteration as stop:
                side_done, side_out = True, stop.value
    while not side_done:
        try:
            next(side)
        except StopIteration as stop:
            side_done, side_out = True, stop.value
    return main_out, side_out


def _rwkv_kernel(cb, rwp_ref, par_ref, o_ref, st_ref):
    C, W, N = CHUNK, GROUP_W, HEAD_DV
    ngroups = o_ref.shape[1] // W

    @pl.when(pl.program_id(1) == 0)
    def _():
        st_ref[...] = jnp.zeros_like(st_ref)

    masks = _chunk_masks(HEADS_PER_GROUP)
    ones_bd = _block_mask(W, W, N, N).astype(F32)
    m = [st_ref[gi] for gi in range(ngroups)]

    def wave_units(chunks):
        return [(slice(j * C, (j + 1) * C), slice(gi * W, (gi + 1) * W), gi)
                for j in chunks for gi in range(ngroups)]

    def factors(units):
        scaled = [_rwkv_chunk_scaled(*[rwp_ref[i, rows, lanes] for i in range(6)])
                  for rows, lanes, _ in units]
        return _rwkv_chunk_factors(scaled, masks)

    def chain(units, fac):
        qh, yh, pm, zm = fac
        y = []
        for u, (_, _, gi) in enumerate(units):
            ym = _dot(jnp.concatenate([qh[u], pm[u]], axis=0), _bd_rows(m[gi], HEADS_PER_GROUP))
            y.append(ym[:C] + yh[u])
            m[gi] = ym[C:] + zm[u]
            yield
        return y

    def epilogue(units, y):
        n = range(len(units))
        par = [par_ref[:, lanes] for _, lanes, _ in units]
        rkr = [rwp_ref[0, rows, lanes] * rwp_ref[2, rows, lanes] * par[u][0:1]
               for u, (rows, lanes, _) in enumerate(units)]
        sums = _group_sums(y + rkr, ones_bd)
        yield
        yc = [y[u] - sums[u] * (1.0 / N) for u in n]
        var = _group_sums([], ones_bd, coarse=[yc[u] * yc[u] for u in n])
        yield
        for u, (rows, lanes, _) in enumerate(units):
            yn = yc[u] * lax.rsqrt(var[u] * (1.0 / N) + RWKV_GN_EPS) * par[u][1:2] + par[u][2:3]
            o_ref[rows, lanes] = ((yn + sums[len(units) + u] * rwp_ref[3, rows, lanes])
                                  * rwp_ref[6, rows, lanes])
            yield

    def in_turn(*gens):
        outs = []
        for g in gens:
            outs.append((yield from g))
        return outs

    nw = RWKV_WAVES_PER_STEP
    per_wave = cb // nw
    units = [wave_units(range(w * per_wave, (w + 1) * per_wave)) for w in range(nw)]
    fac, ys = [None] * nw, [None] * nw
    for k in range(nw):
        side = []
        if k >= 1:
            side.append(chain(units[k - 1], fac[k - 1]))
        if k >= 2:
            side.append(epilogue(units[k - 2], ys[k - 2]))
        fac[k], outs = _interleave(factors(units[k]), in_turn(*side))
        if k >= 1:
            ys[k - 1] = outs[0]
    drain = epilogue(units[nw - 2], ys[nw - 2]) if nw >= 2 else iter(())
    _, ys[nw - 1] = _interleave(drain, chain(units[nw - 1], fac[nw - 1]))
    _interleave(epilogue(units[nw - 1], ys[nw - 1]), iter(()))
    for gi in range(ngroups):
        st_ref[gi] = m[gi]


def _rwkv(rwp, par, batch, seq, cb):
    _, T, RW = rwp.shape
    steps = seq // (CHUNK * cb)
    return pl.pallas_call(
        functools.partial(_rwkv_kernel, cb),
        grid=(batch, steps),
        in_specs=[pl.BlockSpec((7, cb * CHUNK, RW), lambda b, c: (0, b * steps + c, 0)),
                  _const_spec(par.shape)],
        out_specs=pl.BlockSpec((cb * CHUNK, RW), lambda b, c: (b * steps + c, 0)),
        out_shape=jax.ShapeDtypeStruct((T, RW), F32),
        scratch_shapes=[pltpu.VMEM((RW // GROUP_W, HEAD_DV, GROUP_W), F32)],
        compiler_params=pltpu.CompilerParams(
            dimension_semantics=("parallel", "arbitrary"), vmem_limit_bytes=VMEM_LIMIT_BYTES),
        name="rwkv7_chunk",
    )(rwp, par)


def _glaret_kernel(cb, gr_ref, lng_ref, dec_ref, kdec_ref, qdec_ref, cdec_ref, o_ref, sg_ref, sr_ref):
    C, G, DV = CHUNK, GLA_HEADS, HEAD_DV
    WV = G * DV

    @pl.when(pl.program_id(1) == 0)
    def _():
        sg_ref[...] = jnp.zeros_like(sg_ref)
        sr_ref[...] = jnp.zeros_like(sr_ref)

    _, incl, _ = _chunk_masks(G)
    ones_bd = _block_mask(WV, WV, DV, DV).astype(F32)
    st_mask = _block_mask(WV, G * GLA_DK, DV, GLA_DK)
    lng = lng_ref[...]
    dec = dec_ref[...]
    kdec = kdec_ref[...]
    qdec = qdec_ref[...]
    cdec = cdec_ref[...]

    n = range(cb)
    rows = [slice(j * C, (j + 1) * C) for j in n]
    gq = [gr_ref[rows[j], _GQ:_GK] for j in n]
    gk = [gr_ref[rows[j], _GK:_GLA] for j in n]
    gv = [gr_ref[rows[j], _GV:_GG] for j in n]
    bc = [_cumsum_rows(gr_ref[rows[j], _GLA:_GV]) for j in n]
    be = [bc[j][C - 1:C] for j in n]
    mid = [bc[j][C // 2 - 1:C // 2] for j in n]
    ep = [jnp.exp(bc[j] - mid[j]) for j in n]
    en = [jnp.exp(mid[j] - bc[j]) for j in n]
    qp = [gq[j] * jnp.exp(bc[j]) for j in n]
    att_lo = [_dot_nt(gq[j] * ep[j], _bd_rows(gk[j] * en[j], G)) for j in n]
    att_hi = [_dot_nt(gq[j] * en[j], _bd_rows(gk[j] * ep[j], G)) for j in n]
    rq = [gr_ref[rows[j], _RQ:_RK] for j in n]
    rk = [gr_ref[rows[j], _RK:_RV] for j in n]
    rv = [gr_ref[rows[j], _RV:_RG] for j in n]
    sc = [_dot_nt(rq[j], _bd_rows(rk[j], G)) * dec for j in n]
    g_intra = [_bd_dot(jnp.where(incl, att_lo[j], att_hi[j]), gv[j], G) for j in n]
    r_intra = [_bd_dot(sc[j], rv[j], G) for j in n]
    g_kv = [jnp.where(st_mask, _dot_tn(gv[j], gk[j] * jnp.exp(be[j] - bc[j])), 0.0) for j in n]
    r_kv = [jnp.where(st_mask, _dot_tn(rv[j], rk[j] * kdec), 0.0) for j in n]

    sg = sg_ref[...]
    sr = sr_ref[...]
    g_o = []
    r_o = []
    for j in n:
        g_o.append(g_intra[j] + _dot_nt(qp[j], sg))
        sg = sg * jnp.exp(be[j]) + g_kv[j]
        r_o.append(r_intra[j] + _dot_nt(rq[j] * qdec, sr))
        sr = sr * cdec + r_kv[j]
    sg_ref[...] = sg
    sr_ref[...] = sr

    sums = _group_sums(r_o, ones_bd, coarse=[g_o[j] * g_o[j] for j in n])
    r_c = [r_o[j] - sums[j] * (1.0 / DV) for j in n]
    r_var = _group_sums([], ones_bd, coarse=[r_c[j] * r_c[j] for j in n])
    for j in n:
        o_ref[rows[j], 0:WV] = (g_o[j] * lax.rsqrt(sums[cb + j] * (1.0 / DV) + EPS) * lng
                                * gr_ref[rows[j], _GG:_RQ])
        o_ref[rows[j], WV:2 * WV] = (r_c[j] * lax.rsqrt(r_var[j] * (1.0 / DV) + EPS)
                                     * gr_ref[rows[j], _RG:_GR_W])


def _glaret(gr, lng, dec, kdec, qdec, cdec, batch, seq, cb):
    T = gr.shape[0]
    nc = seq // (CHUNK * cb)
    wv = GLA_HEADS * HEAD_DV
    consts = [lng, dec, kdec, qdec, cdec]
    return pl.pallas_call(
        functools.partial(_glaret_kernel, cb),
        grid=(batch, nc),
        in_specs=[pl.BlockSpec((cb * CHUNK, _GR_W), lambda b, c: (b * nc + c, 0))]
                 + [_const_spec(a.shape) for a in consts],
        out_specs=pl.BlockSpec((cb * CHUNK, 2 * wv), lambda b, c: (b * nc + c, 0)),
        out_shape=jax.ShapeDtypeStruct((T, 2 * wv), F32),
        scratch_shapes=[pltpu.VMEM((wv, GLA_HEADS * GLA_DK), F32),
                        pltpu.VMEM((wv, RET_HEADS * RET_DK), F32)],
        compiler_params=pltpu.CompilerParams(
            dimension_semantics=("parallel", "arbitrary"), vmem_limit_bytes=VMEM_LIMIT_BYTES),
        name="gla_retention_chunk",
    )(gr, *consts)


def _outffn_kernel(final_norm, x_ref, ya_ref, ybc_ref, mod_ref, n2g_ref, wo_ref, wg_ref, wu_ref,
                   wd_ref, nfg_ref, o_ref):
    mod = mod_ref[0]
    y = jnp.concatenate([ya_ref[...], ybc_ref[...]], axis=-1)
    x = x_ref[...] + mod[2:3] * _dot(y, wo_ref[...])
    ms = jnp.mean(x * x, axis=-1, keepdims=True)
    h = (x * lax.rsqrt(ms + EPS) * (n2g_ref[...] * (1.0 + mod[4:5])) + mod[3:4]).astype(BF16)
    gate = jnp.dot(h, wg_ref[...], preferred_element_type=F32)
    up = jnp.dot(h, wu_ref[...], preferred_element_type=F32)
    x = x + mod[5:6] * _dot(gate * _sigmoid(gate) * up, wd_ref[...])
    if final_norm:
        ms = jnp.mean(x * x, axis=-1, keepdims=True)
        x = x * lax.rsqrt(ms + EPS) * nfg_ref[...]
    o_ref[...] = x


def _outffn(x2d, ya, ybc, mod, layer, seq, tm, final_norm, n2g, wo, wg, wu, wd, nfg):
    T, D = x2d.shape
    tiles_per_seq = seq // tm
    per_layer = [n2g, wo, wg, wu, wd]
    return pl.pallas_call(
        functools.partial(_outffn_kernel, final_norm),
        grid=(T // tm,),
        in_specs=[pl.BlockSpec((tm, D), lambda i: (i, 0)),
                  pl.BlockSpec((tm, ya.shape[1]), lambda i: (i, 0)),
                  pl.BlockSpec((tm, ybc.shape[1]), lambda i: (i, 0)),
                  pl.BlockSpec((None, 1, 6, D), lambda i: (layer, i // tiles_per_seq, 0, 0))]
                 + [_layer_spec(a.shape, layer) for a in per_layer] + [_const_spec(nfg.shape)],
        out_specs=pl.BlockSpec((tm, D), lambda i: (i, 0)),
        out_shape=jax.ShapeDtypeStruct((T, D), F32),
        compiler_params=pltpu.CompilerParams(
            dimension_semantics=("parallel",), vmem_limit_bytes=VMEM_LIMIT_BYTES),
        name="outproj_swiglu",
    )(x2d, ya, ybc, mod, *per_layer, nfg)


def _rope_tables(seq):
    half = RET_DK // 2
    inv_freq = ROPE_BASE ** (-jnp.arange(half, dtype=F32) / half)
    ang = jnp.arange(seq, dtype=F32)[:, None] * inv_freq[None, :]
    cos, sin = jnp.cos(ang), jnp.sin(ang)
    cos_t = jnp.tile(jnp.concatenate([cos, cos], axis=-1), (1, RET_HEADS))
    sin_t = jnp.tile(jnp.concatenate([-sin, sin], axis=-1), (1, RET_HEADS))
    return cos_t, sin_t


def _retention_tables():
    H, C = RET_HEADS, CHUNK
    log_gamma = jnp.log1p(-(2.0 ** (-5.0 - jnp.arange(H, dtype=F32))))
    pos = jnp.arange(C, dtype=F32)
    intra = jnp.exp(log_gamma[:, None, None] * jnp.abs(pos[:, None] - pos[None, :]))
    dec = jnp.transpose(intra, (1, 0, 2)).reshape(C, H * C)
    k_dec = jnp.exp(log_gamma[None, :] * (C - 1.0 - pos)[:, None])
    q_dec = jnp.exp(log_gamma[None, :] * (pos + 1.0)[:, None])
    chunk_dec = jnp.exp(log_gamma * C)
    kdec = jnp.repeat(k_dec, RET_DK, axis=1)
    qdec = jnp.repeat(q_dec, RET_DK, axis=1)
    cdec = jnp.repeat(chunk_dec, RET_DK)[None, :]
    return dec, kdec, qdec, cdec


def kernel(x, c, ada_w, ada_b, norm1_g, norm2_g, w_in, w_out, rk_mu_rkv, rk_mu_x, rk_w0, rk_w1, rk_w2, rk_a0, rk_a1, rk_a2, rk_g1, rk_g2, rk_k_k, rk_k_a, rk_r_k, rk_ln_g, rk_ln_b, rk_mu_v, rk_v0, rk_v1, rk_v2, gla_a1, gla_a2, gla_ab, gla_ln_g, ffn_w_gate, ffn_w_up, ffn_w_down, norm_f_g):
    B, S, D = x.shape
    L = ada_w.shape[0]
    T = B * S
    assert S % PREP_TOKENS_PER_STEP == 0 and S % FFN_TOKENS_PER_STEP == 0
    assert S % (CHUNK * RWKV_CHUNKS_PER_STEP) == 0 and S % (CHUNK * GLARET_CHUNKS_PER_STEP) == 0

    mod = _adaln(c, ada_w, ada_b).reshape(L, B, 6, D)
    cos_t, sin_t = _rope_tables(S)
    dec, kdec, qdec, cdec = _retention_tables()
    obd = _block_mask(GROUP_W, GROUP_W, HEAD_DV, HEAD_DV).astype(BF16)
    RW = RWKV_WIDTH

    def zeros(*shape):
        return jnp.zeros(shape, F32)

    mu_v = jnp.concatenate([zeros(1, D), rk_mu_v], axis=0)
    v0 = jnp.concatenate([zeros(1, RW), rk_v0], axis=0)
    v1 = jnp.concatenate([zeros(1, D, rk_v1.shape[2]), rk_v1], axis=0)
    v2 = jnp.concatenate([zeros(1, rk_v2.shape[1], RW), rk_v2], axis=0)

    def on_h(mu, w):
        return (1.0 - mu)[:, :, None] * w

    def on_prev(mu, w):
        return mu[:, :, None] * w

    mu = rk_mu_x
    misc = jnp.concatenate([gla_a1, zeros(L, D, 16), on_h(mu_v, v1), on_prev(mu_v, v1), zeros(L, D, 32)],
                           axis=2)
    lora_h = jnp.concatenate([on_h(mu[:, 0], rk_w1), on_h(mu[:, 1], rk_a1), on_h(mu[:, 2], rk_g1)], axis=2)
    lora_s = jnp.concatenate([on_prev(mu[:, 0], rk_w1), on_prev(mu[:, 1], rk_a1), on_prev(mu[:, 2], rk_g1)],
                             axis=2)
    win = jnp.concatenate([t.astype(BF16) for t in (misc, lora_h, lora_s, w_in)], axis=2)
    assert win.shape[2] == _WIN_W and misc.shape[2] == _W_LORA_H - _W_MISC
    w2 = jnp.concatenate([
        jnp.concatenate([rk_w2, zeros(L, 64, 2 * RW)], axis=2),
        jnp.concatenate([zeros(L, 64, RW), rk_a2, zeros(L, 64, RW)], axis=2),
        jnp.concatenate([zeros(L, 128, 2 * RW), rk_g2], axis=2)], axis=1).astype(BF16)
    ga2 = jnp.concatenate([gla_a2, zeros(L, 128 - gla_a2.shape[1], gla_a2.shape[2])], axis=1).astype(BF16)
    v2p = jnp.concatenate([zeros(L, 32, RW), v2, v2, zeros(L, 32, RW)], axis=1).astype(BF16)
    vec = jnp.stack([rk_mu_rkv[:, 0], rk_mu_rkv[:, 1], rk_mu_rkv[:, 2], rk_w0, rk_a0, v0, rk_k_k, rk_k_a],
                    axis=1)
    par = jnp.stack([rk_r_k.reshape(L, RW), rk_ln_g, rk_ln_b], axis=1)
    lng = jnp.tile(gla_ln_g, (1, GLA_HEADS))[:, None, :]
    wo, wg, wu, wd = (w.astype(BF16) for w in (w_out, ffn_w_gate, ffn_w_up, ffn_w_down))

    x2d = x.reshape(T, D)
    rwp_first = None
    for l in range(L):
        has_vres = l > 0
        extra = dict(v2=v2p, rwp_first=rwp_first) if has_vres else {}
        rwp, gr = _prep(x2d, mod, l, S, PREP_TOKENS_PER_STEP, has_vres, norm1_g[:, None, :], win, w2, ga2,
                        gla_ab[:, None, :], vec, obd, cos_t, sin_t, **extra)
        if l == 0:
            rwp_first = rwp
        ya = _rwkv(rwp, par[l], B, S, RWKV_CHUNKS_PER_STEP)
        ybc = _glaret(gr, lng[l], dec, kdec, qdec, cdec, B, S, GLARET_CHUNKS_PER_STEP)
        x2d = _outffn(x2d, ya, ybc, mod, l, S, FFN_TOKENS_PER_STEP, l == L - 1, norm2_g[:, None, :],
                      wo, wg, wu, wd, norm_f_g[None])
    return x2d.reshape(B, S, D)
```

```python
import functools

import numpy as np
import jax
import jax.numpy as jnp
from jax import lax
from jax.experimental import pallas as pl
from jax.experimental.pallas import tpu as pltpu

F32 = jnp.float32
BF16 = jnp.bfloat16

CHUNK = 64
EPS = 1e-6
HEAD_DV = 64
RWKV_HEADS = 8
RWKV_WIDTH = RWKV_HEADS * HEAD_DV
RWKV_GN_EPS = 64e-5
GLA_HEADS = 4
GLA_DK = 32
GLA_GATE_TAU = 16.0
RET_HEADS = 4
RET_DK = 32
ROPE_BASE = 10000.0
HEADS_PER_GROUP = 4
GROUP_W = HEADS_PER_GROUP * HEAD_DV
VMEM_LIMIT_BYTES = 56 * 1024 * 1024
RWKV_CHUNKS_PER_STEP = 16
INV_BASE_BLOCK = 8
RWKV_WAVES_PER_STEP = 4
GLARET_CHUNKS_PER_STEP = 16
PREP_TOKENS_PER_STEP = 512
FFN_TOKENS_PER_STEP = 512

_GQ, _GK, _GLA, _GV, _GG, _RQ, _RK, _RV, _RG, _GR_W = 0, 128, 256, 384, 640, 896, 1024, 1152, 1408, 1664
_W_MISC, _W_LORA_H, _W_LORA_S, _W_IN, _WIN_W = 0, 128, 384, 640, 3712


def _dot(a, b):
    return jnp.dot(a.astype(BF16), b.astype(BF16), preferred_element_type=F32)


def _dot_nt(a, b):
    return lax.dot_general(a.astype(BF16), b.astype(BF16), (((1,), (1,)), ((), ())),
                           preferred_element_type=F32)


def _dot_tn(a, b):
    return lax.dot_general(a.astype(BF16), b.astype(BF16), (((0,), (0,)), ((), ())),
                           preferred_element_type=F32)


def _group_sums(xs, ones_bd, coarse=()):
    hi = [x.astype(BF16) for x in xs]
    lo = [(x - h.astype(F32)).astype(BF16) for x, h in zip(xs, hi)]
    terms = hi + [x.astype(BF16) for x in coarse] + lo
    s = jnp.dot(jnp.concatenate(terms, axis=0), ones_bd.astype(BF16), preferred_element_type=F32)
    offs = np.cumsum([0] + [t.shape[0] for t in terms])
    nx, nc = len(xs), len(coarse)
    out = [s[offs[i]:offs[i + 1]] + s[offs[nx + nc + i]:offs[nx + nc + i + 1]] for i in range(nx)]
    return out + [s[offs[nx + i]:offs[nx + i + 1]] for i in range(nc)]


def _cumsum_rows(x):
    n = x.shape[0]
    row = _iota(x.shape, 0)
    s = 1
    while s < n:
        x = x + jnp.where(row >= s, pltpu.roll(x, s, 0), 0.0)
        s *= 2
    return x


def _sigmoid(x):
    return 1.0 / (1.0 + jnp.exp(-x))


def _softplus(x):
    return jnp.maximum(x, 0.0) + jnp.log(1.0 + jnp.exp(-jnp.abs(x)))


def _iota(shape, axis):
    return lax.broadcasted_iota(jnp.int32, shape, axis)


def _bd_rows(x, groups):
    c, w = x.shape
    n = w // groups
    t = jnp.concatenate([x] * groups, axis=0)
    keep = (_iota(t.shape, 0) // c) == (_iota(t.shape, 1) // n)
    return jnp.where(keep, t, 0.0)


def _bd_dot(x, y, groups, nt=False):
    e = _bd_rows(y, groups)
    return _dot_nt(x, e) if nt else _dot(x, e)


def _block_mask(rows, cols, rblk, cblk):
    return (_iota((rows, cols), 0) // rblk) == (_iota((rows, cols), 1) // cblk)


def _const_spec(shape):
    nd = len(shape)
    return pl.BlockSpec(shape, lambda *_: (0,) * nd, pipeline_mode=pl.Buffered(1))


def _layer_spec(shape, layer):
    nd = len(shape)
    return pl.BlockSpec((None,) + tuple(shape[1:]), lambda *_: (layer,) + (0,) * (nd - 1),
                        pipeline_mode=pl.Buffered(1))


def _adaln_kernel(c_ref, w_ref, b_ref, o_ref):
    c = c_ref[...]
    cond = c * _sigmoid(c)
    o_ref[0] = _dot(cond, w_ref[0]) + b_ref[0]


def _adaln(c, ada_w, ada_b):
    L, D, D6 = ada_w.shape
    B = c.shape[0]
    tn = 1536
    return pl.pallas_call(
        _adaln_kernel,
        grid=(L, D6 // tn),
        in_specs=[pl.BlockSpec((B, D), lambda l, j: (0, 0)),
                  pl.BlockSpec((1, D, tn), lambda l, j: (l, 0, j)),
                  pl.BlockSpec((1, 1, tn), lambda l, j: (l, 0, j))],
        out_specs=pl.BlockSpec((1, B, tn), lambda l, j: (l, 0, j)),
        out_shape=jax.ShapeDtypeStruct((L, B, D6), F32),
        compiler_params=pltpu.CompilerParams(
            dimension_semantics=("arbitrary", "arbitrary"), vmem_limit_bytes=VMEM_LIMIT_BYTES),
        name="adaln_mod",
    )(c, ada_w, ada_b.reshape(L, 1, D6))


def _prep_kernel(tiles_per_seq, has_vres, *refs):
    if has_vres:
        (x_ref, xh_ref, mod_ref, n1g_ref, win_ref, w2_ref, ga2_ref, gab_ref, vec_ref, obd_ref,
         cos_ref, sin_ref, v2_ref, vf_ref, rwp_ref, gr_ref) = refs
    else:
        (x_ref, xh_ref, mod_ref, n1g_ref, win_ref, w2_ref, ga2_ref, gab_ref, vec_ref, obd_ref,
         cos_ref, sin_ref, rwp_ref, gr_ref) = refs
    tm = x_ref.shape[0]
    rw = RWKV_WIDTH
    first = (pl.program_id(0) % tiles_per_seq) == 0

    xe = jnp.concatenate([xh_ref[...], x_ref[...]], axis=0)
    mod = mod_ref[0]
    ms = jnp.mean(xe * xe, axis=-1, keepdims=True)
    he = xe * lax.rsqrt(ms + EPS) * (n1g_ref[...] * (1.0 + mod[1:2])) + mod[0:1]
    he = jnp.concatenate([jnp.where(first, 0.0, he[:8]), he[8:]], axis=0)

    heb = he.astype(BF16)

    def project(lo, hi):
        return jnp.dot(heb, win_ref[:, lo:hi], preferred_element_type=F32)

    def prev_rows(t):
        return pltpu.roll(t, 1, 0)[8:]

    p_lora = project(_W_MISC, _W_IN)
    p_rkv = project(_W_IN, _W_IN + 3 * rw)
    vec = vec_ref[...]

    pre = p_lora[8:, _W_LORA_H:_W_LORA_S] + prev_rows(p_lora[:, _W_LORA_S:_W_IN])
    lane = _iota(pre.shape, 1)
    act = jnp.where(lane < 64, jnp.tanh(pre), jnp.where(lane < 128, pre, _sigmoid(pre)))
    second = _dot(act, w2_ref[...])
    p_gla = project(_W_IN + 3 * rw, _W_IN + 3 * rw + 768)
    lw = -jnp.exp(-_softplus(-(vec[3:4] + second[:, 0:rw])) - 0.5)
    a = _sigmoid(vec[4:5] + second[:, rw:2 * rw])
    g = second[:, 2 * rw:3 * rw]

    p = p_rkv[8:]
    ps = prev_rows(p_rkv)
    r = p[:, 0:rw]
    r = r + (ps[:, 0:rw] - r) * vec[0:1]
    k = p[:, rw:2 * rw]
    k = k + (ps[:, rw:2 * rw] - k) * vec[1:2]
    v = p[:, 2 * rw:3 * rw]
    v = v + (ps[:, 2 * rw:3 * rw] - v) * vec[2:3]

    kk = k * vec[6:7]
    kk2 = kk * kk
    ss = jnp.concatenate(_group_sums([], obd_ref[...], coarse=[kk2[:, :GROUP_W], kk2[:, GROUP_W:]]), axis=1)
    kk = kk / jnp.maximum(jnp.sqrt(ss), 1e-12)
    k = k * (1.0 + (a - 1.0) * vec[7:8])
    misc = p_lora[8:, _W_MISC:_W_LORA_H]
    if has_vres:
        mv = jnp.where(_iota(misc.shape, 1) < 64, misc, prev_rows(p_lora[:, _W_MISC:_W_LORA_H]))
        v = v + (vf_ref[0] - v) * _sigmoid(vec[5:6] + _dot(mv, v2_ref[...]))
    p_ret = project(_W_IN + 3 * rw + 768, _WIN_W)

    rwp_ref[0] = r
    rwp_ref[1] = lw
    rwp_ref[2] = k
    rwp_ref[3] = v
    rwp_ref[4] = kk
    rwp_ref[5] = kk * a
    rwp_ref[6] = g

    p = p_gla[8:]
    gr_ref[:, _GQ:_GK] = p[:, 0:128] * (GLA_DK ** -0.5)
    gr_ref[:, _GK:_GLA] = p[:, 128:256]
    la_pre = _dot(misc, ga2_ref[...]) + gab_ref[...]
    gr_ref[:, _GLA:_GV] = -_softplus(-la_pre) * (1.0 / GLA_GATE_TAU)
    gr_ref[:, _GV:_GG] = p[:, 256:512]
    gate = p[:, 512:768]
    gr_ref[:, _GG:_RQ] = gate * _sigmoid(gate)

    p = p_ret[8:]
    cos = cos_ref[...]
    sin = sin_ref[...]
    lo_half = (_iota((tm, 128), 1) % RET_DK) < (RET_DK // 2)

    def rope(t):
        swapped = jnp.where(lo_half, pltpu.roll(t, 128 - RET_DK // 2, 1), pltpu.roll(t, RET_DK // 2, 1))
        return t * cos + swapped * sin

    gr_ref[:, _RQ:_RK] = rope(p[:, 0:128]) * (RET_DK ** -0.5)
    gr_ref[:, _RK:_RV] = rope(p[:, 128:256])
    gr_ref[:, _RV:_RG] = p[:, 256:512]
    gate = p[:, 512:768]
    gr_ref[:, _RG:_GR_W] = gate * _sigmoid(gate)


def _prep(x2d, mod, layer, seq, tm, has_vres, n1g, win, w2, ga2, gab, vec, obd, cos_t, sin_t,
          v2=None, rwp_first=None):
    T, D = x2d.shape
    tiles_per_seq = seq // tm
    n_tiles = T // tm
    per_layer = [n1g, win, w2, ga2, gab, vec]
    in_specs = [pl.BlockSpec((tm, D), lambda i: (i, 0)),
                pl.BlockSpec((8, D), lambda i: (jnp.maximum(i * (tm // 8) - 1, 0), 0)),
                pl.BlockSpec((None, 1, 6, D), lambda i: (layer, i // tiles_per_seq, 0, 0))]
    in_specs += [_layer_spec(a.shape, layer) for a in per_layer]
    in_specs += [_const_spec(obd.shape),
                 pl.BlockSpec((tm, 128), lambda i: (i % tiles_per_seq, 0)),
                 pl.BlockSpec((tm, 128), lambda i: (i % tiles_per_seq, 0))]
    args = [x2d, x2d, mod] + per_layer + [obd, cos_t, sin_t]
    if has_vres:
        in_specs += [_layer_spec(v2.shape, layer),
                     pl.BlockSpec((1, tm, RWKV_WIDTH), lambda i: (3, i, 0))]
        args += [v2, rwp_first]
    return pl.pallas_call(
        functools.partial(_prep_kernel, tiles_per_seq, has_vres),
        grid=(n_tiles,),
        in_specs=in_specs,
        out_specs=[pl.BlockSpec((7, tm, RWKV_WIDTH), lambda i: (0, i, 0)),
                   pl.BlockSpec((tm, _GR_W), lambda i: (i, 0))],
        out_shape=[jax.ShapeDtypeStruct((7, T, RWKV_WIDTH), F32),
                   jax.ShapeDtypeStruct((T, _GR_W), F32)],
        compiler_params=pltpu.CompilerParams(
            dimension_semantics=("parallel",), vmem_limit_bytes=VMEM_LIMIT_BYTES),
        name="proj_prep",
    )(*args)


def _chunk_masks(groups):
    C = CHUNK
    row = _iota((C, groups * C), 0)
    col = _iota((C, groups * C), 1) % C
    return row > col, row >= col, (row == col).astype(F32)


def _rwkv_chunk_scaled(r, lw, k, v, kk, kb):
    C = CHUNK
    cl = _cumsum_rows(lw)
    cle = cl[C - 1:C]
    e_neg = jnp.exp(-cl)
    e_end = jnp.exp(cle - cl)
    at = -kk * jnp.exp(cl - lw)
    rt = r * jnp.exp(cl)
    return dict(at=at, rt=rt, ar=jnp.concatenate([at, rt], axis=0), bt=kb * e_neg, kt=k * e_neg,
                bk_end=jnp.concatenate([kb * e_end, k * e_end], axis=0), g_end=jnp.exp(cle), v=v)


def _rwkv_chunk_factors(units, masks):
    C, G = CHUNK, HEADS_PER_GROUP
    strict, incl, ident = masks
    n = range(len(units))
    at, rt, ar, v = ([u[name] for u in units] for name in ("at", "rt", "ar", "v"))
    sb = [_bd_dot(ar[i], units[i]["bt"], G, nt=True) for i in n]
    sk = [_bd_dot(ar[i], units[i]["kt"], G, nt=True) for i in n]
    yield
    a_rb = [jnp.where(incl, sb[i][C:], 0.0) for i in n]
    a_ak = [jnp.where(strict, sk[i][:C], 0.0) for i in n]
    a_rk = [jnp.where(incl, sk[i][C:], 0.0) for i in n]

    a_ab = [jnp.where(strict, sb[i][:C], 0.0) for i in n]
    row = _iota((C, G * C), 0)
    col = _iota((C, G * C), 1) % C
    bb = [jnp.where(row // INV_BASE_BLOCK == col // INV_BASE_BLOCK, a_ab[i], 0.0) for i in n]
    tm = [ident + bb[i] for i in n]
    xp = [_bd_dot(bb[i], bb[i], G) for i in n]
    yield
    rr = [_bd_dot(jnp.concatenate([tm[i], xp[i]], axis=0), xp[i], G) for i in n]
    tm = [tm[i] + rr[i][:C] for i in n]
    yield
    tm = [tm[i] + _bd_dot(tm[i], rr[i][C:], G) for i in n]
    yield
    xs = []
    s = INV_BASE_BLOCK
    while s < C:
        sub = (row // (2 * s) == col // (2 * s)) & (row % (2 * s) >= s) & (col % (2 * s) < s)
        xs.append([jnp.where(sub, a_ab[i], 0.0) for i in n])
        s *= 2
    rr = [_bd_dot(jnp.concatenate([x[i] for x in xs], axis=0), tm[i], G) for i in n]
    yield
    xt = [[rr[i][k * C:(k + 1) * C] for i in n] for k in range(len(xs))]
    while xt:
        rr = [_bd_dot(jnp.concatenate([tm[i]] + [z[i] for z in xt[1:]], axis=0), xt[0][i], G) for i in n]
        yield
        tm = [tm[i] + rr[i][:C] for i in n]
        xt = [[z[i] + rr[i][(k + 1) * C:(k + 2) * C] for i in n] for k, z in enumerate(xt[1:])]

    vv = [_bd_dot(jnp.concatenate([a_ak[i], a_rk[i]], axis=0), v[i], G) for i in n]
    yield
    tg = [_bd_dot(a_rb[i], tm[i], G) for i in n]
    yield
    tg = [jnp.concatenate([tm[i], tg[i]], axis=0) for i in n]
    wq = [_bd_dot(tg[i], at[i], G) for i in n]
    yield
    uy = [_bd_dot(tg[i], vv[i][:C], G) for i in n]
    yield
    wt = [wq[i][:C] for i in n]
    ut = [uy[i][:C] for i in n]
    qh = [rt[i] + wq[i][C:] for i in n]
    yh = [uy[i][C:] + vv[i][C:] for i in n]
    zero = jnp.zeros((C, 128), F32)
    left = _iota((C, 128), 1) < HEAD_DV
    pm, zm = [[] for _ in n], [[] for _ in n]
    for s in range(GROUP_W // 128):
        sl = slice(128 * s, 128 * (s + 1))
        pz = [_dot_tn(units[i]["bk_end"][:, sl],
                      jnp.concatenate([jnp.concatenate([wt[i][:, sl], ut[i][:, sl]], axis=1),
                                       jnp.concatenate([zero, v[i][:, sl]], axis=1)], axis=0)) for i in n]
        for i in n:
            pm[i].append(jnp.where(left, pz[i][:C, :128], pz[i][C:, :128]))
            zm[i].append(jnp.where(left, pz[i][:C, 128:], pz[i][C:, 128:]))
        yield
    pm = [jnp.concatenate(pm[i], axis=1) + jnp.where(ident > 0, units[i]["g_end"], 0.0) for i in n]
    zm = [jnp.concatenate(zm[i], axis=1) for i in n]
    return qh, yh, pm, zm


def _interleave(main, side):
    side_done, side_out = False, None
    while True:
        try:
            next(main)
        except StopIteration as stop:
            main_out = stop.value
            break
        if not side_done:
            try:
                next(side)
            except StopIteration as stop:
                side_done, side_out = True, stop.value
    while not side_done:
        try:
            next(side)
        except StopIteration as stop:
            side_done, side_out = True, stop.value
    return main_out, side_out


def _rwkv_kernel(cb, rwp_ref, par_ref, o_ref, st_ref):
    C, W, N = CHUNK, GROUP_W, HEAD_DV
    ngroups = o_ref.shape[1] // W

    @pl.when(pl.program_id(1) == 0)
    def _():
        st_ref[...] = jnp.zeros_like(st_ref)

    masks = _chunk_masks(HEADS_PER_GROUP)
    ones_bd = _block_mask(W, W, N, N).astype(F32)
    m = [st_ref[gi] for gi in range(ngroups)]

    def wave_units(chunks):
        return [(slice(j * C, (j + 1) * C), slice(gi * W, (gi + 1) * W), gi)
                for j in chunks for gi in range(ngroups)]

    def factors(units):
        scaled = [_rwkv_chunk_scaled(*[rwp_ref[i, rows, lanes] for i in range(6)])
                  for rows, lanes, _ in units]
        return _rwkv_chunk_factors(scaled, masks)

    def chain(units, fac):
        qh, yh, pm, zm = fac
        y = []
        for u, (_, _, gi) in enumerate(units):
            ym = _dot(jnp.concatenate([qh[u], pm[u]], axis=0), _bd_rows(m[gi], HEADS_PER_GROUP))
            y.append(ym[:C] + yh[u])
            m[gi] = ym[C:] + zm[u]
            yield
        return y

    def epilogue(units, y):
        n = range(len(units))
        par = [par_ref[:, lanes] for _, lanes, _ in units]
        rkr = [rwp_ref[0, rows, lanes] * rwp_ref[2, rows, lanes] * par[u][0:1]
               for u, (rows, lanes, _) in enumerate(units)]
        sums = _group_sums(y + rkr, ones_bd)
        yield
        yc = [y[u] - sums[u] * (1.0 / N) for u in n]
        var = _group_sums([], ones_bd, coarse=[yc[u] * yc[u] for u in n])
        yield
        for u, (rows, lanes, _) in enumerate(units):
            yn = yc[u] * lax.rsqrt(var[u] * (1.0 / N) + RWKV_GN_EPS) * par[u][1:2] + par[u][2:3]
            o_ref[rows, lanes] = ((yn + sums[len(units) + u] * rwp_ref[3, rows, lanes])
                                  * rwp_ref[6, rows, lanes])
            yield

    def in_turn(*gens):
        outs = []
        for g in gens:
            outs.append((yield from g))
        return outs

    nw = RWKV_WAVES_PER_STEP
    per_wave = cb // nw
    units = [wave_units(range(w * per_wave, (w + 1) * per_wave)) for w in range(nw)]
    fac, ys = [None] * nw, [None] * nw
    for k in range(nw):
        side = []
        if k >= 1:
            side.append(chain(units[k - 1], fac[k - 1]))
        if k >= 2:
            side.append(epilogue(units[k - 2], ys[k - 2]))
        fac[k], outs = _interleave(factors(units[k]), in_turn(*side))
        if k >= 1:
            ys[k - 1] = outs[0]
    drain = epilogue(units[nw - 2], ys[nw - 2]) if nw >= 2 else iter(())
    _, ys[nw - 1] = _interleave(drain, chain(units[nw - 1], fac[nw - 1]))
    _interleave(epilogue(units[nw - 1], ys[nw - 1]), iter(()))
    for gi in range(ngroups):
        st_ref[gi] = m[gi]


def _rwkv(rwp, par, batch, seq, cb):
    _, T, RW = rwp.shape
    steps = seq // (CHUNK * cb)
    return pl.pallas_call(
        functools.partial(_rwkv_kernel, cb),
        grid=(batch, steps),
        in_specs=[pl.BlockSpec((7, cb * CHUNK, RW), lambda b, c: (0, b * steps + c, 0)),
                  _const_spec(par.shape)],
        out_specs=pl.BlockSpec((cb * CHUNK, RW), lambda b, c: (b * steps + c, 0)),
        out_shape=jax.ShapeDtypeStruct((T, RW), F32),
        scratch_shapes=[pltpu.VMEM((RW // GROUP_W, HEAD_DV, GROUP_W), F32)],
        compiler_params=pltpu.CompilerParams(
            dimension_semantics=("parallel", "arbitrary"), vmem_limit_bytes=VMEM_LIMIT_BYTES),
        name="rwkv7_chunk",
    )(rwp, par)


def _glaret_kernel(cb, gr_ref, lng_ref, dec_ref, kdec_ref, qdec_ref, cdec_ref, o_ref, sg_ref, sr_ref):
    C, G, DV = CHUNK, GLA_HEADS, HEAD_DV
    WV = G * DV

    @pl.when(pl.program_id(1) == 0)
    def _():
        sg_ref[...] = jnp.zeros_like(sg_ref)
        sr_ref[...] = jnp.zeros_like(sr_ref)

    _, incl, _ = _chunk_masks(G)
    ones_bd = _block_mask(WV, WV, DV, DV).astype(F32)
    st_mask = _block_mask(WV, G * GLA_DK, DV, GLA_DK)
    lng = lng_ref[...]
    dec = dec_ref[...]
    kdec = kdec_ref[...]
    qdec = qdec_ref[...]
    cdec = cdec_ref[...]

    n = range(cb)
    rows = [slice(j * C, (j + 1) * C) for j in n]
    gq = [gr_ref[rows[j], _GQ:_GK] for j in n]
    gk = [gr_ref[rows[j], _GK:_GLA] for j in n]
    gv = [gr_ref[rows[j], _GV:_GG] for j in n]
    bc = [_cumsum_rows(gr_ref[rows[j], _GLA:_GV]) for j in n]
    be = [bc[j][C - 1:C] for j in n]
    mid = [bc[j][C // 2 - 1:C // 2] for j in n]
    ep = [jnp.exp(bc[j] - mid[j]) for j in n]
    en = [jnp.exp(mid[j] - bc[j]) for j in n]
    qp = [gq[j] * jnp.exp(bc[j]) for j in n]
    att_lo = [_dot_nt(gq[j] * ep[j], _bd_rows(gk[j] * en[j], G)) for j in n]
    att_hi = [_dot_nt(gq[j] * en[j], _bd_rows(gk[j] * ep[j], G)) for j in n]
    rq = [gr_ref[rows[j], _RQ:_RK] for j in n]
    rk = [gr_ref[rows[j], _RK:_RV] for j in n]
    rv = [gr_ref[rows[j], _RV:_RG] for j in n]
    sc = [_dot_nt(rq[j], _bd_rows(rk[j], G)) * dec for j in n]
    g_intra = [_bd_dot(jnp.where(incl, att_lo[j], att_hi[j]), gv[j], G) for j in n]
    r_intra = [_bd_dot(sc[j], rv[j], G) for j in n]
    g_kv = [jnp.where(st_mask, _dot_tn(gv[j], gk[j] * jnp.exp(be[j] - bc[j])), 0.0) for j in n]
    r_kv = [jnp.where(st_mask, _dot_tn(rv[j], rk[j] * kdec), 0.0) for j in n]

    sg = sg_ref[...]
    sr = sr_ref[...]
    g_o = []
    r_o = []
    for j in n:
        g_o.append(g_intra[j] + _dot_nt(qp[j], sg))
        sg = sg * jnp.exp(be[j]) + g_kv[j]
        r_o.append(r_intra[j] + _dot_nt(rq[j] * qdec, sr))
        sr = sr * cdec + r_kv[j]
    sg_ref[...] = sg
    sr_ref[...] = sr

    sums = _group_sums(r_o, ones_bd, coarse=[g_o[j] * g_o[j] for j in n])
    r_c = [r_o[j] - sums[j] * (1.0 / DV) for j in n]
    r_var = _group_sums([], ones_bd, coarse=[r_c[j] * r_c[j] for j in n])
    for j in n:
        o_ref[rows[j], 0:WV] = (g_o[j] * lax.rsqrt(sums[cb + j] * (1.0 / DV) + EPS) * lng
                                * gr_ref[rows[j], _GG:_RQ])
        o_ref[rows[j], WV:2 * WV] = (r_c[j] * lax.rsqrt(r_var[j] * (1.0 / DV) + EPS)
                                     * gr_ref[rows[j], _RG:_GR_W])


def _glaret(gr, lng, dec, kdec, qdec, cdec, batch, seq, cb):
    T = gr.shape[0]
    nc = seq // (CHUNK * cb)
    wv = GLA_HEADS * HEAD_DV
    consts = [lng, dec, kdec, qdec, cdec]
    return pl.pallas_call(
        functools.partial(_glaret_kernel, cb),
        grid=(batch, nc),
        in_specs=[pl.BlockSpec((cb * CHUNK, _GR_W), lambda b, c: (b * nc + c, 0))]
                 + [_const_spec(a.shape) for a in consts],
        out_specs=pl.BlockSpec((cb * CHUNK, 2 * wv), lambda b, c: (b * nc + c, 0)),
        out_shape=jax.ShapeDtypeStruct((T, 2 * wv), F32),
        scratch_shapes=[pltpu.VMEM((wv, GLA_HEADS * GLA_DK), F32),
                        pltpu.VMEM((wv, RET_HEADS * RET_DK), F32)],
        compiler_params=pltpu.CompilerParams(
            dimension_semantics=("parallel", "arbitrary"), vmem_limit_bytes=VMEM_LIMIT_BYTES),
        name="gla_retention_chunk",
    )(gr, *consts)


def _outffn_kernel(final_norm, x_ref, ya_ref, ybc_ref, mod_ref, n2g_ref, wo_ref, wg_ref, wu_ref,
                   wd_ref, nfg_ref, o_ref):
    mod = mod_ref[0]
    y = jnp.concatenate([ya_ref[...], ybc_ref[...]], axis=-1)
    x = x_ref[...] + mod[2:3] * _dot(y, wo_ref[...])
    ms = jnp.mean(x * x, axis=-1, keepdims=True)
    h = (x * lax.rsqrt(ms + EPS) * (n2g_ref[...] * (1.0 + mod[4:5])) + mod[3:4]).astype(BF16)
    gate = jnp.dot(h, wg_ref[...], preferred_element_type=F32)
    up = jnp.dot(h, wu_ref[...], preferred_element_type=F32)
    x = x + mod[5:6] * _dot(gate * _sigmoid(gate) * up, wd_ref[...])
    if final_norm:
        ms = jnp.mean(x * x, axis=-1, keepdims=True)
        x = x * lax.rsqrt(ms + EPS) * nfg_ref[...]
    o_ref[...] = x


def _outffn(x2d, ya, ybc, mod, layer, seq, tm, final_norm, n2g, wo, wg, wu, wd, nfg):
    T, D = x2d.shape
    tiles_per_seq = seq // tm
    per_layer = [n2g, wo, wg, wu, wd]
    return pl.pallas_call(
        functools.partial(_outffn_kernel, final_norm),
        grid=(T // tm,),
        in_specs=[pl.BlockSpec((tm, D), lambda i: (i, 0)),
                  pl.BlockSpec((tm, ya.shape[1]), lambda i: (i, 0)),
                  pl.BlockSpec((tm, ybc.shape[1]), lambda i: (i, 0)),
                  pl.BlockSpec((None, 1, 6, D), lambda i: (layer, i // tiles_per_seq, 0, 0))]
                 + [_layer_spec(a.shape, layer) for a in per_layer] + [_const_spec(nfg.shape)],
        out_specs=pl.BlockSpec((tm, D), lambda i: (i, 0)),
        out_shape=jax.ShapeDtypeStruct((T, D), F32),
        compiler_params=pltpu.CompilerParams(
            dimension_semantics=("parallel",), vmem_limit_bytes=VMEM_LIMIT_BYTES),
        name="outproj_swiglu",
    )(x2d, ya, ybc, mod, *per_layer, nfg)


def _rope_tables(seq):
    half = RET_DK // 2
    inv_freq = ROPE_BASE ** (-jnp.arange(half, dtype=F32) / half)
    ang = jnp.arange(seq, dtype=F32)[:, None] * inv_freq[None, :]
    cos, sin = jnp.cos(ang), jnp.sin(ang)
    cos_t = jnp.tile(jnp.concatenate([cos, cos], axis=-1), (1, RET_HEADS))
    sin_t = jnp.tile(jnp.concatenate([-sin, sin], axis=-1), (1, RET_HEADS))
    return cos_t, sin_t


def _retention_tables():
    H, C = RET_HEADS, CHUNK
    log_gamma = jnp.log1p(-(2.0 ** (-5.0 - jnp.arange(H, dtype=F32))))
    pos = jnp.arange(C, dtype=F32)
    intra = jnp.exp(log_gamma[:, None, None] * jnp.abs(pos[:, None] - pos[None, :]))
    dec = jnp.transpose(intra, (1, 0, 2)).reshape(C, H * C)
    k_dec = jnp.exp(log_gamma[None, :] * (C - 1.0 - pos)[:, None])
    q_dec = jnp.exp(log_gamma[None, :] * (pos + 1.0)[:, None])
    chunk_dec = jnp.exp(log_gamma * C)
    kdec = jnp.repeat(k_dec, RET_DK, axis=1)
    qdec = jnp.repeat(q_dec, RET_DK, axis=1)
    cdec = jnp.repeat(chunk_dec, RET_DK)[None, :]
    return dec, kdec, qdec, cdec


def kernel(x, c, ada_w, ada_b, norm1_g, norm2_g, w_in, w_out, rk_mu_rkv, rk_mu_x, rk_w0, rk_w1, rk_w2, rk_a0, rk_a1, rk_a2, rk_g1, rk_g2, rk_k_k, rk_k_a, rk_r_k, rk_ln_g, rk_ln_b, rk_mu_v, rk_v0, rk_v1, rk_v2, gla_a1, gla_a2, gla_ab, gla_ln_g, ffn_w_gate, ffn_w_up, ffn_w_down, norm_f_g):
    B, S, D = x.shape
    L = ada_w.shape[0]
    T = B * S
    assert S % PREP_TOKENS_PER_STEP == 0 and S % FFN_TOKENS_PER_STEP == 0
    assert S % (CHUNK * RWKV_CHUNKS_PER_STEP) == 0 and S % (CHUNK * GLARET_CHUNKS_PER_STEP) == 0

    mod = _adaln(c, ada_w, ada_b).reshape(L, B, 6, D)
    cos_t, sin_t = _rope_tables(S)
    dec, kdec, qdec, cdec = _retention_tables()
    obd = _block_mask(GROUP_W, GROUP_W, HEAD_DV, HEAD_DV).astype(BF16)
    RW = RWKV_WIDTH

    def zeros(*shape):
        return jnp.zeros(shape, F32)

    mu_v = jnp.concatenate([zeros(1, D), rk_mu_v], axis=0)
    v0 = jnp.concatenate([zeros(1, RW), rk_v0], axis=0)
    v1 = jnp.concatenate([zeros(1, D, rk_v1.shape[2]), rk_v1], axis=0)
    v2 = jnp.concatenate([zeros(1, rk_v2.shape[1], RW), rk_v2], axis=0)

    def on_h(mu, w):
        return (1.0 - mu)[:, :, None] * w

    def on_prev(mu, w):
        return mu[:, :, None] * w

    mu = rk_mu_x
    misc = jnp.concatenate([gla_a1, zeros(L, D, 16), on_h(mu_v, v1), on_prev(mu_v, v1), zeros(L, D, 32)],
                           axis=2)
    lora_h = jnp.concatenate([on_h(mu[:, 0], rk_w1), on_h(mu[:, 1], rk_a1), on_h(mu[:, 2], rk_g1)], axis=2)
    lora_s = jnp.concatenate([on_prev(mu[:, 0], rk_w1), on_prev(mu[:, 1], rk_a1), on_prev(mu[:, 2], rk_g1)],
                             axis=2)
    win = jnp.concatenate([t.astype(BF16) for t in (misc, lora_h, lora_s, w_in)], axis=2)
    assert win.shape[2] == _WIN_W and misc.shape[2] == _W_LORA_H - _W_MISC
    w2 = jnp.concatenate([
        jnp.concatenate([rk_w2, zeros(L, 64, 2 * RW)], axis=2),
        jnp.concatenate([zeros(L, 64, RW), rk_a2, zeros(L, 64, RW)], axis=2),
        jnp.concatenate([zeros(L, 128, 2 * RW), rk_g2], axis=2)], axis=1).astype(BF16)
    ga2 = jnp.concatenate([gla_a2, zeros(L, 128 - gla_a2.shape[1], gla_a2.shape[2])], axis=1).astype(BF16)
    v2p = jnp.concatenate([zeros(L, 32, RW), v2, v2, zeros(L, 32, RW)], axis=1).astype(BF16)
    vec = jnp.stack([rk_mu_rkv[:, 0], rk_mu_rkv[:, 1], rk_mu_rkv[:, 2], rk_w0, rk_a0, v0, rk_k_k, rk_k_a],
                    axis=1)
    par = jnp.stack([rk_r_k.reshape(L, RW), rk_ln_g, rk_ln_b], axis=1)
    lng = jnp.tile(gla_ln_g, (1, GLA_HEADS))[:, None, :]
    wo, wg, wu, wd = (w.astype(BF16) for w in (w_out, ffn_w_gate, ffn_w_up, ffn_w_down))

    x2d = x.reshape(T, D)
    rwp_first = None
    for l in range(L):
        has_vres = l > 0
        extra = dict(v2=v2p, rwp_first=rwp_first) if has_vres else {}
        rwp, gr = _prep(x2d, mod, l, S, PREP_TOKENS_PER_STEP, has_vres, norm1_g[:, None, :], win, w2, ga2,
                        gla_ab[:, None, :], vec, obd, cos_t, sin_t, **extra)
        if l == 0:
            rwp_first = rwp
        ya = _rwkv(rwp, par[l], B, S, RWKV_CHUNKS_PER_STEP)
        ybc = _glaret(gr, lng[l], dec, kdec, qdec, cdec, B, S, GLARET_CHUNKS_PER_STEP)
        x2d = _outffn(x2d, ya, ybc, mod, l, S, FFN_TOKENS_PER_STEP, l == L - 1, norm2_g[:, None, :],
                      wo, wg, wu, wd, norm_f_g[None])
    return x2d.reshape(B, S, D)
```

```python
import functools

import numpy as np
import jax
import jax.numpy as jnp
from jax import lax
from jax.experimental import pallas as pl
from jax.experimental.pallas import tpu as pltpu

F32 = jnp.float32
BF16 = jnp.bfloat16

CHUNK = 64
EPS = 1e-6
HEAD_DV = 64
RWKV_HEADS = 8
RWKV_WIDTH = RWKV_HEADS * HEAD_DV
RWKV_GN_EPS = 64e-5
GLA_HEADS = 4
GLA_DK = 32
GLA_GATE_TAU = 16.0
RET_HEADS = 4
RET_DK = 32
ROPE_BASE = 10000.0
HEADS_PER_GROUP = 4
GROUP_W = HEADS_PER_GROUP * HEAD_DV
VMEM_LIMIT_BYTES = 56 * 1024 * 1024
RWKV_CHUNKS_PER_STEP = 16
INV_BASE_BLOCK = 8
RWKV_WAVES_PER_STEP = 4
GLARET_CHUNKS_PER_STEP = 16
PREP_TOKENS_PER_STEP = 512
FFN_TOKENS_PER_STEP = 512

_GQ, _GK, _GLA, _GV, _GG, _RQ, _RK, _RV, _RG, _GR_W = 0, 128, 256, 384, 640, 896, 1024, 1152, 1408, 1664
_W_MISC, _W_LORA_H, _W_LORA_S, _W_IN, _WIN_W = 0, 128, 384, 640, 3712


def _dot(a, b):
    return jnp.dot(a.astype(BF16), b.astype(BF16), preferred_element_type=F32)


def _dot_nt(a, b):
    return lax.dot_general(a.astype(BF16), b.astype(BF16), (((1,), (1,)), ((), ())),
                           preferred_element_type=F32)


def _dot_tn(a, b):
    return lax.dot_general(a.astype(BF16), b.astype(BF16), (((0,), (0,)), ((), ())),
                           preferred_element_type=F32)


def _group_sums(xs, ones_bd, coarse=()):
    hi = [x.astype(BF16) for x in xs]
    lo = [(x - h.astype(F32)).astype(BF16) for x, h in zip(xs, hi)]
    terms = hi + [x.astype(BF16) for x in coarse] + lo
    s = jnp.dot(jnp.concatenate(terms, axis=0), ones_bd.astype(BF16), preferred_element_type=F32)
    offs = np.cumsum([0] + [t.shape[0] for t in terms])
    nx, nc = len(xs), len(coarse)
    out = [s[offs[i]:offs[i + 1]] + s[offs[nx + nc + i]:offs[nx + nc + i + 1]] for i in range(nx)]
    return out + [s[offs[nx + i]:offs[nx + i + 1]] for i in range(nc)]


def _cumsum_rows(x):
    n = x.shape[0]
    row = _iota(x.shape, 0)
    s = 1
    while s < n:
        x = x + jnp.where(row >= s, pltpu.roll(x, s, 0), 0.0)
        s *= 2
    return x


def _sigmoid(x):
    return 1.0 / (1.0 + jnp.exp(-x))


def _softplus(x):
    return jnp.maximum(x, 0.0) + jnp.log(1.0 + jnp.exp(-jnp.abs(x)))


def _iota(shape, axis):
    return lax.broadcasted_iota(jnp.int32, shape, axis)


def _bd_rows(x, groups):
    c, w = x.shape
    n = w // groups
    t = jnp.concatenate([x] * groups, axis=0)
    keep = (_iota(t.shape, 0) // c) == (_iota(t.shape, 1) // n)
    return jnp.where(keep, t, 0.0)


def _bd_dot(x, y, groups, nt=False):
    e = _bd_rows(y, groups)
    return _dot_nt(x, e) if nt else _dot(x, e)


def _block_mask(rows, cols, rblk, cblk):
    return (_iota((rows, cols), 0) // rblk) == (_iota((rows, cols), 1) // cblk)


def _const_spec(shape):
    nd = len(shape)
    return pl.BlockSpec(shape, lambda *_: (0,) * nd, pipeline_mode=pl.Buffered(1))


def _layer_spec(shape, layer):
    nd = len(shape)
    return pl.BlockSpec((None,) + tuple(shape[1:]), lambda *_: (layer,) + (0,) * (nd - 1),
                        pipeline_mode=pl.Buffered(1))


def _adaln_kernel(c_ref, w_ref, b_ref, o_ref):
    c = c_ref[...]
    cond = c * _sigmoid(c)
    o_ref[0] = _dot(cond, w_ref[0]) + b_ref[0]


def _adaln(c, ada_w, ada_b):
    L, D, D6 = ada_w.shape
    B = c.shape[0]
    tn = 1536
    return pl.pallas_call(
        _adaln_kernel,
        grid=(L, D6 // tn),
        in_specs=[pl.BlockSpec((B, D), lambda l, j: (0, 0)),
                  pl.BlockSpec((1, D, tn), lambda l, j: (l, 0, j)),
                  pl.BlockSpec((1, 1, tn), lambda l, j: (l, 0, j))],
        out_specs=pl.BlockSpec((1, B, tn), lambda l, j: (l, 0, j)),
        out_shape=jax.ShapeDtypeStruct((L, B, D6), F32),
        compiler_params=pltpu.CompilerParams(
            dimension_semantics=("arbitrary", "arbitrary"), vmem_limit_bytes=VMEM_LIMIT_BYTES),
        name="adaln_mod",
    )(c, ada_w, ada_b.reshape(L, 1, D6))


def _prep_kernel(tiles_per_seq, has_vres, *refs):
    if has_vres:
        (x_ref, xh_ref, mod_ref, n1g_ref, win_ref, w2_ref, ga2_ref, gab_ref, vec_ref, obd_ref,
         cos_ref, sin_ref, v2_ref, vf_ref, rwp_ref, gr_ref) = refs
    else:
        (x_ref, xh_ref, mod_ref, n1g_ref, win_ref, w2_ref, ga2_ref, gab_ref, vec_ref, obd_ref,
         cos_ref, sin_ref, rwp_ref, gr_ref) = refs
    tm = x_ref.shape[0]
    rw = RWKV_WIDTH
    first = (pl.program_id(0) % tiles_per_seq) == 0

    xe = jnp.concatenate([xh_ref[...], x_ref[...]], axis=0)
    mod = mod_ref[0]
    ms = jnp.mean(xe * xe, axis=-1, keepdims=True)
    he = xe * lax.rsqrt(ms + EPS) * (n1g_ref[...] * (1.0 + mod[1:2])) + mod[0:1]
    he = jnp.concatenate([jnp.where(first, 0.0, he[:8]), he[8:]], axis=0)

    heb = he.astype(BF16)

    def project(lo, hi):
        return jnp.dot(heb, win_ref[:, lo:hi], preferred_element_type=F32)

    def prev_rows(t):
        return pltpu.roll(t, 1, 0)[8:]

    p_lora = project(_W_MISC, _W_IN)
    p_rkv = project(_W_IN, _W_IN + 3 * rw)
    vec = vec_ref[...]

    pre = p_lora[8:, _W_LORA_H:_W_LORA_S] + prev_rows(p_lora[:, _W_LORA_S:_W_IN])
    lane = _iota(pre.shape, 1)
    act = jnp.where(lane < 64, jnp.tanh(pre), jnp.where(lane < 128, pre, _sigmoid(pre)))
    second = _dot(act, w2_ref[...])
    p_gla = project(_W_IN + 3 * rw, _W_IN + 3 * rw + 768)
    lw = -float(np.exp(-0.5)) * _sigmoid(vec[3:4] + second[:, 0:rw])
    a = _sigmoid(vec[4:5] + second[:, rw:2 * rw])
    g = second[:, 2 * rw:3 * rw]

    p = p_rkv[8:]
    ps = prev_rows(p_rkv)
    r = p[:, 0:rw]
    r = r + (ps[:, 0:rw] - r) * vec[0:1]
    k = p[:, rw:2 * rw]
    k = k + (ps[:, rw:2 * rw] - k) * vec[1:2]
    v = p[:, 2 * rw:3 * rw]
    v = v + (ps[:, 2 * rw:3 * rw] - v) * vec[2:3]

    kk = k * vec[6:7]
    kk2 = kk * kk
    ss = jnp.concatenate(_group_sums([], obd_ref[...], coarse=[kk2[:, :GROUP_W], kk2[:, GROUP_W:]]), axis=1)
    kk = kk * lax.rsqrt(jnp.maximum(ss, 1e-24))
    k = k * (1.0 + (a - 1.0) * vec[7:8])
    misc = p_lora[8:, _W_MISC:_W_LORA_H]
    if has_vres:
        mv = jnp.where(_iota(misc.shape, 1) < 64, misc, prev_rows(p_lora[:, _W_MISC:_W_LORA_H]))
        v = v + (vf_ref[0] - v) * _sigmoid(vec[5:6] + _dot(mv, v2_ref[...]))
    p_ret = project(_W_IN + 3 * rw + 768, _WIN_W)

    rwp_ref[0] = r
    rwp_ref[1] = lw
    rwp_ref[2] = k
    rwp_ref[3] = v
    rwp_ref[4] = kk
    rwp_ref[5] = kk * a
    rwp_ref[6] = g

    p = p_gla[8:]
    gr_ref[:, _GQ:_GK] = p[:, 0:128] * (GLA_DK ** -0.5)
    gr_ref[:, _GK:_GLA] = p[:, 128:256]
    la_pre = _dot(misc, ga2_ref[...]) + gab_ref[...]
    gr_ref[:, _GLA:_GV] = -_softplus(-la_pre) * (1.0 / GLA_GATE_TAU)
    gr_ref[:, _GV:_GG] = p[:, 256:512]
    gate = p[:, 512:768]
    gr_ref[:, _GG:_RQ] = gate * _sigmoid(gate)

    p = p_ret[8:]
    cos = cos_ref[...]
    sin = sin_ref[...]
    lo_half = (_iota((tm, 128), 1) % RET_DK) < (RET_DK // 2)

    def rope(t):
        swapped = jnp.where(lo_half, pltpu.roll(t, 128 - RET_DK // 2, 1), pltpu.roll(t, RET_DK // 2, 1))
        return t * cos + swapped * sin

    gr_ref[:, _RQ:_RK] = rope(p[:, 0:128]) * (RET_DK ** -0.5)
    gr_ref[:, _RK:_RV] = rope(p[:, 128:256])
    gr_ref[:, _RV:_RG] = p[:, 256:512]
    gate = p[:, 512:768]
    gr_ref[:, _RG:_GR_W] = gate * _sigmoid(gate)


def _prep(x2d, mod, layer, seq, tm, has_vres, n1g, win, w2, ga2, gab, vec, obd, cos_t, sin_t,
          v2=None, rwp_first=None):
    T, D = x2d.shape
    tiles_per_seq = seq // tm
    n_tiles = T // tm
    per_layer = [n1g, win, w2, ga2, gab, vec]
    in_specs = [pl.BlockSpec((tm, D), lambda i: (i, 0)),
                pl.BlockSpec((8, D), lambda i: (jnp.maximum(i * (tm // 8) - 1, 0), 0)),
                pl.BlockSpec((None, 1, 6, D), lambda i: (layer, i // tiles_per_seq, 0, 0))]
    in_specs += [_layer_spec(a.shape, layer) for a in per_layer]
    in_specs += [_const_spec(obd.shape),
                 pl.BlockSpec((tm, 128), lambda i: (i % tiles_per_seq, 0)),
                 pl.BlockSpec((tm, 128), lambda i: (i % tiles_per_seq, 0))]
    args = [x2d, x2d, mod] + per_layer + [obd, cos_t, sin_t]
    if has_vres:
        in_specs += [_layer_spec(v2.shape, layer),
                     pl.BlockSpec((1, tm, RWKV_WIDTH), lambda i: (3, i, 0))]
        args += [v2, rwp_first]
    return pl.pallas_call(
        functools.partial(_prep_kernel, tiles_per_seq, has_vres),
        grid=(n_tiles,),
        in_specs=in_specs,
        out_specs=[pl.BlockSpec((7, tm, RWKV_WIDTH), lambda i: (0, i, 0)),
                   pl.BlockSpec((tm, _GR_W), lambda i: (i, 0))],
        out_shape=[jax.ShapeDtypeStruct((7, T, RWKV_WIDTH), F32),
                   jax.ShapeDtypeStruct((T, _GR_W), F32)],
        compiler_params=pltpu.CompilerParams(
            dimension_semantics=("parallel",), vmem_limit_bytes=VMEM_LIMIT_BYTES),
        name="proj_prep",
    )(*args)


def _chunk_masks(groups):
    C = CHUNK
    row = _iota((C, groups * C), 0)
    col = _iota((C, groups * C), 1) % C
    return row > col, row >= col, (row == col).astype(F32)


def _rwkv_chunk_scaled(r, lw, k, v, kk, kb):
    C = CHUNK
    cl = _cumsum_rows(lw)
    cle = cl[C - 1:C]
    e_neg = jnp.exp(-cl)
    e_end = jnp.exp(cle - cl)
    at = -kk * jnp.exp(cl - lw)
    rt = r * jnp.exp(cl)
    return dict(at=at, rt=rt, ar=jnp.concatenate([at, rt], axis=0), bt=kb * e_neg, kt=k * e_neg,
                bk_end=jnp.concatenate([kb * e_end, k * e_end], axis=0), g_end=jnp.exp(cle), v=v)


def _rwkv_chunk_factors(units, masks):
    C, G = CHUNK, HEADS_PER_GROUP
    strict, incl, ident = masks
    n = range(len(units))
    at, rt, ar, v = ([u[name] for u in units] for name in ("at", "rt", "ar", "v"))
    sb = [_bd_dot(ar[i], units[i]["bt"], G, nt=True) for i in n]
    sk = [_bd_dot(ar[i], units[i]["kt"], G, nt=True) for i in n]
    yield
    a_rb = [jnp.where(incl, sb[i][C:], 0.0) for i in n]
    a_ak = [jnp.where(strict, sk[i][:C], 0.0) for i in n]
    a_rk = [jnp.where(incl, sk[i][C:], 0.0) for i in n]

    a_ab = [jnp.where(strict, sb[i][:C], 0.0) for i in n]
    row = _iota((C, G * C), 0)
    col = _iota((C, G * C), 1) % C
    bb = [jnp.where(row // INV_BASE_BLOCK == col // INV_BASE_BLOCK, a_ab[i], 0.0) for i in n]
    tm = [ident + bb[i] for i in n]
    xp = [_bd_dot(bb[i], bb[i], G) for i in n]
    yield
    rr = [_bd_dot(jnp.concatenate([tm[i], xp[i]], axis=0), xp[i], G) for i in n]
    tm = [tm[i] + rr[i][:C] for i in n]
    yield
    tm = [tm[i] + _bd_dot(tm[i], rr[i][C:], G) for i in n]
    yield
    xs = []
    s = INV_BASE_BLOCK
    while s < C:
        sub = (row // (2 * s) == col // (2 * s)) & (row % (2 * s) >= s) & (col % (2 * s) < s)
        xs.append([jnp.where(sub, a_ab[i], 0.0) for i in n])
        s *= 2
    rr = [_bd_dot(jnp.concatenate([x[i] for x in xs], axis=0), tm[i], G) for i in n]
    yield
    xt = [[rr[i][k * C:(k + 1) * C] for i in n] for k in range(len(xs))]
    while xt:
        rr = [_bd_dot(jnp.concatenate([tm[i]] + [z[i] for z in xt[1:]], axis=0), xt[0][i], G) for i in n]
        yield
        tm = [tm[i] + rr[i][:C] for i in n]
        xt = [[z[i] + rr[i][(k + 1) * C:(k + 2) * C] for i in n] for k, z in enumerate(xt[1:])]

    vv = [_bd_dot(jnp.concatenate([a_ak[i], a_rk[i]], axis=0), v[i], G) for i in n]
    yield
    tg = [_bd_dot(a_rb[i], tm[i], G) for i in n]
    yield
    tg = [jnp.concatenate([tm[i], tg[i]], axis=0) for i in n]
    wq = [_bd_dot(tg[i], at[i], G) for i in n]
    yield
    uy = [_bd_dot(tg[i], vv[i][:C], G) for i in n]
    yield
    wt = [wq[i][:C] for i in n]
    ut = [uy[i][:C] for i in n]
    qh = [rt[i] + wq[i][C:] for i in n]
    yh = [uy[i][C:] + vv[i][C:] for i in n]
    zero = jnp.zeros((C, 128), F32)
    left = _iota((C, 128), 1) < HEAD_DV
    pm, zm = [[] for _ in n], [[] for _ in n]
    for s in range(GROUP_W // 128):
        sl = slice(128 * s, 128 * (s + 1))
        pz = [_dot_tn(units[i]["bk_end"][:, sl],
                      jnp.concatenate([jnp.concatenate([wt[i][:, sl], ut[i][:, sl]], axis=1),
                                       jnp.concatenate([zero, v[i][:, sl]], axis=1)], axis=0)) for i in n]
        for i in n:
            pm[i].append(jnp.where(left, pz[i][:C, :128], pz[i][C:, :128]))
            zm[i].append(jnp.where(left, pz[i][:C, 128:], pz[i][C:, 128:]))
        yield
    pm = [jnp.concatenate(pm[i], axis=1) + jnp.where(ident > 0, units[i]["g_end"], 0.0) for i in n]
    zm = [jnp.concatenate(zm[i], axis=1) for i in n]
    return qh, yh, pm, zm


def _interleave(main, side):
    side_done, side_out = False, None
    while True:
        try:
            next(main)
        except StopIteration as stop:
            main_out = stop.value
            break
        if not side_done:
            try:
                next(side)
            except StopIteration as stop:
                side_done, side_out = True, stop.value
    while not side_done:
        try:
            next(side)
        except StopIteration as stop:
            side_done, side_out = True, stop.value
    return main_out, side_out


def _rwkv_kernel(cb, rwp_ref, par_ref, o_ref, st_ref):
    C, W, N = CHUNK, GROUP_W, HEAD_DV
    ngroups = o_ref.shape[1] // W

    @pl.when(pl.program_id(1) == 0)
    def _():
        st_ref[...] = jnp.zeros_like(st_ref)

    masks = _chunk_masks(HEADS_PER_GROUP)
    ones_bd = _block_mask(W, W, N, N).astype(F32)
    m = [st_ref[gi] for gi in range(ngroups)]

    def wave_units(chunks):
        return [(slice(j * C, (j + 1) * C), slice(gi * W, (gi + 1) * W), gi)
                for j in chunks for gi in range(ngroups)]

    def factors(units):
        scaled = [_rwkv_chunk_scaled(*[rwp_ref[i, rows, lanes] for i in range(6)])
                  for rows, lanes, _ in units]
        return _rwkv_chunk_factors(scaled, masks)

    def chain(units, fac):
        qh, yh, pm, zm = fac
        y = []
        for u, (_, _, gi) in enumerate(units):
            ym = _dot(jnp.concatenate([qh[u], pm[u]], axis=0), _bd_rows(m[gi], HEADS_PER_GROUP))
            y.append(ym[:C] + yh[u])
            m[gi] = ym[C:] + zm[u]
            yield
        return y

    def epilogue(units, y):
        n = range(len(units))
        par = [par_ref[:, lanes] for _, lanes, _ in units]
        rkr = [rwp_ref[0, rows, lanes] * rwp_ref[2, rows, lanes] * par[u][0:1]
               for u, (rows, lanes, _) in enumerate(units)]
        sums = _group_sums(y + rkr, ones_bd)
        yield
        yc = [y[u] - sums[u] * (1.0 / N) for u in n]
        var = _group_sums([], ones_bd, coarse=[yc[u] * yc[u] for u in n])
        yield
        for u, (rows, lanes, _) in enumerate(units):
            yn = yc[u] * lax.rsqrt(var[u] * (1.0 / N) + RWKV_GN_EPS) * par[u][1:2] + par[u][2:3]
            o_ref[rows, lanes] = ((yn + sums[len(units) + u] * rwp_ref[3, rows, lanes])
                                  * rwp_ref[6, rows, lanes])
            yield

    def in_turn(*gens):
        outs = []
        for g in gens:
            outs.append((yield from g))
        return outs

    nw = RWKV_WAVES_PER_STEP
    per_wave = cb // nw
    units = [wave_units(range(w * per_wave, (w + 1) * per_wave)) for w in range(nw)]
    fac, ys = [None] * nw, [None] * nw
    for k in range(nw):
        side = []
        if k >= 1:
            side.append(chain(units[k - 1], fac[k - 1]))
        if k >= 2:
            side.append(epilogue(units[k - 2], ys[k - 2]))
        fac[k], outs = _interleave(factors(units[k]), in_turn(*side))
        if k >= 1:
            ys[k - 1] = outs[0]
    drain = epilogue(units[nw - 2], ys[nw - 2]) if nw >= 2 else iter(())
    _, ys[nw - 1] = _interleave(drain, chain(units[nw - 1], fac[nw - 1]))
    _interleave(epilogue(units[nw - 1], ys[nw - 1]), iter(()))
    for gi in range(ngroups):
        st_ref[gi] = m[gi]


def _rwkv(rwp, par, batch, seq, cb):
    _, T, RW = rwp.shape
    steps = seq // (CHUNK * cb)
    return pl.pallas_call(
        functools.partial(_rwkv_kernel, cb),
        grid=(batch, steps),
        in_specs=[pl.BlockSpec((7, cb * CHUNK, RW), lambda b, c: (0, b * steps + c, 0)),
                  _const_spec(par.shape)],
        out_specs=pl.BlockSpec((cb * CHUNK, RW), lambda b, c: (b * steps + c, 0)),
        out_shape=jax.ShapeDtypeStruct((T, RW), F32),
        scratch_shapes=[pltpu.VMEM((RW // GROUP_W, HEAD_DV, GROUP_W), F32)],
        compiler_params=pltpu.CompilerParams(
            dimension_semantics=("parallel", "arbitrary"), vmem_limit_bytes=VMEM_LIMIT_BYTES),
        name="rwkv7_chunk",
    )(rwp, par)


def _glaret_kernel(cb, gr_ref, lng_ref, dec_ref, kdec_ref, qdec_ref, cdec_ref, o_ref, sg_ref, sr_ref):
    C, G, DV = CHUNK, GLA_HEADS, HEAD_DV
    WV = G * DV

    @pl.when(pl.program_id(1) == 0)
    def _():
        sg_ref[...] = jnp.zeros_like(sg_ref)
        sr_ref[...] = jnp.zeros_like(sr_ref)

    _, incl, _ = _chunk_masks(G)
    ones_bd = _block_mask(WV, WV, DV, DV).astype(F32)
    st_mask = _block_mask(WV, G * GLA_DK, DV, GLA_DK)
    lng = lng_ref[...]
    dec = dec_ref[...]
    kdec = kdec_ref[...]
    qdec = qdec_ref[...]
    cdec = cdec_ref[...]

    n = range(cb)
    rows = [slice(j * C, (j + 1) * C) for j in n]
    gq = [gr_ref[rows[j], _GQ:_GK] for j in n]
    gk = [gr_ref[rows[j], _GK:_GLA] for j in n]
    gv = [gr_ref[rows[j], _GV:_GG] for j in n]
    bc = [_cumsum_rows(gr_ref[rows[j], _GLA:_GV]) for j in n]
    be = [bc[j][C - 1:C] for j in n]
    mid = [bc[j][C // 2 - 1:C // 2] for j in n]
    ep = [jnp.exp(bc[j] - mid[j]) for j in n]
    en = [jnp.exp(mid[j] - bc[j]) for j in n]
    qp = [gq[j] * jnp.exp(bc[j]) for j in n]
    att_lo = [_dot_nt(gq[j] * ep[j], _bd_rows(gk[j] * en[j], G)) for j in n]
    att_hi = [_dot_nt(gq[j] * en[j], _bd_rows(gk[j] * ep[j], G)) for j in n]
    rq = [gr_ref[rows[j], _RQ:_RK] for j in n]
    rk = [gr_ref[rows[j], _RK:_RV] for j in n]
    rv = [gr_ref[rows[j], _RV:_RG] for j in n]
    sc = [_dot_nt(rq[j], _bd_rows(rk[j], G)) * dec for j in n]
    g_intra = [_bd_dot(jnp.where(incl, att_lo[j], att_hi[j]), gv[j], G) for j in n]
    r_intra = [_bd_dot(sc[j], rv[j], G) for j in n]
    g_kv = [jnp.where(st_mask, _dot_tn(gv[j], gk[j] * jnp.exp(be[j] - bc[j])), 0.0) for j in n]
    r_kv = [jnp.where(st_mask, _dot_tn(rv[j], rk[j] * kdec), 0.0) for j in n]

    sg = sg_ref[...]
    sr = sr_ref[...]
    g_o = []
    r_o = []
    for j in n:
        g_o.append(g_intra[j] + _dot_nt(qp[j], sg))
        sg = sg * jnp.exp(be[j]) + g_kv[j]
        r_o.append(r_intra[j] + _dot_nt(rq[j] * qdec, sr))
        sr = sr * cdec + r_kv[j]
    sg_ref[...] = sg
    sr_ref[...] = sr

    sums = _group_sums(r_o, ones_bd, coarse=[g_o[j] * g_o[j] for j in n])
    r_c = [r_o[j] - sums[j] * (1.0 / DV) for j in n]
    r_var = _group_sums([], ones_bd, coarse=[r_c[j] * r_c[j] for j in n])
    for j in n:
        o_ref[rows[j], 0:WV] = (g_o[j] * lax.rsqrt(sums[cb + j] * (1.0 / DV) + EPS) * lng
                                * gr_ref[rows[j], _GG:_RQ])
        o_ref[rows[j], WV:2 * WV] = (r_c[j] * lax.rsqrt(r_var[j] * (1.0 / DV) + EPS)
                                     * gr_ref[rows[j], _RG:_GR_W])


def _glaret(gr, lng, dec, kdec, qdec, cdec, batch, seq, cb):
    T = gr.shape[0]
    nc = seq // (CHUNK * cb)
    wv = GLA_HEADS * HEAD_DV
    consts = [lng, dec, kdec, qdec, cdec]
    return pl.pallas_call(
        functools.partial(_glaret_kernel, cb),
        grid=(batch, nc),
        in_specs=[pl.BlockSpec((cb * CHUNK, _GR_W), lambda b, c: (b * nc + c, 0))]
                 + [_const_spec(a.shape) for a in consts],
        out_specs=pl.BlockSpec((cb * CHUNK, 2 * wv), lambda b, c: (b * nc + c, 0)),
        out_shape=jax.ShapeDtypeStruct((T, 2 * wv), F32),
        scratch_shapes=[pltpu.VMEM((wv, GLA_HEADS * GLA_DK), F32),
                        pltpu.VMEM((wv, RET_HEADS * RET_DK), F32)],
        compiler_params=pltpu.CompilerParams(
            dimension_semantics=("parallel", "arbitrary"), vmem_limit_bytes=VMEM_LIMIT_BYTES),
        name="gla_retention_chunk",
    )(gr, *consts)


def _outffn_kernel(final_norm, x_ref, ya_ref, ybc_ref, mod_ref, n2g_ref, wo_ref, wg_ref, wu_ref,
                   wd_ref, nfg_ref, o_ref):
    mod = mod_ref[0]
    y = jnp.concatenate([ya_ref[...], ybc_ref[...]], axis=-1)
    x = x_ref[...] + mod[2:3] * _dot(y, wo_ref[...])
    ms = jnp.mean(x * x, axis=-1, keepdims=True)
    h = (x * lax.rsqrt(ms + EPS) * (n2g_ref[...] * (1.0 + mod[4:5])) + mod[3:4]).astype(BF16)
    gate = jnp.dot(h, wg_ref[...], preferred_element_type=F32)
    up = jnp.dot(h, wu_ref[...], preferred_element_type=F32)
    x = x + mod[5:6] * _dot(gate * _sigmoid(gate) * up, wd_ref[...])
    if final_norm:
        ms = jnp.mean(x * x, axis=-1, keepdims=True)
        x = x * lax.rsqrt(ms + EPS) * nfg_ref[...]
    o_ref[...] = x


def _outffn(x2d, ya, ybc, mod, layer, seq, tm, final_norm, n2g, wo, wg, wu, wd, nfg):
    T, D = x2d.shape
    tiles_per_seq = seq // tm
    per_layer = [n2g, wo, wg, wu, wd]
    return pl.pallas_call(
        functools.partial(_outffn_kernel, final_norm),
        grid=(T // tm,),
        in_specs=[pl.BlockSpec((tm, D), lambda i: (i, 0)),
                  pl.BlockSpec((tm, ya.shape[1]), lambda i: (i, 0)),
                  pl.BlockSpec((tm, ybc.shape[1]), lambda i: (i, 0)),
                  pl.BlockSpec((None, 1, 6, D), lambda i: (layer, i // tiles_per_seq, 0, 0))]
                 + [_layer_spec(a.shape, layer) for a in per_layer] + [_const_spec(nfg.shape)],
        out_specs=pl.BlockSpec((tm, D), lambda i: (i, 0)),
        out_shape=jax.ShapeDtypeStruct((T, D), F32),
        compiler_params=pltpu.CompilerParams(
            dimension_semantics=("parallel",), vmem_limit_bytes=VMEM_LIMIT_BYTES),
        name="outproj_swiglu",
    )(x2d, ya, ybc, mod, *per_layer, nfg)


def _rope_tables(seq):
    half = RET_DK // 2
    inv_freq = ROPE_BASE ** (-jnp.arange(half, dtype=F32) / half)
    ang = jnp.arange(seq, dtype=F32)[:, None] * inv_freq[None, :]
    cos, sin = jnp.cos(ang), jnp.sin(ang)
    cos_t = jnp.tile(jnp.concatenate([cos, cos], axis=-1), (1, RET_HEADS))
    sin_t = jnp.tile(jnp.concatenate([-sin, sin], axis=-1), (1, RET_HEADS))
    return cos_t, sin_t


def _retention_tables():
    H, C = RET_HEADS, CHUNK
    log_gamma = jnp.log1p(-(2.0 ** (-5.0 - jnp.arange(H, dtype=F32))))
    pos = jnp.arange(C, dtype=F32)
    intra = jnp.exp(log_gamma[:, None, None] * jnp.abs(pos[:, None] - pos[None, :]))
    dec = jnp.transpose(intra, (1, 0, 2)).reshape(C, H * C)
    k_dec = jnp.exp(log_gamma[None, :] * (C - 1.0 - pos)[:, None])
    q_dec = jnp.exp(log_gamma[None, :] * (pos + 1.0)[:, None])
    chunk_dec = jnp.exp(log_gamma * C)
    kdec = jnp.repeat(k_dec, RET_DK, axis=1)
    qdec = jnp.repeat(q_dec, RET_DK, axis=1)
    cdec = jnp.repeat(chunk_dec, RET_DK)[None, :]
    return dec, kdec, qdec, cdec


def kernel(x, c, ada_w, ada_b, norm1_g, norm2_g, w_in, w_out, rk_mu_rkv, rk_mu_x, rk_w0, rk_w1, rk_w2, rk_a0, rk_a1, rk_a2, rk_g1, rk_g2, rk_k_k, rk_k_a, rk_r_k, rk_ln_g, rk_ln_b, rk_mu_v, rk_v0, rk_v1, rk_v2, gla_a1, gla_a2, gla_ab, gla_ln_g, ffn_w_gate, ffn_w_up, ffn_w_down, norm_f_g):
    B, S, D = x.shape
    L = ada_w.shape[0]
    T = B * S
    assert S % PREP_TOKENS_PER_STEP == 0 and S % FFN_TOKENS_PER_STEP == 0
    assert S % (CHUNK * RWKV_CHUNKS_PER_STEP) == 0 and S % (CHUNK * GLARET_CHUNKS_PER_STEP) == 0

    mod = _adaln(c, ada_w, ada_b).reshape(L, B, 6, D)
    cos_t, sin_t = _rope_tables(S)
    dec, kdec, qdec, cdec = _retention_tables()
    obd = _block_mask(GROUP_W, GROUP_W, HEAD_DV, HEAD_DV).astype(BF16)
    RW = RWKV_WIDTH

    def zeros(*shape):
        return jnp.zeros(shape, F32)

    mu_v = jnp.concatenate([zeros(1, D), rk_mu_v], axis=0)
    v0 = jnp.concatenate([zeros(1, RW), rk_v0], axis=0)
    v1 = jnp.concatenate([zeros(1, D, rk_v1.shape[2]), rk_v1], axis=0)
    v2 = jnp.concatenate([zeros(1, rk_v2.shape[1], RW), rk_v2], axis=0)

    def on_h(mu, w):
        return (1.0 - mu)[:, :, None] * w

    def on_prev(mu, w):
        return mu[:, :, None] * w

    mu = rk_mu_x
    misc = jnp.concatenate([gla_a1, zeros(L, D, 16), on_h(mu_v, v1), on_prev(mu_v, v1), zeros(L, D, 32)],
                           axis=2)
    lora_h = jnp.concatenate([on_h(mu[:, 0], rk_w1), on_h(mu[:, 1], rk_a1), on_h(mu[:, 2], rk_g1)], axis=2)
    lora_s = jnp.concatenate([on_prev(mu[:, 0], rk_w1), on_prev(mu[:, 1], rk_a1), on_prev(mu[:, 2], rk_g1)],
                             axis=2)
    win = jnp.concatenate([t.astype(BF16) for t in (misc, lora_h, lora_s, w_in)], axis=2)
    assert win.shape[2] == _WIN_W and misc.shape[2] == _W_LORA_H - _W_MISC
    w2 = jnp.concatenate([
        jnp.concatenate([rk_w2, zeros(L, 64, 2 * RW)], axis=2),
        jnp.concatenate([zeros(L, 64, RW), rk_a2, zeros(L, 64, RW)], axis=2),
        jnp.concatenate([zeros(L, 128, 2 * RW), rk_g2], axis=2)], axis=1).astype(BF16)
    ga2 = jnp.concatenate([gla_a2, zeros(L, 128 - gla_a2.shape[1], gla_a2.shape[2])], axis=1).astype(BF16)
    v2p = jnp.concatenate([zeros(L, 32, RW), v2, v2, zeros(L, 32, RW)], axis=1).astype(BF16)
    vec = jnp.stack([rk_mu_rkv[:, 0], rk_mu_rkv[:, 1], rk_mu_rkv[:, 2], rk_w0, rk_a0, v0, rk_k_k, rk_k_a],
                    axis=1)
    par = jnp.stack([rk_r_k.reshape(L, RW), rk_ln_g, rk_ln_b], axis=1)
    lng = jnp.tile(gla_ln_g, (1, GLA_HEADS))[:, None, :]
    wo, wg, wu, wd = (w.astype(BF16) for w in (w_out, ffn_w_gate, ffn_w_up, ffn_w_down))

    x2d = x.reshape(T, D)
    rwp_first = None
    for l in range(L):
        has_vres = l > 0
        extra = dict(v2=v2p, rwp_first=rwp_first) if has_vres else {}
        rwp, gr = _prep(x2d, mod, l, S, PREP_TOKENS_PER_STEP, has_vres, norm1_g[:, None, :], win, w2, ga2,
                        gla_ab[:, None, :], vec, obd, cos_t, sin_t, **extra)
        if l == 0:
            rwp_first = rwp
        ya = _rwkv(rwp, par[l], B, S, RWKV_CHUNKS_PER_STEP)
        ybc = _glaret(gr, lng[l], dec, kdec, qdec, cdec, B, S, GLARET_CHUNKS_PER_STEP)
        x2d = _outffn(x2d, ya, ybc, mod, l, S, FFN_TOKENS_PER_STEP, l == L - 1, norm2_g[:, None, :],
                      wo, wg, wu, wd, norm_f_g[None])
    return x2d.reshape(B, S, D)
```

```python
import functools

import numpy as np
import jax
import jax.numpy as jnp
from jax import lax
from jax.experimental import pallas as pl
from jax.experimental.pallas import tpu as pltpu

F32 = jnp.float32
BF16 = jnp.bfloat16

CHUNK = 64
EPS = 1e-6
HEAD_DV = 64
RWKV_HEADS = 8
RWKV_WIDTH = RWKV_HEADS * HEAD_DV
RWKV_GN_EPS = 64e-5
GLA_HEADS = 4
GLA_DK = 32
GLA_GATE_TAU = 16.0
RET_HEADS = 4
RET_DK = 32
ROPE_BASE = 10000.0
HEADS_PER_GROUP = 4
GROUP_W = HEADS_PER_GROUP * HEAD_DV
VMEM_LIMIT_BYTES = 56 * 1024 * 1024
RWKV_CHUNKS_PER_STEP = 16
INV_BASE_BLOCK = 8
RWKV_WAVES_PER_STEP = 4
GLARET_CHUNKS_PER_STEP = 16
PREP_TOKENS_PER_STEP = 512
FFN_TOKENS_PER_STEP = 512

_GQ, _GK, _GLA, _GV, _GG, _RQ, _RK, _RV, _RG, _GR_W = 0, 128, 256, 384, 640, 896, 1024, 1152, 1408, 1664
_W_MISC, _W_LORA_H, _W_LORA_S, _W_IN, _WIN_W = 0, 128, 384, 640, 3712


def _dot(a, b):
    return jnp.dot(a.astype(BF16), b.astype(BF16), preferred_element_type=F32)


def _dot_nt(a, b):
    return lax.dot_general(a.astype(BF16), b.astype(BF16), (((1,), (1,)), ((), ())),
                           preferred_element_type=F32)


def _dot_tn(a, b):
    return lax.dot_general(a.astype(BF16), b.astype(BF16), (((0,), (0,)), ((), ())),
                           preferred_element_type=F32)


def _group_sums(xs, ones_bd, coarse=()):
    hi = [x.astype(BF16) for x in xs]
    lo = [(x - h.astype(F32)).astype(BF16) for x, h in zip(xs, hi)]
    terms = hi + [x.astype(BF16) for x in coarse] + lo
    s = jnp.dot(jnp.concatenate(terms, axis=0), ones_bd.astype(BF16), preferred_element_type=F32)
    offs = np.cumsum([0] + [t.shape[0] for t in terms])
    nx, nc = len(xs), len(coarse)
    out = [s[offs[i]:offs[i + 1]] + s[offs[nx + nc + i]:offs[nx + nc + i + 1]] for i in range(nx)]
    return out + [s[offs[nx + i]:offs[nx + i + 1]] for i in range(nc)]


def _cumsum_rows(x):
    n = x.shape[0]
    row = _iota(x.shape, 0)
    s = 1
    while s < n:
        x = x + jnp.where(row >= s, pltpu.roll(x, s, 0), 0.0)
        s *= 2
    return x


def _sigmoid(x):
    return 1.0 / (1.0 + jnp.exp(-x))


def _softplus(x):
    return jnp.maximum(x, 0.0) + jnp.log(1.0 + jnp.exp(-jnp.abs(x)))


def _iota(shape, axis):
    return lax.broadcasted_iota(jnp.int32, shape, axis)


def _bd_rows(x, groups):
    c, w = x.shape
    n = w // groups
    t = jnp.concatenate([x] * groups, axis=0)
    keep = (_iota(t.shape, 0) // c) == (_iota(t.shape, 1) // n)
    return jnp.where(keep, t, 0.0)


def _bd_dot(x, y, groups, nt=False):
    e = _bd_rows(y, groups)
    return _dot_nt(x, e) if nt else _dot(x, e)


def _block_mask(rows, cols, rblk, cblk):
    return (_iota((rows, cols), 0) // rblk) == (_iota((rows, cols), 1) // cblk)


def _const_spec(shape):
    nd = len(shape)
    return pl.BlockSpec(shape, lambda *_: (0,) * nd, pipeline_mode=pl.Buffered(1))


def _layer_spec(shape, layer):
    nd = len(shape)
    return pl.BlockSpec((None,) + tuple(shape[1:]), lambda *_: (layer,) + (0,) * (nd - 1),
                        pipeline_mode=pl.Buffered(1))


def _adaln_kernel(c_ref, w_ref, b_ref, o_ref):
    c = c_ref[...]
    cond = c * _sigmoid(c)
    o_ref[0] = _dot(cond, w_ref[0]) + b_ref[0]


def _adaln(c, ada_w, ada_b):
    L, D, D6 = ada_w.shape
    B = c.shape[0]
    tn = 1536
    return pl.pallas_call(
        _adaln_kernel,
        grid=(L, D6 // tn),
        in_specs=[pl.BlockSpec((B, D), lambda l, j: (0, 0)),
                  pl.BlockSpec((1, D, tn), lambda l, j: (l, 0, j)),
                  pl.BlockSpec((1, 1, tn), lambda l, j: (l, 0, j))],
        out_specs=pl.BlockSpec((1, B, tn), lambda l, j: (l, 0, j)),
        out_shape=jax.ShapeDtypeStruct((L, B, D6), F32),
        compiler_params=pltpu.CompilerParams(
            dimension_semantics=("arbitrary", "arbitrary"), vmem_limit_bytes=VMEM_LIMIT_BYTES),
        name="adaln_mod",
    )(c, ada_w, ada_b.reshape(L, 1, D6))


def _prep_kernel(tiles_per_seq, has_vres, *refs):
    if has_vres:
        (x_ref, xh_ref, mod_ref, n1g_ref, win_ref, w2_ref, ga2_ref, gab_ref, vec_ref, obd_ref,
         cos_ref, sin_ref, v2_ref, vf_ref, rwp_ref, gr_ref) = refs
    else:
        (x_ref, xh_ref, mod_ref, n1g_ref, win_ref, w2_ref, ga2_ref, gab_ref, vec_ref, obd_ref,
         cos_ref, sin_ref, rwp_ref, gr_ref) = refs
    tm = x_ref.shape[0]
    rw = RWKV_WIDTH
    first = (pl.program_id(0) % tiles_per_seq) == 0

    xe = jnp.concatenate([xh_ref[...], x_ref[...]], axis=0)
    mod = mod_ref[0]
    ms = jnp.mean(xe * xe, axis=-1, keepdims=True)
    he = xe * lax.rsqrt(ms + EPS) * (n1g_ref[...] * (1.0 + mod[1:2])) + mod[0:1]
    he = jnp.concatenate([jnp.where(first, 0.0, he[:8]), he[8:]], axis=0)

    heb = he.astype(BF16)

    def project(lo, hi):
        return jnp.dot(heb, win_ref[:, lo:hi], preferred_element_type=F32)

    def prev_rows(t):
        return pltpu.roll(t, 1, 0)[8:]

    p_lora = project(_W_MISC, _W_IN)
    p_rkv = project(_W_IN, _W_IN + 3 * rw)
    vec = vec_ref[...]

    pre = p_lora[8:, _W_LORA_H:_W_LORA_S] + prev_rows(p_lora[:, _W_LORA_S:_W_IN])
    pre_wa, pre_g = pre[:, :128], pre[:, 128:]
    act = jnp.concatenate([jnp.where(_iota(pre_wa.shape, 1) < 64, jnp.tanh(pre_wa), pre_wa),
                           _sigmoid(pre_g)], axis=1)
    second = _dot(act, w2_ref[...])
    p_gla = project(_W_IN + 3 * rw, _W_IN + 3 * rw + 768)
    lw = -float(np.exp(-0.5)) * _sigmoid(vec[3:4] + second[:, 0:rw])
    a = _sigmoid(vec[4:5] + second[:, rw:2 * rw])
    g = second[:, 2 * rw:3 * rw]

    p = p_rkv[8:]
    ps = prev_rows(p_rkv)
    r = p[:, 0:rw]
    r = r + (ps[:, 0:rw] - r) * vec[0:1]
    k = p[:, rw:2 * rw]
    k = k + (ps[:, rw:2 * rw] - k) * vec[1:2]
    v = p[:, 2 * rw:3 * rw]
    v = v + (ps[:, 2 * rw:3 * rw] - v) * vec[2:3]

    kk = k * vec[6:7]
    kk2 = kk * kk
    ss = jnp.concatenate(_group_sums([], obd_ref[...], coarse=[kk2[:, :GROUP_W], kk2[:, GROUP_W:]]), axis=1)
    kk = kk * lax.rsqrt(jnp.maximum(ss, 1e-24))
    k = k * ((1.0 - vec[7:8]) + a * vec[7:8])
    misc = p_lora[8:, _W_MISC:_W_LORA_H]
    if has_vres:
        mv = jnp.where(_iota(misc.shape, 1) < 64, misc, prev_rows(p_lora[:, _W_MISC:_W_LORA_H]))
        v = v + (vf_ref[0] - v) * _sigmoid(vec[5:6] + _dot(mv, v2_ref[...]))
    p_ret = project(_W_IN + 3 * rw + 768, _WIN_W)

    rwp_ref[0] = r
    rwp_ref[1] = lw
    rwp_ref[2] = k
    rwp_ref[3] = v
    rwp_ref[4] = kk
    rwp_ref[5] = kk * a
    rwp_ref[6] = g

    p = p_gla[8:]
    gr_ref[:, _GQ:_GK] = p[:, 0:128] * (GLA_DK ** -0.5)
    gr_ref[:, _GK:_GLA] = p[:, 128:256]
    la_pre = _dot(misc, ga2_ref[...]) + gab_ref[...]
    gr_ref[:, _GLA:_GV] = -_softplus(-la_pre) * (1.0 / GLA_GATE_TAU)
    gr_ref[:, _GV:_GG] = p[:, 256:512]
    gate = p[:, 512:768]
    gr_ref[:, _GG:_RQ] = gate * _sigmoid(gate)

    p = p_ret[8:]
    cos = cos_ref[...]
    sin = sin_ref[...]
    lo_half = (_iota((tm, 128), 1) % RET_DK) < (RET_DK // 2)

    def rope(t):
        swapped = jnp.where(lo_half, pltpu.roll(t, 128 - RET_DK // 2, 1), pltpu.roll(t, RET_DK // 2, 1))
        return t * cos + swapped * sin

    gr_ref[:, _RQ:_RK] = rope(p[:, 0:128]) * (RET_DK ** -0.5)
    gr_ref[:, _RK:_RV] = rope(p[:, 128:256])
    gr_ref[:, _RV:_RG] = p[:, 256:512]
    gate = p[:, 512:768]
    gr_ref[:, _RG:_GR_W] = gate * _sigmoid(gate)


def _prep(x2d, mod, layer, seq, tm, has_vres, n1g, win, w2, ga2, gab, vec, obd, cos_t, sin_t,
          v2=None, rwp_first=None):
    T, D = x2d.shape
    tiles_per_seq = seq // tm
    n_tiles = T // tm
    per_layer = [n1g, win, w2, ga2, gab, vec]
    in_specs = [pl.BlockSpec((tm, D), lambda i: (i, 0)),
                pl.BlockSpec((8, D), lambda i: (jnp.maximum(i * (tm // 8) - 1, 0), 0)),
                pl.BlockSpec((None, 1, 6, D), lambda i: (layer, i // tiles_per_seq, 0, 0))]
    in_specs += [_layer_spec(a.shape, layer) for a in per_layer]
    in_specs += [_const_spec(obd.shape),
                 pl.BlockSpec((tm, 128), lambda i: (i % tiles_per_seq, 0)),
                 pl.BlockSpec((tm, 128), lambda i: (i % tiles_per_seq, 0))]
    args = [x2d, x2d, mod] + per_layer + [obd, cos_t, sin_t]
    if has_vres:
        in_specs += [_layer_spec(v2.shape, layer),
                     pl.BlockSpec((1, tm, RWKV_WIDTH), lambda i: (3, i, 0))]
        args += [v2, rwp_first]
    return pl.pallas_call(
        functools.partial(_prep_kernel, tiles_per_seq, has_vres),
        grid=(n_tiles,),
        in_specs=in_specs,
        out_specs=[pl.BlockSpec((7, tm, RWKV_WIDTH), lambda i: (0, i, 0)),
                   pl.BlockSpec((tm, _GR_W), lambda i: (i, 0))],
        out_shape=[jax.ShapeDtypeStruct((7, T, RWKV_WIDTH), F32),
                   jax.ShapeDtypeStruct((T, _GR_W), F32)],
        compiler_params=pltpu.CompilerParams(
            dimension_semantics=("parallel",), vmem_limit_bytes=VMEM_LIMIT_BYTES),
        name="proj_prep",
    )(*args)


def _chunk_masks(groups):
    C = CHUNK
    row = _iota((C, groups * C), 0)
    col = _iota((C, groups * C), 1) % C
    return row > col, row >= col, (row == col).astype(F32)


def _rwkv_chunk_scaled(r, lw, k, v, kk, kb):
    C = CHUNK
    cl = _cumsum_rows(lw)
    cle = cl[C - 1:C]
    e_neg = jnp.exp(-cl)
    e_end = jnp.exp(cle - cl)
    at = -kk * jnp.exp(cl - lw)
    rt = r * jnp.exp(cl)
    return dict(at=at, rt=rt, ar=jnp.concatenate([at, rt], axis=0), bt=kb * e_neg, kt=k * e_neg,
                bk_end=jnp.concatenate([kb * e_end, k * e_end], axis=0), g_end=jnp.exp(cle), v=v)


def _rwkv_chunk_factors(units, masks):
    C, G = CHUNK, HEADS_PER_GROUP
    strict, incl, ident = masks
    n = range(len(units))
    at, rt, ar, v = ([u[name] for u in units] for name in ("at", "rt", "ar", "v"))
    sb = [_bd_dot(ar[i], units[i]["bt"], G, nt=True) for i in n]
    sk = [_bd_dot(ar[i], units[i]["kt"], G, nt=True) for i in n]
    yield
    a_rb = [jnp.where(incl, sb[i][C:], 0.0) for i in n]
    a_ak = [jnp.where(strict, sk[i][:C], 0.0) for i in n]
    a_rk = [jnp.where(incl, sk[i][C:], 0.0) for i in n]

    a_ab = [jnp.where(strict, sb[i][:C], 0.0) for i in n]
    row = _iota((C, G * C), 0)
    col = _iota((C, G * C), 1) % C
    bb = [jnp.where(row // INV_BASE_BLOCK == col // INV_BASE_BLOCK, a_ab[i], 0.0) for i in n]
    tm = [ident + bb[i] for i in n]
    xp = [_bd_dot(bb[i], bb[i], G) for i in n]
    yield
    rr = [_bd_dot(jnp.concatenate([tm[i], xp[i]], axis=0), xp[i], G) for i in n]
    tm = [tm[i] + rr[i][:C] for i in n]
    yield
    tm = [tm[i] + _bd_dot(tm[i], rr[i][C:], G) for i in n]
    yield
    xs = []
    s = INV_BASE_BLOCK
    while s < C:
        sub = (row // (2 * s) == col // (2 * s)) & (row % (2 * s) >= s) & (col % (2 * s) < s)
        xs.append([jnp.where(sub, a_ab[i], 0.0) for i in n])
        s *= 2
    rr = [_bd_dot(jnp.concatenate([x[i] for x in xs], axis=0), tm[i], G) for i in n]
    yield
    xt = [[rr[i][k * C:(k + 1) * C] for i in n] for k in range(len(xs))]
    while xt:
        rr = [_bd_dot(jnp.concatenate([tm[i]] + [z[i] for z in xt[1:]], axis=0), xt[0][i], G) for i in n]
        yield
        tm = [tm[i] + rr[i][:C] for i in n]
        xt = [[z[i] + rr[i][(k + 1) * C:(k + 2) * C] for i in n] for k, z in enumerate(xt[1:])]

    vv = [_bd_dot(jnp.concatenate([a_ak[i], a_rk[i]], axis=0), v[i], G) for i in n]
    yield
    tg = [_bd_dot(a_rb[i], tm[i], G) for i in n]
    yield
    tg = [jnp.concatenate([tm[i], tg[i]], axis=0) for i in n]
    wq = [_bd_dot(tg[i], at[i], G) for i in n]
    yield
    uy = [_bd_dot(tg[i], vv[i][:C], G) for i in n]
    yield
    wt = [wq[i][:C] for i in n]
    ut = [uy[i][:C] for i in n]
    qh = [rt[i] + wq[i][C:] for i in n]
    yh = [uy[i][C:] + vv[i][C:] for i in n]
    zero = jnp.zeros((C, 128), F32)
    left = _iota((C, 128), 1) < HEAD_DV
    pm, zm = [[] for _ in n], [[] for _ in n]
    for s in range(GROUP_W // 128):
        sl = slice(128 * s, 128 * (s + 1))
        pz = [_dot_tn(units[i]["bk_end"][:, sl],
                      jnp.concatenate([jnp.concatenate([wt[i][:, sl], ut[i][:, sl]], axis=1),
                                       jnp.concatenate([zero, v[i][:, sl]], axis=1)], axis=0)) for i in n]
        for i in n:
            pm[i].append(jnp.where(left, pz[i][:C, :128], pz[i][C:, :128]))
            zm[i].append(jnp.where(left, pz[i][:C, 128:], pz[i][C:, 128:]))
        yield
    pm = [jnp.concatenate(pm[i], axis=1) + jnp.where(ident > 0, units[i]["g_end"], 0.0) for i in n]
    zm = [jnp.concatenate(zm[i], axis=1) for i in n]
    return qh, yh, pm, zm


def _interleave(main, side):
    side_done, side_out = False, None
    while True:
        try:
            next(main)
        except StopIteration as stop:
            main_out = stop.value
            break
        if not side_done:
            try:
                next(side)
            except StopIteration as stop:
                side_done, side_out = True, stop.value
    while not side_done:
        try:
            next(side)
        except StopIteration as stop:
            side_done, side_out = True, stop.value
    return main_out, side_out


def _rwkv_kernel(cb, rwp_ref, par_ref, o_ref, st_ref):
    C, W, N = CHUNK, GROUP_W, HEAD_DV
    ngroups = o_ref.shape[1] // W

    @pl.when(pl.program_id(1) == 0)
    def _():
        st_ref[...] = jnp.zeros_like(st_ref)

    masks = _chunk_masks(HEADS_PER_GROUP)
    mean_bd = _block_mask(W, W, N, N).astype(F32) * (1.0 / N)
    m = [st_ref[gi] for gi in range(ngroups)]

    def wave_units(chunks):
        return [(slice(j * C, (j + 1) * C), slice(gi * W, (gi + 1) * W), gi)
                for j in chunks for gi in range(ngroups)]

    def factors(units):
        scaled = [_rwkv_chunk_scaled(*[rwp_ref[i, rows, lanes] for i in range(6)])
                  for rows, lanes, _ in units]
        return _rwkv_chunk_factors(scaled, masks)

    def chain(units, fac):
        qh, yh, pm, zm = fac
        y = []
        for u, (_, _, gi) in enumerate(units):
            ym = _dot(jnp.concatenate([qh[u], pm[u]], axis=0), _bd_rows(m[gi], HEADS_PER_GROUP))
            y.append(ym[:C] + yh[u])
            m[gi] = ym[C:] + zm[u]
            yield
        return y

    def epilogue(units, y):
        n = range(len(units))
        par = [par_ref[:, lanes] for _, lanes, _ in units]
        rkr = [rwp_ref[0, rows, lanes] * rwp_ref[2, rows, lanes] * (par[u][0:1] * N)
               for u, (rows, lanes, _) in enumerate(units)]
        sums = _group_sums(y + rkr, mean_bd)
        yield
        yc = [y[u] - sums[u] for u in n]
        var = _group_sums([], mean_bd, coarse=[yc[u] * yc[u] for u in n])
        yield
        for u, (rows, lanes, _) in enumerate(units):
            yn = yc[u] * lax.rsqrt(var[u] + RWKV_GN_EPS) * par[u][1:2] + par[u][2:3]
            o_ref[rows, lanes] = ((yn + sums[len(units) + u] * rwp_ref[3, rows, lanes])
                                  * rwp_ref[6, rows, lanes])
            yield

    def in_turn(*gens):
        outs = []
        for g in gens:
            outs.append((yield from g))
        return outs

    nw = RWKV_WAVES_PER_STEP
    per_wave = cb // nw
    units = [wave_units(range(w * per_wave, (w + 1) * per_wave)) for w in range(nw)]
    fac, ys = [None] * nw, [None] * nw
    for k in range(nw):
        side = []
        if k >= 1:
            side.append(chain(units[k - 1], fac[k - 1]))
        if k >= 2:
            side.append(epilogue(units[k - 2], ys[k - 2]))
        fac[k], outs = _interleave(factors(units[k]), in_turn(*side))
        if k >= 1:
            ys[k - 1] = outs[0]
    drain = epilogue(units[nw - 2], ys[nw - 2]) if nw >= 2 else iter(())
    _, ys[nw - 1] = _interleave(drain, chain(units[nw - 1], fac[nw - 1]))
    _interleave(epilogue(units[nw - 1], ys[nw - 1]), iter(()))
    for gi in range(ngroups):
        st_ref[gi] = m[gi]


def _rwkv(rwp, par, batch, seq, cb):
    _, T, RW = rwp.shape
    steps = seq // (CHUNK * cb)
    return pl.pallas_call(
        functools.partial(_rwkv_kernel, cb),
        grid=(batch, steps),
        in_specs=[pl.BlockSpec((7, cb * CHUNK, RW), lambda b, c: (0, b * steps + c, 0)),
                  _const_spec(par.shape)],
        out_specs=pl.BlockSpec((cb * CHUNK, RW), lambda b, c: (b * steps + c, 0)),
        out_shape=jax.ShapeDtypeStruct((T, RW), F32),
        scratch_shapes=[pltpu.VMEM((RW // GROUP_W, HEAD_DV, GROUP_W), F32)],
        compiler_params=pltpu.CompilerParams(
            dimension_semantics=("parallel", "arbitrary"), vmem_limit_bytes=VMEM_LIMIT_BYTES),
        name="rwkv7_chunk",
    )(rwp, par)


def _glaret_kernel(cb, gr_ref, lng_ref, dec_ref, kdec_ref, qdec_ref, cdec_ref, o_ref, sg_ref, sr_ref):
    C, G, DV = CHUNK, GLA_HEADS, HEAD_DV
    WV = G * DV

    @pl.when(pl.program_id(1) == 0)
    def _():
        sg_ref[...] = jnp.zeros_like(sg_ref)
        sr_ref[...] = jnp.zeros_like(sr_ref)

    _, incl, _ = _chunk_masks(G)
    mean_bd = _block_mask(WV, WV, DV, DV).astype(F32) * (1.0 / DV)
    st_mask = _block_mask(WV, G * GLA_DK, DV, GLA_DK)
    lng = lng_ref[...]
    dec = dec_ref[...]
    kdec = kdec_ref[...]
    qdec = qdec_ref[...]
    cdec = cdec_ref[...]

    n = range(cb)
    rows = [slice(j * C, (j + 1) * C) for j in n]
    gq = [gr_ref[rows[j], _GQ:_GK] for j in n]
    gk = [gr_ref[rows[j], _GK:_GLA] for j in n]
    gv = [gr_ref[rows[j], _GV:_GG] for j in n]
    bc = [_cumsum_rows(gr_ref[rows[j], _GLA:_GV]) for j in n]
    be = [bc[j][C - 1:C] for j in n]
    mid = [bc[j][C // 2 - 1:C // 2] for j in n]
    ep = [jnp.exp(bc[j] - mid[j]) for j in n]
    en = [jnp.exp(mid[j] - bc[j]) for j in n]
    qp = [gq[j] * jnp.exp(bc[j]) for j in n]
    att_lo = [_dot_nt(gq[j] * ep[j], _bd_rows(gk[j] * en[j], G)) for j in n]
    att_hi = [_dot_nt(gq[j] * en[j], _bd_rows(gk[j] * ep[j], G)) for j in n]
    rq = [gr_ref[rows[j], _RQ:_RK] for j in n]
    rk = [gr_ref[rows[j], _RK:_RV] for j in n]
    rv = [gr_ref[rows[j], _RV:_RG] for j in n]
    sc = [_dot_nt(rq[j], _bd_rows(rk[j], G)) * dec for j in n]
    g_intra = [_bd_dot(jnp.where(incl, att_lo[j], att_hi[j]), gv[j], G) for j in n]
    r_intra = [_bd_dot(sc[j], rv[j], G) for j in n]
    g_kv = [jnp.where(st_mask, _dot_tn(gv[j], gk[j] * jnp.exp(be[j] - bc[j])), 0.0) for j in n]
    r_kv = [jnp.where(st_mask, _dot_tn(rv[j], rk[j] * kdec), 0.0) for j in n]

    sg = sg_ref[...]
    sr = sr_ref[...]
    g_o = []
    r_o = []
    for j in n:
        g_o.append(g_intra[j] + _dot_nt(qp[j], sg))
        sg = sg * jnp.exp(be[j]) + g_kv[j]
        r_o.append(r_intra[j] + _dot_nt(rq[j] * qdec, sr))
        sr = sr * cdec + r_kv[j]
    sg_ref[...] = sg
    sr_ref[...] = sr

    means = _group_sums(r_o, mean_bd, coarse=[g_o[j] * g_o[j] for j in n])
    r_c = [r_o[j] - means[j] for j in n]
    r_var = _group_sums([], mean_bd, coarse=[r_c[j] * r_c[j] for j in n])
    for j in n:
        o_ref[rows[j], 0:WV] = (g_o[j] * lax.rsqrt(means[cb + j] + EPS) * lng
                                * gr_ref[rows[j], _GG:_RQ])
        o_ref[rows[j], WV:2 * WV] = (r_c[j] * lax.rsqrt(r_var[j] + EPS)
                                     * gr_ref[rows[j], _RG:_GR_W])


def _glaret(gr, lng, dec, kdec, qdec, cdec, batch, seq, cb):
    T = gr.shape[0]
    nc = seq // (CHUNK * cb)
    wv = GLA_HEADS * HEAD_DV
    consts = [lng, dec, kdec, qdec, cdec]
    return pl.pallas_call(
        functools.partial(_glaret_kernel, cb),
        grid=(batch, nc),
        in_specs=[pl.BlockSpec((cb * CHUNK, _GR_W), lambda b, c: (b * nc + c, 0))]
                 + [_const_spec(a.shape) for a in consts],
        out_specs=pl.BlockSpec((cb * CHUNK, 2 * wv), lambda b, c: (b * nc + c, 0)),
        out_shape=jax.ShapeDtypeStruct((T, 2 * wv), F32),
        scratch_shapes=[pltpu.VMEM((wv, GLA_HEADS * GLA_DK), F32),
                        pltpu.VMEM((wv, RET_HEADS * RET_DK), F32)],
        compiler_params=pltpu.CompilerParams(
            dimension_semantics=("parallel", "arbitrary"), vmem_limit_bytes=VMEM_LIMIT_BYTES),
        name="gla_retention_chunk",
    )(gr, *consts)


def _outffn_kernel(final_norm, x_ref, ya_ref, ybc_ref, mod_ref, n2g_ref, wo_ref, wg_ref, wu_ref,
                   wd_ref, nfg_ref, o_ref):
    mod = mod_ref[0]
    y = jnp.concatenate([ya_ref[...], ybc_ref[...]], axis=-1)
    x = x_ref[...] + mod[2:3] * _dot(y, wo_ref[...])
    ms = jnp.mean(x * x, axis=-1, keepdims=True)
    h = (x * lax.rsqrt(ms + EPS) * (n2g_ref[...] * (1.0 + mod[4:5])) + mod[3:4]).astype(BF16)
    gate = jnp.dot(h, wg_ref[...], preferred_element_type=F32)
    up = jnp.dot(h, wu_ref[...], preferred_element_type=F32)
    x = x + mod[5:6] * _dot(gate * _sigmoid(gate) * up, wd_ref[...])
    if final_norm:
        ms = jnp.mean(x * x, axis=-1, keepdims=True)
        x = x * lax.rsqrt(ms + EPS) * nfg_ref[...]
    o_ref[...] = x


def _outffn(x2d, ya, ybc, mod, layer, seq, tm, final_norm, n2g, wo, wg, wu, wd, nfg):
    T, D = x2d.shape
    tiles_per_seq = seq // tm
    per_layer = [n2g, wo, wg, wu, wd]
    return pl.pallas_call(
        functools.partial(_outffn_kernel, final_norm),
        grid=(T // tm,),
        in_specs=[pl.BlockSpec((tm, D), lambda i: (i, 0)),
                  pl.BlockSpec((tm, ya.shape[1]), lambda i: (i, 0)),
                  pl.BlockSpec((tm, ybc.shape[1]), lambda i: (i, 0)),
                  pl.BlockSpec((None, 1, 6, D), lambda i: (layer, i // tiles_per_seq, 0, 0))]
                 + [_layer_spec(a.shape, layer) for a in per_layer] + [_const_spec(nfg.shape)],
        out_specs=pl.BlockSpec((tm, D), lambda i: (i, 0)),
        out_shape=jax.ShapeDtypeStruct((T, D), F32),
        compiler_params=pltpu.CompilerParams(
            dimension_semantics=("parallel",), vmem_limit_bytes=VMEM_LIMIT_BYTES),
        name="outproj_swiglu",
    )(x2d, ya, ybc, mod, *per_layer, nfg)


def _rope_tables(seq):
    half = RET_DK // 2
    inv_freq = ROPE_BASE ** (-jnp.arange(half, dtype=F32) / half)
    ang = jnp.arange(seq, dtype=F32)[:, None] * inv_freq[None, :]
    cos, sin = jnp.cos(ang), jnp.sin(ang)
    cos_t = jnp.tile(jnp.concatenate([cos, cos], axis=-1), (1, RET_HEADS))
    sin_t = jnp.tile(jnp.concatenate([-sin, sin], axis=-1), (1, RET_HEADS))
    return cos_t, sin_t


def _retention_tables():
    H, C = RET_HEADS, CHUNK
    log_gamma = jnp.log1p(-(2.0 ** (-5.0 - jnp.arange(H, dtype=F32))))
    pos = jnp.arange(C, dtype=F32)
    intra = jnp.exp(log_gamma[:, None, None] * jnp.abs(pos[:, None] - pos[None, :]))
    dec = jnp.transpose(intra, (1, 0, 2)).reshape(C, H * C)
    k_dec = jnp.exp(log_gamma[None, :] * (C - 1.0 - pos)[:, None])
    q_dec = jnp.exp(log_gamma[None, :] * (pos + 1.0)[:, None])
    chunk_dec = jnp.exp(log_gamma * C)
    kdec = jnp.repeat(k_dec, RET_DK, axis=1)
    qdec = jnp.repeat(q_dec, RET_DK, axis=1)
    cdec = jnp.repeat(chunk_dec, RET_DK)[None, :]
    return dec, kdec, qdec, cdec


def kernel(x, c, ada_w, ada_b, norm1_g, norm2_g, w_in, w_out, rk_mu_rkv, rk_mu_x, rk_w0, rk_w1, rk_w2, rk_a0, rk_a1, rk_a2, rk_g1, rk_g2, rk_k_k, rk_k_a, rk_r_k, rk_ln_g, rk_ln_b, rk_mu_v, rk_v0, rk_v1, rk_v2, gla_a1, gla_a2, gla_ab, gla_ln_g, ffn_w_gate, ffn_w_up, ffn_w_down, norm_f_g):
    B, S, D = x.shape
    L = ada_w.shape[0]
    T = B * S
    assert S % PREP_TOKENS_PER_STEP == 0 and S % FFN_TOKENS_PER_STEP == 0
    assert S % (CHUNK * RWKV_CHUNKS_PER_STEP) == 0 and S % (CHUNK * GLARET_CHUNKS_PER_STEP) == 0

    mod = _adaln(c, ada_w, ada_b).reshape(L, B, 6, D)
    cos_t, sin_t = _rope_tables(S)
    dec, kdec, qdec, cdec = _retention_tables()
    obd = _block_mask(GROUP_W, GROUP_W, HEAD_DV, HEAD_DV).astype(BF16)
    RW = RWKV_WIDTH

    def zeros(*shape):
        return jnp.zeros(shape, F32)

    mu_v = jnp.concatenate([zeros(1, D), rk_mu_v], axis=0)
    v0 = jnp.concatenate([zeros(1, RW), rk_v0], axis=0)
    v1 = jnp.concatenate([zeros(1, D, rk_v1.shape[2]), rk_v1], axis=0)
    v2 = jnp.concatenate([zeros(1, rk_v2.shape[1], RW), rk_v2], axis=0)

    def on_h(mu, w):
        return (1.0 - mu)[:, :, None] * w

    def on_prev(mu, w):
        return mu[:, :, None] * w

    mu = rk_mu_x
    misc = jnp.concatenate([gla_a1, zeros(L, D, 16), on_h(mu_v, v1), on_prev(mu_v, v1), zeros(L, D, 32)],
                           axis=2)
    lora_h = jnp.concatenate([on_h(mu[:, 0], rk_w1), on_h(mu[:, 1], rk_a1), on_h(mu[:, 2], rk_g1)], axis=2)
    lora_s = jnp.concatenate([on_prev(mu[:, 0], rk_w1), on_prev(mu[:, 1], rk_a1), on_prev(mu[:, 2], rk_g1)],
                             axis=2)
    win = jnp.concatenate([t.astype(BF16) for t in (misc, lora_h, lora_s, w_in)], axis=2)
    assert win.shape[2] == _WIN_W and misc.shape[2] == _W_LORA_H - _W_MISC
    w2 = jnp.concatenate([
        jnp.concatenate([rk_w2, zeros(L, 64, 2 * RW)], axis=2),
        jnp.concatenate([zeros(L, 64, RW), rk_a2, zeros(L, 64, RW)], axis=2),
        jnp.concatenate([zeros(L, 128, 2 * RW), rk_g2], axis=2)], axis=1).astype(BF16)
    ga2 = jnp.concatenate([gla_a2, zeros(L, 128 - gla_a2.shape[1], gla_a2.shape[2])], axis=1).astype(BF16)
    v2p = jnp.concatenate([zeros(L, 32, RW), v2, v2, zeros(L, 32, RW)], axis=1).astype(BF16)
    vec = jnp.stack([rk_mu_rkv[:, 0], rk_mu_rkv[:, 1], rk_mu_rkv[:, 2], rk_w0, rk_a0, v0, rk_k_k, rk_k_a],
                    axis=1)
    par = jnp.stack([rk_r_k.reshape(L, RW), rk_ln_g, rk_ln_b], axis=1)
    lng = jnp.tile(gla_ln_g, (1, GLA_HEADS))[:, None, :]
    wo, wg, wu, wd = (w.astype(BF16) for w in (w_out, ffn_w_gate, ffn_w_up, ffn_w_down))

    x2d = x.reshape(T, D)
    rwp_first = None
    for l in range(L):
        has_vres = l > 0
        extra = dict(v2=v2p, rwp_first=rwp_first) if has_vres else {}
        rwp, gr = _prep(x2d, mod, l, S, PREP_TOKENS_PER_STEP, has_vres, norm1_g[:, None, :], win, w2, ga2,
                        gla_ab[:, None, :], vec, obd, cos_t, sin_t, **extra)
        if l == 0:
            rwp_first = rwp
        ya = _rwkv(rwp, par[l], B, S, RWKV_CHUNKS_PER_STEP)
        ybc = _glaret(gr, lng[l], dec, kdec, qdec, cdec, B, S, GLARET_CHUNKS_PER_STEP)
        x2d = _outffn(x2d, ya, ybc, mod, l, S, FFN_TOKENS_PER_STEP, l == L - 1, norm2_g[:, None, :],
                      wo, wg, wu, wd, norm_f_g[None])
    return x2d.reshape(B, S, D)
```

```python
import functools

import numpy as np
import jax
import jax.numpy as jnp
from jax import lax
from jax.experimental import pallas as pl
from jax.experimental.pallas import tpu as pltpu

F32 = jnp.float32
BF16 = jnp.bfloat16

CHUNK = 64
EPS = 1e-6
HEAD_DV = 64
RWKV_HEADS = 8
RWKV_WIDTH = RWKV_HEADS * HEAD_DV
RWKV_GN_EPS = 64e-5
GLA_HEADS = 4
GLA_DK = 32
GLA_GATE_TAU = 16.0
RET_HEADS = 4
RET_DK = 32
ROPE_BASE = 10000.0
HEADS_PER_GROUP = 4
GROUP_W = HEADS_PER_GROUP * HEAD_DV
VMEM_LIMIT_BYTES = 56 * 1024 * 1024
RWKV_CHUNKS_PER_STEP = 16
INV_BASE_BLOCK = 8
RWKV_WAVES_PER_STEP = 4
GLARET_CHUNKS_PER_STEP = 16
PREP_TOKENS_PER_STEP = 512
FFN_TOKENS_PER_STEP = 512

_GQ, _GK, _GLA, _GV, _GG, _RQ, _RK, _RV, _RG, _GR_W = 0, 128, 256, 384, 640, 896, 1024, 1152, 1408, 1664
_W_MISC, _W_LORA_H, _W_LORA_S, _W_IN, _WIN_W = 0, 128, 384, 640, 3712


def _dot(a, b):
    return jnp.dot(a.astype(BF16), b.astype(BF16), preferred_element_type=F32)


def _dot_nt(a, b):
    return lax.dot_general(a.astype(BF16), b.astype(BF16), (((1,), (1,)), ((), ())),
                           preferred_element_type=F32)


def _dot_tn(a, b):
    return lax.dot_general(a.astype(BF16), b.astype(BF16), (((0,), (0,)), ((), ())),
                           preferred_element_type=F32)


def _group_sums(xs, ones_bd, coarse=()):
    hi = [x.astype(BF16) for x in xs]
    lo = [(x - h.astype(F32)).astype(BF16) for x, h in zip(xs, hi)]
    terms = hi + [x.astype(BF16) for x in coarse] + lo
    s = jnp.dot(jnp.concatenate(terms, axis=0), ones_bd.astype(BF16), preferred_element_type=F32)
    offs = np.cumsum([0] + [t.shape[0] for t in terms])
    nx, nc = len(xs), len(coarse)
    out = [s[offs[i]:offs[i + 1]] + s[offs[nx + nc + i]:offs[nx + nc + i + 1]] for i in range(nx)]
    return out + [s[offs[nx + i]:offs[nx + i + 1]] for i in range(nc)]


def _cumsum_rows(x):
    n = x.shape[0]
    row = _iota(x.shape, 0)
    s = 1
    while s < n:
        x = x + jnp.where(row >= s, pltpu.roll(x, s, 0), 0.0)
        s *= 2
    return x


def _sigmoid(x):
    return 1.0 / (1.0 + jnp.exp(-x))


def _softplus(x):
    return jnp.maximum(x, 0.0) + jnp.log(1.0 + jnp.exp(-jnp.abs(x)))


def _iota(shape, axis):
    return lax.broadcasted_iota(jnp.int32, shape, axis)


def _bd_rows(x, groups):
    c, w = x.shape
    n = w // groups
    t = jnp.concatenate([x] * groups, axis=0)
    keep = (_iota(t.shape, 0) // c) == (_iota(t.shape, 1) // n)
    return jnp.where(keep, t, 0.0)


def _bd_dot(x, y, groups, nt=False):
    e = _bd_rows(y, groups)
    return _dot_nt(x, e) if nt else _dot(x, e)


def _block_mask(rows, cols, rblk, cblk):
    return (_iota((rows, cols), 0) // rblk) == (_iota((rows, cols), 1) // cblk)


def _const_spec(shape):
    nd = len(shape)
    return pl.BlockSpec(shape, lambda *_: (0,) * nd, pipeline_mode=pl.Buffered(1))


def _layer_spec(shape, layer):
    nd = len(shape)
    return pl.BlockSpec((None,) + tuple(shape[1:]), lambda *_: (layer,) + (0,) * (nd - 1),
                        pipeline_mode=pl.Buffered(1))


def _adaln_kernel(c_ref, w_ref, b_ref, o_ref):
    c = c_ref[...]
    cond = c * _sigmoid(c)
    o_ref[0] = _dot(cond, w_ref[0]) + b_ref[0]


def _adaln(c, ada_w, ada_b):
    L, D, D6 = ada_w.shape
    B = c.shape[0]
    tn = 1536
    return pl.pallas_call(
        _adaln_kernel,
        grid=(L, D6 // tn),
        in_specs=[pl.BlockSpec((B, D), lambda l, j: (0, 0)),
                  pl.BlockSpec((1, D, tn), lambda l, j: (l, 0, j)),
                  pl.BlockSpec((1, 1, tn), lambda l, j: (l, 0, j))],
        out_specs=pl.BlockSpec((1, B, tn), lambda l, j: (l, 0, j)),
        out_shape=jax.ShapeDtypeStruct((L, B, D6), F32),
        compiler_params=pltpu.CompilerParams(
            dimension_semantics=("arbitrary", "arbitrary"), vmem_limit_bytes=VMEM_LIMIT_BYTES),
        name="adaln_mod",
    )(c, ada_w, ada_b.reshape(L, 1, D6))


def _prep_kernel(tiles_per_seq, has_vres, *refs):
    if has_vres:
        (x_ref, xh_ref, mod_ref, n1g_ref, win_ref, w2_ref, ga2_ref, gab_ref, vec_ref, obd_ref,
         cos_ref, sin_ref, v2_ref, vf_ref, rwp_ref, gr_ref) = refs
    else:
        (x_ref, xh_ref, mod_ref, n1g_ref, win_ref, w2_ref, ga2_ref, gab_ref, vec_ref, obd_ref,
         cos_ref, sin_ref, rwp_ref, gr_ref) = refs
    tm = x_ref.shape[0]
    rw = RWKV_WIDTH
    first = (pl.program_id(0) % tiles_per_seq) == 0

    xe = jnp.concatenate([xh_ref[...], x_ref[...]], axis=0)
    mod = mod_ref[0]
    ms = jnp.mean(xe * xe, axis=-1, keepdims=True)
    he = xe * lax.rsqrt(ms + EPS) * (n1g_ref[...] * (1.0 + mod[1:2])) + mod[0:1]
    he = jnp.concatenate([jnp.where(first, 0.0, he[:8]), he[8:]], axis=0)

    heb = he.astype(BF16)

    def project(lo, hi):
        return jnp.dot(heb, win_ref[:, lo:hi], preferred_element_type=F32)

    def prev_rows(t):
        return pltpu.roll(t, 1, 0)[8:]

    p_lora = project(_W_MISC, _W_IN)
    p_rkv = project(_W_IN, _W_IN + 3 * rw)
    vec = vec_ref[...]

    pre = p_lora[8:, _W_LORA_H:_W_LORA_S] + prev_rows(p_lora[:, _W_LORA_S:_W_IN])
    pre_wa, pre_g = pre[:, :128], pre[:, 128:]
    act = jnp.concatenate([jnp.where(_iota(pre_wa.shape, 1) < 64, jnp.tanh(pre_wa), pre_wa),
                           _sigmoid(pre_g)], axis=1)
    second = _dot(act, w2_ref[...])
    p_gla = project(_W_IN + 3 * rw, _W_IN + 3 * rw + 768)
    lw = -float(np.exp(-0.5)) * _sigmoid(vec[3:4] + second[:, 0:rw])
    a = _sigmoid(vec[4:5] + second[:, rw:2 * rw])
    g = second[:, 2 * rw:3 * rw]

    p = p_rkv[8:]
    ps = prev_rows(p_rkv)
    r = p[:, 0:rw]
    r = r + (ps[:, 0:rw] - r) * vec[0:1]
    k = p[:, rw:2 * rw]
    k = k + (ps[:, rw:2 * rw] - k) * vec[1:2]
    v = p[:, 2 * rw:3 * rw]
    v = v + (ps[:, 2 * rw:3 * rw] - v) * vec[2:3]

    kk = k * vec[6:7]
    kk2 = kk * kk
    ss = jnp.concatenate(_group_sums([], obd_ref[...], coarse=[kk2[:, :GROUP_W], kk2[:, GROUP_W:]]), axis=1)
    kk = kk * lax.rsqrt(jnp.maximum(ss, 1e-24))
    k = k * ((1.0 - vec[7:8]) + a * vec[7:8])
    misc = p_lora[8:, _W_MISC:_W_LORA_H]
    if has_vres:
        mv = jnp.where(_iota(misc.shape, 1) < 64, misc, prev_rows(p_lora[:, _W_MISC:_W_LORA_H]))
        v = v + (vf_ref[0] - v) * _sigmoid(vec[5:6] + _dot(mv, v2_ref[...]))
    p_ret = project(_W_IN + 3 * rw + 768, _WIN_W)

    rwp_ref[0] = r
    rwp_ref[1] = lw
    rwp_ref[2] = k
    rwp_ref[3] = v
    rwp_ref[4] = kk
    rwp_ref[5] = kk * a
    rwp_ref[6] = g

    p = p_gla[8:]
    gr_ref[:, _GQ:_GK] = p[:, 0:128] * (GLA_DK ** -0.5)
    gr_ref[:, _GK:_GLA] = p[:, 128:256]
    la_pre = _dot(misc, ga2_ref[...]) + gab_ref[...]
    gr_ref[:, _GLA:_GV] = -_softplus(-la_pre) * (1.0 / GLA_GATE_TAU)
    gr_ref[:, _GV:_GG] = p[:, 256:512]
    gate = p[:, 512:768]
    gr_ref[:, _GG:_RQ] = gate * _sigmoid(gate)

    p = p_ret[8:]
    cos = cos_ref[...]
    sin = sin_ref[...]
    lo_half = (_iota((tm, 128), 1) % RET_DK) < (RET_DK // 2)

    def rope(t):
        swapped = jnp.where(lo_half, pltpu.roll(t, 128 - RET_DK // 2, 1), pltpu.roll(t, RET_DK // 2, 1))
        return t * cos + swapped * sin

    gr_ref[:, _RQ:_RK] = rope(p[:, 0:128]) * (RET_DK ** -0.5)
    gr_ref[:, _RK:_RV] = rope(p[:, 128:256])
    gr_ref[:, _RV:_RG] = p[:, 256:512]
    gate = p[:, 512:768]
    gr_ref[:, _RG:_GR_W] = gate * _sigmoid(gate)


def _prep(x2d, mod, layer, seq, tm, has_vres, n1g, win, w2, ga2, gab, vec, obd, cos_t, sin_t,
          v2=None, rwp_first=None):
    T, D = x2d.shape
    tiles_per_seq = seq // tm
    n_tiles = T // tm
    per_layer = [n1g, win, w2, ga2, gab, vec]
    in_specs = [pl.BlockSpec((tm, D), lambda i: (i, 0)),
                pl.BlockSpec((8, D), lambda i: (jnp.maximum(i * (tm // 8) - 1, 0), 0)),
                pl.BlockSpec((None, 1, 6, D), lambda i: (layer, i // tiles_per_seq, 0, 0))]
    in_specs += [_layer_spec(a.shape, layer) for a in per_layer]
    in_specs += [_const_spec(obd.shape),
                 pl.BlockSpec((tm, 128), lambda i: (i % tiles_per_seq, 0)),
                 pl.BlockSpec((tm, 128), lambda i: (i % tiles_per_seq, 0))]
    args = [x2d, x2d, mod] + per_layer + [obd, cos_t, sin_t]
    if has_vres:
        in_specs += [_layer_spec(v2.shape, layer),
                     pl.BlockSpec((1, tm, RWKV_WIDTH), lambda i: (3, i, 0))]
        args += [v2, rwp_first]
    return pl.pallas_call(
        functools.partial(_prep_kernel, tiles_per_seq, has_vres),
        grid=(n_tiles,),
        in_specs=in_specs,
        out_specs=[pl.BlockSpec((7, tm, RWKV_WIDTH), lambda i: (0, i, 0)),
                   pl.BlockSpec((tm, _GR_W), lambda i: (i, 0))],
        out_shape=[jax.ShapeDtypeStruct((7, T, RWKV_WIDTH), F32),
                   jax.ShapeDtypeStruct((T, _GR_W), F32)],
        compiler_params=pltpu.CompilerParams(
            dimension_semantics=("parallel",), vmem_limit_bytes=VMEM_LIMIT_BYTES),
        name="proj_prep",
    )(*args)


def _chunk_masks(groups):
    C = CHUNK
    row = _iota((C, groups * C), 0)
    col = _iota((C, groups * C), 1) % C
    return row > col, row >= col, (row == col).astype(F32)


def _rwkv_chunk_scaled(r, lw, k, v, kk, kb):
    C = CHUNK
    cl = _cumsum_rows(lw)
    cle = cl[C - 1:C]
    e_neg = jnp.exp(-cl)
    e_end = jnp.exp(cle - cl)
    at = -kk * jnp.exp(cl - lw)
    rt = r * jnp.exp(cl)
    return dict(at=at, rt=rt, ar=jnp.concatenate([at, rt], axis=0), bt=kb * e_neg, kt=k * e_neg,
                bk_end=jnp.concatenate([kb * e_end, k * e_end], axis=0), g_end=jnp.exp(cle), v=v)


def _rwkv_chunk_factors(units, masks):
    C, G = CHUNK, HEADS_PER_GROUP
    strict, incl, ident = masks
    n = range(len(units))
    at, rt, ar, v = ([u[name] for u in units] for name in ("at", "rt", "ar", "v"))
    sb = [_bd_dot(ar[i], units[i]["bt"], G, nt=True) for i in n]
    sk = [_bd_dot(ar[i], units[i]["kt"], G, nt=True) for i in n]
    yield
    a_rb = [jnp.where(incl, sb[i][C:], 0.0) for i in n]
    a_ak = [jnp.where(strict, sk[i][:C], 0.0) for i in n]
    a_rk = [jnp.where(incl, sk[i][C:], 0.0) for i in n]

    a_ab = [jnp.where(strict, sb[i][:C], 0.0) for i in n]
    row = _iota((C, G * C), 0)
    col = _iota((C, G * C), 1) % C
    bb = [jnp.where(row // INV_BASE_BLOCK == col // INV_BASE_BLOCK, a_ab[i], 0.0) for i in n]
    tm = [ident + bb[i] for i in n]
    xp = [_bd_dot(bb[i], bb[i], G) for i in n]
    yield
    rr = [_bd_dot(jnp.concatenate([tm[i], xp[i]], axis=0), xp[i], G) for i in n]
    tm = [tm[i] + rr[i][:C] for i in n]
    yield
    tm = [tm[i] + _bd_dot(tm[i], rr[i][C:], G) for i in n]
    yield
    xs = []
    s = INV_BASE_BLOCK
    while s < C:
        sub = (row // (2 * s) == col // (2 * s)) & (row % (2 * s) >= s) & (col % (2 * s) < s)
        xs.append([jnp.where(sub, a_ab[i], 0.0) for i in n])
        s *= 2
    rr = [_bd_dot(jnp.concatenate([x[i] for x in xs], axis=0), tm[i], G) for i in n]
    yield
    xt = [[rr[i][k * C:(k + 1) * C] for i in n] for k in range(len(xs))]
    while xt:
        rr = [_bd_dot(jnp.concatenate([tm[i]] + [z[i] for z in xt[1:]], axis=0), xt[0][i], G) for i in n]
        yield
        tm = [tm[i] + rr[i][:C] for i in n]
        xt = [[z[i] + rr[i][(k + 1) * C:(k + 2) * C] for i in n] for k, z in enumerate(xt[1:])]

    vv = [_bd_dot(jnp.concatenate([a_ak[i], a_rk[i]], axis=0), v[i], G) for i in n]
    yield
    tg = [_bd_dot(a_rb[i], tm[i], G) for i in n]
    yield
    tg = [jnp.concatenate([tm[i], tg[i]], axis=0) for i in n]
    wq = [_bd_dot(tg[i], at[i], G) for i in n]
    yield
    uy = [_bd_dot(tg[i], vv[i][:C], G) for i in n]
    yield
    wt = [wq[i][:C] for i in n]
    ut = [uy[i][:C] for i in n]
    qh = [rt[i] + wq[i][C:] for i in n]
    yh = [uy[i][C:] + vv[i][C:] for i in n]
    zero = jnp.zeros((C, 128), F32)
    left = _iota((C, 128), 1) < HEAD_DV
    pm, zm = [[] for _ in n], [[] for _ in n]
    for s in range(GROUP_W // 128):
        sl = slice(128 * s, 128 * (s + 1))
        pz = [_dot_tn(units[i]["bk_end"][:, sl],
                      jnp.concatenate([jnp.concatenate([wt[i][:, sl], ut[i][:, sl]], axis=1),
                                       jnp.concatenate([zero, v[i][:, sl]], axis=1)], axis=0)) for i in n]
        for i in n:
            pm[i].append(jnp.where(left, pz[i][:C, :128], pz[i][C:, :128]))
            zm[i].append(jnp.where(left, pz[i][:C, 128:], pz[i][C:, 128:]))
        yield
    pm = [jnp.concatenate(pm[i], axis=1) + jnp.where(ident > 0, units[i]["g_end"], 0.0) for i in n]
    zm = [jnp.concatenate(zm[i], axis=1) for i in n]
    return qh, yh, pm, zm


def _interleave(main, side):
    side_done, side_out = False, None
    while True:
        try:
            next(main)
        except StopIteration as stop:
            main_out = stop.value
            break
        if not side_done:
            try:
                next(side)
            except StopIteration as stop:
                side_done, side_out = True, stop.value
    while not side_done:
        try:
            next(side)
        except StopIteration as stop:
            side_done, side_out = True, stop.value
    return main_out, side_out


def _rwkv_kernel(cb, rwp_ref, par_ref, o_ref, st_ref):
    C, W, N = CHUNK, GROUP_W, HEAD_DV
    ngroups = o_ref.shape[1] // W

    @pl.when(pl.program_id(1) == 0)
    def _():
        st_ref[...] = jnp.zeros_like(st_ref)

    masks = _chunk_masks(HEADS_PER_GROUP)
    mean_bd = _block_mask(W, W, N, N).astype(F32) * (1.0 / N)
    m = [st_ref[gi] for gi in range(ngroups)]

    def wave_units(chunks):
        return [(slice(j * C, (j + 1) * C), slice(gi * W, (gi + 1) * W), gi)
                for j in chunks for gi in range(ngroups)]

    def factors(units):
        scaled = [_rwkv_chunk_scaled(*[rwp_ref[i, rows, lanes] for i in range(6)])
                  for rows, lanes, _ in units]
        return _rwkv_chunk_factors(scaled, masks)

    def chain(units, fac):
        qh, yh, pm, zm = fac
        y = []
        for u, (_, _, gi) in enumerate(units):
            ym = _dot(jnp.concatenate([qh[u], pm[u]], axis=0), _bd_rows(m[gi], HEADS_PER_GROUP))
            y.append(ym[:C] + yh[u])
            m[gi] = ym[C:] + zm[u]
            yield
        return y

    def epilogue(units, y):
        n = range(len(units))
        par = [par_ref[:, lanes] for _, lanes, _ in units]
        rkr = [rwp_ref[0, rows, lanes] * rwp_ref[2, rows, lanes] * (par[u][0:1] * N)
               for u, (rows, lanes, _) in enumerate(units)]
        sums = _group_sums(y + rkr, mean_bd)
        yield
        yc = [y[u] - sums[u] for u in n]
        var = _group_sums([], mean_bd, coarse=[yc[u] * yc[u] for u in n])
        yield
        for u, (rows, lanes, _) in enumerate(units):
            yn = yc[u] * lax.rsqrt(var[u] + RWKV_GN_EPS) * par[u][1:2] + par[u][2:3]
            o_ref[rows, lanes] = ((yn + sums[len(units) + u] * rwp_ref[3, rows, lanes])
                                  * rwp_ref[6, rows, lanes])
            yield

    def in_turn(*gens):
        outs = []
        for g in gens:
            outs.append((yield from g))
        return outs

    nw = RWKV_WAVES_PER_STEP
    per_wave = cb // nw
    units = [wave_units(range(w * per_wave, (w + 1) * per_wave)) for w in range(nw)]
    fac, ys = [None] * nw, [None] * nw
    for k in range(nw):
        side = []
        if k >= 1:
            side.append(chain(units[k - 1], fac[k - 1]))
        if k >= 2:
            side.append(epilogue(units[k - 2], ys[k - 2]))
        fac[k], outs = _interleave(factors(units[k]), in_turn(*side))
        if k >= 1:
            ys[k - 1] = outs[0]
    drain = epilogue(units[nw - 2], ys[nw - 2]) if nw >= 2 else iter(())
    _, ys[nw - 1] = _interleave(drain, chain(units[nw - 1], fac[nw - 1]))
    _interleave(epilogue(units[nw - 1], ys[nw - 1]), iter(()))
    for gi in range(ngroups):
        st_ref[gi] = m[gi]


def _rwkv(rwp, par, batch, seq, cb):
    _, T, RW = rwp.shape
    steps = seq // (CHUNK * cb)
    return pl.pallas_call(
        functools.partial(_rwkv_kernel, cb),
        grid=(batch, steps),
        in_specs=[pl.BlockSpec((7, cb * CHUNK, RW), lambda b, c: (0, b * steps + c, 0)),
                  _const_spec(par.shape)],
        out_specs=pl.BlockSpec((cb * CHUNK, RW), lambda b, c: (b * steps + c, 0)),
        out_shape=jax.ShapeDtypeStruct((T, RW), F32),
        scratch_shapes=[pltpu.VMEM((RW // GROUP_W, HEAD_DV, GROUP_W), F32)],
        compiler_params=pltpu.CompilerParams(
            dimension_semantics=("parallel", "arbitrary"), vmem_limit_bytes=VMEM_LIMIT_BYTES),
        name="rwkv7_chunk",
    )(rwp, par)


def _glaret_kernel(cb, gr_ref, lng_ref, dec_ref, kdec_ref, qdec_ref, cdec_ref, o_ref, sg_ref, sr_ref):
    C, G, DV = CHUNK, GLA_HEADS, HEAD_DV
    WV = G * DV

    @pl.when(pl.program_id(1) == 0)
    def _():
        sg_ref[...] = jnp.zeros_like(sg_ref)
        sr_ref[...] = jnp.zeros_like(sr_ref)

    _, incl, _ = _chunk_masks(G)
    mean_bd = _block_mask(WV, WV, DV, DV).astype(F32) * (1.0 / DV)
    st_mask = _block_mask(WV, G * GLA_DK, DV, GLA_DK)
    lng = lng_ref[...]
    dec = dec_ref[...]
    kdec = kdec_ref[...]
    qdec = qdec_ref[...]
    cdec = cdec_ref[...]

    n = range(cb)
    rows = [slice(j * C, (j + 1) * C) for j in n]
    gq = [gr_ref[rows[j], _GQ:_GK] for j in n]
    gk = [gr_ref[rows[j], _GK:_GLA] for j in n]
    gv = [gr_ref[rows[j], _GV:_GG] for j in n]
    bc = [_cumsum_rows(gr_ref[rows[j], _GLA:_GV]) for j in n]
    be = [bc[j][C - 1:C] for j in n]
    mid = [bc[j][C // 2 - 1:C // 2] for j in n]
    ep = [jnp.exp(bc[j] - mid[j]) for j in n]
    en = [jnp.exp(mid[j] - bc[j]) for j in n]
    qp = [gq[j] * jnp.exp(bc[j]) for j in n]
    att_lo = [_dot_nt(gq[j] * ep[j], _bd_rows(gk[j] * en[j], G)) for j in n]
    att_hi = [_dot_nt(gq[j] * en[j], _bd_rows(gk[j] * ep[j], G)) for j in n]
    rq = [gr_ref[rows[j], _RQ:_RK] for j in n]
    rk = [gr_ref[rows[j], _RK:_RV] for j in n]
    rv = [gr_ref[rows[j], _RV:_RG] for j in n]
    sc = [_dot_nt(rq[j], _bd_rows(rk[j], G)) * dec for j in n]
    g_intra = [_bd_dot(jnp.where(incl, att_lo[j], att_hi[j]), gv[j], G) for j in n]
    r_intra = [_bd_dot(sc[j], rv[j], G) for j in n]
    g_kv = [jnp.where(st_mask, _dot_tn(gv[j], gk[j] * jnp.exp(be[j] - bc[j])), 0.0) for j in n]
    r_kv = [jnp.where(st_mask, _dot_tn(rv[j], rk[j] * kdec), 0.0) for j in n]

    sg = sg_ref[...]
    sr = sr_ref[...]
    g_o = []
    r_o = []
    for j in n:
        g_o.append(g_intra[j] + _dot_nt(qp[j], sg))
        sg = sg * jnp.exp(be[j]) + g_kv[j]
        r_o.append(r_intra[j] + _dot_nt(rq[j] * qdec, sr))
        sr = sr * cdec + r_kv[j]
    sg_ref[...] = sg
    sr_ref[...] = sr

    means = _group_sums(r_o, mean_bd, coarse=[g_o[j] * g_o[j] for j in n])
    r_c = [r_o[j] - means[j] for j in n]
    r_var = _group_sums([], mean_bd, coarse=[r_c[j] * r_c[j] for j in n])
    for j in n:
        o_ref[rows[j], 0:WV] = (g_o[j] * lax.rsqrt(means[cb + j] + EPS) * lng
                                * gr_ref[rows[j], _GG:_RQ])
        o_ref[rows[j], WV:2 * WV] = (r_c[j] * lax.rsqrt(r_var[j] + EPS)
                                     * gr_ref[rows[j], _RG:_GR_W])


def _glaret(gr, lng, dec, kdec, qdec, cdec, batch, seq, cb):
    T = gr.shape[0]
    nc = seq // (CHUNK * cb)
    wv = GLA_HEADS * HEAD_DV
    consts = [lng, dec, kdec, qdec, cdec]
    return pl.pallas_call(
        functools.partial(_glaret_kernel, cb),
        grid=(batch, nc),
        in_specs=[pl.BlockSpec((cb * CHUNK, _GR_W), lambda b, c: (b * nc + c, 0))]
                 + [_const_spec(a.shape) for a in consts],
        out_specs=pl.BlockSpec((cb * CHUNK, 2 * wv), lambda b, c: (b * nc + c, 0)),
        out_shape=jax.ShapeDtypeStruct((T, 2 * wv), F32),
        scratch_shapes=[pltpu.VMEM((wv, GLA_HEADS * GLA_DK), F32),
                        pltpu.VMEM((wv, RET_HEADS * RET_DK), F32)],
        compiler_params=pltpu.CompilerParams(
            dimension_semantics=("parallel", "arbitrary"), vmem_limit_bytes=VMEM_LIMIT_BYTES),
        name="gla_retention_chunk",
    )(gr, *consts)


def _outffn_kernel(final_norm, x_ref, ya_ref, ybc_ref, mod_ref, n2g_ref, wo_ref, wg_ref, wu_ref,
                   wd_ref, nfg_ref, o_ref):
    mod = mod_ref[0]
    y = jnp.concatenate([ya_ref[...], ybc_ref[...]], axis=-1)
    x = x_ref[...] + mod[2:3] * _dot(y, wo_ref[...])
    ms = jnp.mean(x * x, axis=-1, keepdims=True)
    h = (x * lax.rsqrt(ms + EPS) * (n2g_ref[...] * (1.0 + mod[4:5])) + mod[3:4]).astype(BF16)
    gate = jnp.dot(h, wg_ref[...], preferred_element_type=F32)
    up = jnp.dot(h, wu_ref[...], preferred_element_type=F32)
    x = x + mod[5:6] * _dot(gate * _sigmoid(gate) * up, wd_ref[...])
    if final_norm:
        ms = jnp.mean(x * x, axis=-1, keepdims=True)
        x = x * lax.rsqrt(ms + EPS) * nfg_ref[...]
    o_ref[...] = x


def _outffn(x2d, ya, ybc, mod, layer, seq, tm, final_norm, n2g, wo, wg, wu, wd, nfg):
    T, D = x2d.shape
    tiles_per_seq = seq // tm
    per_layer = [n2g, wo, wg, wu, wd]
    return pl.pallas_call(
        functools.partial(_outffn_kernel, final_norm),
        grid=(T // tm,),
        in_specs=[pl.BlockSpec((tm, D), lambda i: (i, 0)),
                  pl.BlockSpec((tm, ya.shape[1]), lambda i: (i, 0)),
                  pl.BlockSpec((tm, ybc.shape[1]), lambda i: (i, 0)),
                  pl.BlockSpec((None, 1, 6, D), lambda i: (layer, i // tiles_per_seq, 0, 0))]
                 + [_layer_spec(a.shape, layer) for a in per_layer] + [_const_spec(nfg.shape)],
        out_specs=pl.BlockSpec((tm, D), lambda i: (i, 0)),
        out_shape=jax.ShapeDtypeStruct((T, D), F32),
        compiler_params=pltpu.CompilerParams(
            dimension_semantics=("parallel",), vmem_limit_bytes=VMEM_LIMIT_BYTES),
        name="outproj_swiglu",
    )(x2d, ya, ybc, mod, *per_layer, nfg)


def _rope_tables(seq):
    half = RET_DK // 2
    inv_freq = ROPE_BASE ** (-jnp.arange(half, dtype=F32) / half)
    ang = jnp.arange(seq, dtype=F32)[:, None] * inv_freq[None, :]
    cos, sin = jnp.cos(ang), jnp.sin(ang)
    cos_t = jnp.tile(jnp.concatenate([cos, cos], axis=-1), (1, RET_HEADS))
    sin_t = jnp.tile(jnp.concatenate([-sin, sin], axis=-1), (1, RET_HEADS))
    return cos_t, sin_t


def _retention_tables():
    H, C = RET_HEADS, CHUNK
    log_gamma = jnp.log1p(-(2.0 ** (-5.0 - jnp.arange(H, dtype=F32))))
    pos = jnp.arange(C, dtype=F32)
    intra = jnp.exp(log_gamma[:, None, None] * jnp.abs(pos[:, None] - pos[None, :]))
    dec = jnp.transpose(intra, (1, 0, 2)).reshape(C, H * C)
    k_dec = jnp.exp(log_gamma[None, :] * (C - 1.0 - pos)[:, None])
    q_dec = jnp.exp(log_gamma[None, :] * (pos + 1.0)[:, None])
    chunk_dec = jnp.exp(log_gamma * C)
    kdec = jnp.repeat(k_dec, RET_DK, axis=1)
    qdec = jnp.repeat(q_dec, RET_DK, axis=1)
    cdec = jnp.repeat(chunk_dec, RET_DK)[None, :]
    return dec, kdec, qdec, cdec


def kernel(x, c, ada_w, ada_b, norm1_g, norm2_g, w_in, w_out, rk_mu_rkv, rk_mu_x, rk_w0, rk_w1, rk_w2, rk_a0, rk_a1, rk_a2, rk_g1, rk_g2, rk_k_k, rk_k_a, rk_r_k, rk_ln_g, rk_ln_b, rk_mu_v, rk_v0, rk_v1, rk_v2, gla_a1, gla_a2, gla_ab, gla_ln_g, ffn_w_gate, ffn_w_up, ffn_w_down, norm_f_g):
    B, S, D = x.shape
    L = ada_w.shape[0]
    T = B * S
    assert S % PREP_TOKENS_PER_STEP == 0 and S % FFN_TOKENS_PER_STEP == 0
    assert S % (CHUNK * RWKV_CHUNKS_PER_STEP) == 0 and S % (CHUNK * GLARET_CHUNKS_PER_STEP) == 0

    mod = _adaln(c, ada_w, ada_b).reshape(L, B, 6, D)
    cos_t, sin_t = _rope_tables(S)
    dec, kdec, qdec, cdec = _retention_tables()
    obd = _block_mask(GROUP_W, GROUP_W, HEAD_DV, HEAD_DV).astype(BF16)
    RW = RWKV_WIDTH

    def zeros(*shape):
        return jnp.zeros(shape, F32)

    mu_v = jnp.concatenate([zeros(1, D), rk_mu_v], axis=0)
    v0 = jnp.concatenate([zeros(1, RW), rk_v0], axis=0)
    v1 = jnp.concatenate([zeros(1, D, rk_v1.shape[2]), rk_v1], axis=0)
    v2 = jnp.concatenate([zeros(1, rk_v2.shape[1], RW), rk_v2], axis=0)

    def on_h(mu, w):
        return (1.0 - mu)[:, :, None] * w

    def on_prev(mu, w):
        return mu[:, :, None] * w

    mu = rk_mu_x
    misc = jnp.concatenate([gla_a1, zeros(L, D, 16), on_h(mu_v, v1), on_prev(mu_v, v1), zeros(L, D, 32)],
                           axis=2)
    lora_h = jnp.concatenate([on_h(mu[:, 0], rk_w1), on_h(mu[:, 1], rk_a1), on_h(mu[:, 2], rk_g1)], axis=2)
    lora_s = jnp.concatenate([on_prev(mu[:, 0], rk_w1), on_prev(mu[:, 1], rk_a1), on_prev(mu[:, 2], rk_g1)],
                             axis=2)
    small = jnp.concatenate([misc, lora_h, lora_s], axis=2).astype(BF16)
    win = jnp.concatenate([small, w_in.astype(BF16)], axis=2)
    assert win.shape[2] == _WIN_W and misc.shape[2] == _W_LORA_H - _W_MISC
    w2 = jnp.concatenate([
        jnp.concatenate([rk_w2, zeros(L, 64, 2 * RW)], axis=2),
        jnp.concatenate([zeros(L, 64, RW), rk_a2, zeros(L, 64, RW)], axis=2),
        jnp.concatenate([zeros(L, 128, 2 * RW), rk_g2], axis=2)], axis=1).astype(BF16)
    ga2 = jnp.concatenate([gla_a2, zeros(L, 128 - gla_a2.shape[1], gla_a2.shape[2])], axis=1).astype(BF16)
    v2p = jnp.concatenate([zeros(L, 32, RW), v2, v2, zeros(L, 32, RW)], axis=1).astype(BF16)
    vec = jnp.stack([rk_mu_rkv[:, 0], rk_mu_rkv[:, 1], rk_mu_rkv[:, 2], rk_w0, rk_a0, v0, rk_k_k, rk_k_a],
                    axis=1)
    par = jnp.stack([rk_r_k.reshape(L, RW), rk_ln_g, rk_ln_b], axis=1)
    lng = jnp.tile(gla_ln_g, (1, GLA_HEADS))[:, None, :]
    wo, wg, wu, wd = (w.astype(BF16) for w in (w_out, ffn_w_gate, ffn_w_up, ffn_w_down))

    x2d = x.reshape(T, D)
    rwp_first = None
    for l in range(L):
        has_vres = l > 0
        extra = dict(v2=v2p, rwp_first=rwp_first) if has_vres else {}
        rwp, gr = _prep(x2d, mod, l, S, PREP_TOKENS_PER_STEP, has_vres, norm1_g[:, None, :], win, w2, ga2,
                        gla_ab[:, None, :], vec, obd, cos_t, sin_t, **extra)
        if l == 0:
            rwp_first = rwp
        ya = _rwkv(rwp, par[l], B, S, RWKV_CHUNKS_PER_STEP)
        ybc = _glaret(gr, lng[l], dec, kdec, qdec, cdec, B, S, GLARET_CHUNKS_PER_STEP)
        x2d = _outffn(x2d, ya, ybc, mod, l, S, FFN_TOKENS_PER_STEP, l == L - 1, norm2_g[:, None, :],
                      wo, wg, wu, wd, norm_f_g[None])
    return x2d.reshape(B, S, D)
```

```python
import functools

import numpy as np
import jax
import jax.numpy as jnp
from jax import lax
from jax.experimental import pallas as pl
from jax.experimental.pallas import tpu as pltpu

F32 = jnp.float32
BF16 = jnp.bfloat16

CHUNK = 64
EPS = 1e-6
HEAD_DV = 64
RWKV_HEADS = 8
RWKV_WIDTH = RWKV_HEADS * HEAD_DV
RWKV_GN_EPS = 64e-5
GLA_HEADS = 4
GLA_DK = 32
GLA_GATE_TAU = 16.0
RET_HEADS = 4
RET_DK = 32
ROPE_BASE = 10000.0
HEADS_PER_GROUP = 4
GROUP_W = HEADS_PER_GROUP * HEAD_DV
VMEM_LIMIT_BYTES = 56 * 1024 * 1024
RWKV_CHUNKS_PER_STEP = 16
INV_BASE_BLOCK = 8
RWKV_WAVES_PER_STEP = 4
GLARET_CHUNKS_PER_STEP = 16
PREP_TOKENS_PER_STEP = 512
FFN_TOKENS_PER_STEP = 512
FFN_COLUMN_CHUNK = 256

_GQ, _GK, _GLA, _GV, _GG, _RQ, _RK, _RV, _RG, _GR_W = 0, 128, 256, 384, 640, 896, 1024, 1152, 1408, 1664
_W_MISC, _W_LORA_H, _W_LORA_S, _W_IN, _WIN_W = 0, 128, 384, 640, 3712


def _dot(a, b):
    return jnp.dot(a.astype(BF16), b.astype(BF16), preferred_element_type=F32)


def _dot_nt(a, b):
    return lax.dot_general(a.astype(BF16), b.astype(BF16), (((1,), (1,)), ((), ())),
                           preferred_element_type=F32)


def _dot_tn(a, b):
    return lax.dot_general(a.astype(BF16), b.astype(BF16), (((0,), (0,)), ((), ())),
                           preferred_element_type=F32)


def _group_sums(xs, ones_bd, coarse=()):
    hi = [x.astype(BF16) for x in xs]
    lo = [(x - h.astype(F32)).astype(BF16) for x, h in zip(xs, hi)]
    terms = hi + [x.astype(BF16) for x in coarse] + lo
    s = jnp.dot(jnp.concatenate(terms, axis=0), ones_bd.astype(BF16), preferred_element_type=F32)
    offs = np.cumsum([0] + [t.shape[0] for t in terms])
    nx, nc = len(xs), len(coarse)
    out = [s[offs[i]:offs[i + 1]] + s[offs[nx + nc + i]:offs[nx + nc + i + 1]] for i in range(nx)]
    return out + [s[offs[nx + i]:offs[nx + i + 1]] for i in range(nc)]


def _cumsum_rows(x):
    n = x.shape[0]
    row = _iota(x.shape, 0)
    s = 1
    while s < n:
        x = x + jnp.where(row >= s, pltpu.roll(x, s, 0), 0.0)
        s *= 2
    return x


def _sigmoid(x):
    return 1.0 / (1.0 + jnp.exp(-x))


def _softplus(x):
    return jnp.maximum(x, 0.0) + jnp.log(1.0 + jnp.exp(-jnp.abs(x)))


def _iota(shape, axis):
    return lax.broadcasted_iota(jnp.int32, shape, axis)


def _bd_rows(x, groups):
    c, w = x.shape
    n = w // groups
    t = jnp.concatenate([x] * groups, axis=0)
    keep = (_iota(t.shape, 0) // c) == (_iota(t.shape, 1) // n)
    return jnp.where(keep, t, 0.0)


def _bd_dot(x, y, groups, nt=False):
    e = _bd_rows(y, groups)
    return _dot_nt(x, e) if nt else _dot(x, e)


def _block_mask(rows, cols, rblk, cblk):
    return (_iota((rows, cols), 0) // rblk) == (_iota((rows, cols), 1) // cblk)


def _const_spec(shape):
    nd = len(shape)
    return pl.BlockSpec(shape, lambda *_: (0,) * nd, pipeline_mode=pl.Buffered(1))


def _layer_spec(shape, layer):
    nd = len(shape)
    return pl.BlockSpec((None,) + tuple(shape[1:]), lambda *_: (layer,) + (0,) * (nd - 1),
                        pipeline_mode=pl.Buffered(1))


def _adaln_kernel(c_ref, w_ref, b_ref, o_ref):
    c = c_ref[...]
    cond = c * _sigmoid(c)
    o_ref[0] = _dot(cond, w_ref[0]) + b_ref[0]


def _adaln(c, ada_w, ada_b):
    L, D, D6 = ada_w.shape
    B = c.shape[0]
    tn = 1536
    return pl.pallas_call(
        _adaln_kernel,
        grid=(L, D6 // tn),
        in_specs=[pl.BlockSpec((B, D), lambda l, j: (0, 0)),
                  pl.BlockSpec((1, D, tn), lambda l, j: (l, 0, j)),
                  pl.BlockSpec((1, 1, tn), lambda l, j: (l, 0, j))],
        out_specs=pl.BlockSpec((1, B, tn), lambda l, j: (l, 0, j)),
        out_shape=jax.ShapeDtypeStruct((L, B, D6), F32),
        compiler_params=pltpu.CompilerParams(
            dimension_semantics=("arbitrary", "arbitrary"), vmem_limit_bytes=VMEM_LIMIT_BYTES),
        name="adaln_mod",
    )(c, ada_w, ada_b.reshape(L, 1, D6))


def _prep_kernel(tiles_per_seq, has_vres, *refs):
    if has_vres:
        (x_ref, xh_ref, mod_ref, n1g_ref, win_ref, w2_ref, ga2_ref, gab_ref, vec_ref, obd_ref,
         cos_ref, sin_ref, v2_ref, vf_ref, rwp_ref, gr_ref) = refs
    else:
        (x_ref, xh_ref, mod_ref, n1g_ref, win_ref, w2_ref, ga2_ref, gab_ref, vec_ref, obd_ref,
         cos_ref, sin_ref, rwp_ref, gr_ref) = refs
    tm = x_ref.shape[0]
    rw = RWKV_WIDTH
    first = (pl.program_id(0) % tiles_per_seq) == 0

    xe = jnp.concatenate([xh_ref[...], x_ref[...]], axis=0)
    mod = mod_ref[0]
    ms = jnp.mean(xe * xe, axis=-1, keepdims=True)
    he = xe * lax.rsqrt(ms + EPS) * (n1g_ref[...] * (1.0 + mod[1:2])) + mod[0:1]
    he = jnp.concatenate([jnp.where(first, 0.0, he[:8]), he[8:]], axis=0)

    heb = he.astype(BF16)

    def project(lo, hi):
        return jnp.dot(heb, win_ref[:, lo:hi], preferred_element_type=F32)

    def prev_rows(t):
        return pltpu.roll(t, 1, 0)[8:]

    p_lora = project(_W_MISC, _W_IN)
    p_rkv = project(_W_IN, _W_IN + 3 * rw)
    vec = vec_ref[...]

    pre = p_lora[8:, _W_LORA_H:_W_LORA_S] + prev_rows(p_lora[:, _W_LORA_S:_W_IN])
    pre_wa, pre_g = pre[:, :128], pre[:, 128:]
    act = jnp.concatenate([jnp.where(_iota(pre_wa.shape, 1) < 64, jnp.tanh(pre_wa), pre_wa),
                           _sigmoid(pre_g)], axis=1)
    second = _dot(act, w2_ref[...])
    p_gla = project(_W_IN + 3 * rw, _W_IN + 3 * rw + 768)
    lw = -float(np.exp(-0.5)) * _sigmoid(vec[3:4] + second[:, 0:rw])
    a = _sigmoid(vec[4:5] + second[:, rw:2 * rw])
    g = second[:, 2 * rw:3 * rw]

    p = p_rkv[8:]
    ps = prev_rows(p_rkv)
    r = p[:, 0:rw]
    r = r + (ps[:, 0:rw] - r) * vec[0:1]
    k = p[:, rw:2 * rw]
    k = k + (ps[:, rw:2 * rw] - k) * vec[1:2]
    v = p[:, 2 * rw:3 * rw]
    v = v + (ps[:, 2 * rw:3 * rw] - v) * vec[2:3]

    kk = k * vec[6:7]
    kk2 = kk * kk
    ss = jnp.concatenate(_group_sums([], obd_ref[...], coarse=[kk2[:, :GROUP_W], kk2[:, GROUP_W:]]), axis=1)
    kk = kk * lax.rsqrt(jnp.maximum(ss, 1e-24))
    k = k * ((1.0 - vec[7:8]) + a * vec[7:8])
    misc = p_lora[8:, _W_MISC:_W_LORA_H]
    if has_vres:
        mv = jnp.where(_iota(misc.shape, 1) < 64, misc, prev_rows(p_lora[:, _W_MISC:_W_LORA_H]))
        v = v + (vf_ref[0] - v) * _sigmoid(vec[5:6] + _dot(mv, v2_ref[...]))
    p_ret = project(_W_IN + 3 * rw + 768, _WIN_W)

    rwp_ref[0] = r
    rwp_ref[1] = lw
    rwp_ref[2] = k
    rwp_ref[3] = v
    rwp_ref[4] = kk
    rwp_ref[5] = kk * a
    rwp_ref[6] = g

    p = p_gla[8:]
    gr_ref[:, _GQ:_GK] = p[:, 0:128] * (GLA_DK ** -0.5)
    gr_ref[:, _GK:_GLA] = p[:, 128:256]
    la_pre = _dot(misc, ga2_ref[...]) + gab_ref[...]
    gr_ref[:, _GLA:_GV] = -_softplus(-la_pre) * (1.0 / GLA_GATE_TAU)
    gr_ref[:, _GV:_GG] = p[:, 256:512]
    gate = p[:, 512:768]
    gr_ref[:, _GG:_RQ] = gate * _sigmoid(gate)

    p = p_ret[8:]
    cos = cos_ref[...]
    sin = sin_ref[...]
    lo_half = (_iota((tm, 128), 1) % RET_DK) < (RET_DK // 2)

    def rope(t):
        swapped = jnp.where(lo_half, pltpu.roll(t, 128 - RET_DK // 2, 1), pltpu.roll(t, RET_DK // 2, 1))
        return t * cos + swapped * sin

    gr_ref[:, _RQ:_RK] = rope(p[:, 0:128]) * (RET_DK ** -0.5)
    gr_ref[:, _RK:_RV] = rope(p[:, 128:256])
    gr_ref[:, _RV:_RG] = p[:, 256:512]
    gate = p[:, 512:768]
    gr_ref[:, _RG:_GR_W] = gate * _sigmoid(gate)


def _prep(x2d, mod, layer, seq, tm, has_vres, n1g, win, w2, ga2, gab, vec, obd, cos_t, sin_t,
          v2=None, rwp_first=None):
    T, D = x2d.shape
    tiles_per_seq = seq // tm
    n_tiles = T // tm
    per_layer = [n1g, win, w2, ga2, gab, vec]
    in_specs = [pl.BlockSpec((tm, D), lambda i: (i, 0)),
                pl.BlockSpec((8, D), lambda i: (jnp.maximum(i * (tm // 8) - 1, 0), 0)),
                pl.BlockSpec((None, 1, 6, D), lambda i: (layer, i // tiles_per_seq, 0, 0))]
    in_specs += [_layer_spec(a.shape, layer) for a in per_layer]
    in_specs += [_const_spec(obd.shape),
                 pl.BlockSpec((tm, 128), lambda i: (i % tiles_per_seq, 0)),
                 pl.BlockSpec((tm, 128), lambda i: (i % tiles_per_seq, 0))]
    args = [x2d, x2d, mod] + per_layer + [obd, cos_t, sin_t]
    if has_vres:
        in_specs += [_layer_spec(v2.shape, layer),
                     pl.BlockSpec((1, tm, RWKV_WIDTH), lambda i: (3, i, 0))]
        args += [v2, rwp_first]
    return pl.pallas_call(
        functools.partial(_prep_kernel, tiles_per_seq, has_vres),
        grid=(n_tiles,),
        in_specs=in_specs,
        out_specs=[pl.BlockSpec((7, tm, RWKV_WIDTH), lambda i: (0, i, 0)),
                   pl.BlockSpec((tm, _GR_W), lambda i: (i, 0))],
        out_shape=[jax.ShapeDtypeStruct((7, T, RWKV_WIDTH), F32),
                   jax.ShapeDtypeStruct((T, _GR_W), F32)],
        compiler_params=pltpu.CompilerParams(
            dimension_semantics=("parallel",), vmem_limit_bytes=VMEM_LIMIT_BYTES),
        name="proj_prep",
    )(*args)


def _chunk_masks(groups):
    C = CHUNK
    row = _iota((C, groups * C), 0)
    col = _iota((C, groups * C), 1) % C
    return row > col, row >= col, (row == col).astype(F32)


def _rwkv_chunk_scaled(r, lw, k, v, kk, kb):
    C = CHUNK
    cl = _cumsum_rows(lw)
    cle = cl[C - 1:C]
    e_neg = jnp.exp(-cl)
    e_end = jnp.exp(cle - cl)
    at = -kk * jnp.exp(cl - lw)
    rt = r * jnp.exp(cl)
    return dict(at=at, rt=rt, ar=jnp.concatenate([at, rt], axis=0), bt=kb * e_neg, kt=k * e_neg,
                bk_end=jnp.concatenate([kb * e_end, k * e_end], axis=0), g_end=jnp.exp(cle), v=v)


def _rwkv_chunk_factors(units, masks):
    C, G = CHUNK, HEADS_PER_GROUP
    strict, incl, ident = masks
    n = range(len(units))
    at, rt, ar, v = ([u[name] for u in units] for name in ("at", "rt", "ar", "v"))
    sb = [_bd_dot(ar[i], units[i]["bt"], G, nt=True) for i in n]
    sk = [_bd_dot(ar[i], units[i]["kt"], G, nt=True) for i in n]
    yield
    a_rb = [jnp.where(incl, sb[i][C:], 0.0) for i in n]
    a_ak = [jnp.where(strict, sk[i][:C], 0.0) for i in n]
    a_rk = [jnp.where(incl, sk[i][C:], 0.0) for i in n]

    a_ab = [jnp.where(strict, sb[i][:C], 0.0) for i in n]
    row = _iota((C, G * C), 0)
    col = _iota((C, G * C), 1) % C
    bb = [jnp.where(row // INV_BASE_BLOCK == col // INV_BASE_BLOCK, a_ab[i], 0.0) for i in n]
    tm = [ident + bb[i] for i in n]
    xp = [_bd_dot(bb[i], bb[i], G) for i in n]
    yield
    rr = [_bd_dot(jnp.concatenate([tm[i], xp[i]], axis=0), xp[i], G) for i in n]
    tm = [tm[i] + rr[i][:C] for i in n]
    yield
    tm = [tm[i] + _bd_dot(tm[i], rr[i][C:], G) for i in n]
    yield
    xs = []
    s = INV_BASE_BLOCK
    while s < C:
        sub = (row // (2 * s) == col // (2 * s)) & (row % (2 * s) >= s) & (col % (2 * s) < s)
        xs.append([jnp.where(sub, a_ab[i], 0.0) for i in n])
        s *= 2
    rr = [_bd_dot(jnp.concatenate([x[i] for x in xs], axis=0), tm[i], G) for i in n]
    yield
    xt = [[rr[i][k * C:(k + 1) * C] for i in n] for k in range(len(xs))]
    while xt:
        rr = [_bd_dot(jnp.concatenate([tm[i]] + [z[i] for z in xt[1:]], axis=0), xt[0][i], G) for i in n]
        yield
        tm = [tm[i] + rr[i][:C] for i in n]
        xt = [[z[i] + rr[i][(k + 1) * C:(k + 2) * C] for i in n] for k, z in enumerate(xt[1:])]

    vv = [_bd_dot(jnp.concatenate([a_ak[i], a_rk[i]], axis=0), v[i], G) for i in n]
    yield
    tg = [_bd_dot(a_rb[i], tm[i], G) for i in n]
    yield
    tg = [jnp.concatenate([tm[i], tg[i]], axis=0) for i in n]
    wq = [_bd_dot(tg[i], at[i], G) for i in n]
    yield
    uy = [_bd_dot(tg[i], vv[i][:C], G) for i in n]
    yield
    wt = [wq[i][:C] for i in n]
    ut = [uy[i][:C] for i in n]
    qh = [rt[i] + wq[i][C:] for i in n]
    yh = [uy[i][C:] + vv[i][C:] for i in n]
    zero = jnp.zeros((C, 128), F32)
    left = _iota((C, 128), 1) < HEAD_DV
    pm, zm = [[] for _ in n], [[] for _ in n]
    for s in range(GROUP_W // 128):
        sl = slice(128 * s, 128 * (s + 1))
        pz = [_dot_tn(units[i]["bk_end"][:, sl],
                      jnp.concatenate([jnp.concatenate([wt[i][:, sl], ut[i][:, sl]], axis=1),
                                       jnp.concatenate([zero, v[i][:, sl]], axis=1)], axis=0)) for i in n]
        for i in n:
            pm[i].append(jnp.where(left, pz[i][:C, :128], pz[i][C:, :128]))
            zm[i].append(jnp.where(left, pz[i][:C, 128:], pz[i][C:, 128:]))
        yield
    pm = [jnp.concatenate(pm[i], axis=1) + jnp.where(ident > 0, units[i]["g_end"], 0.0) for i in n]
    zm = [jnp.concatenate(zm[i], axis=1) for i in n]
    return qh, yh, pm, zm


def _interleave(main, side):
    side_done, side_out = False, None
    while True:
        try:
            next(main)
        except StopIteration as stop:
            main_out = stop.value
            break
        if not side_done:
            try:
                next(side)
            except StopIteration as stop:
                side_done, side_out = True, stop.value
    while not side_done:
        try:
            next(side)
        except StopIteration as stop:
            side_done, side_out = True, stop.value
    return main_out, side_out


def _rwkv_kernel(cb, rwp_ref, par_ref, o_ref, st_ref):
    C, W, N = CHUNK, GROUP_W, HEAD_DV
    ngroups = o_ref.shape[1] // W

    @pl.when(pl.program_id(1) == 0)
    def _():
        st_ref[...] = jnp.zeros_like(st_ref)

    masks = _chunk_masks(HEADS_PER_GROUP)
    mean_bd = _block_mask(W, W, N, N).astype(F32) * (1.0 / N)
    m = [st_ref[gi] for gi in range(ngroups)]

    def wave_units(chunks):
        return [(slice(j * C, (j + 1) * C), slice(gi * W, (gi + 1) * W), gi)
                for j in chunks for gi in range(ngroups)]

    def factors(units):
        scaled = [_rwkv_chunk_scaled(*[rwp_ref[i, rows, lanes] for i in range(6)])
                  for rows, lanes, _ in units]
        return _rwkv_chunk_factors(scaled, masks)

    def chain(units, fac):
        qh, yh, pm, zm = fac
        y = []
        for u, (_, _, gi) in enumerate(units):
            ym = _dot(jnp.concatenate([qh[u], pm[u]], axis=0), _bd_rows(m[gi], HEADS_PER_GROUP))
            y.append(ym[:C] + yh[u])
            m[gi] = ym[C:] + zm[u]
            yield
        return y

    def epilogue(units, y):
        n = range(len(units))
        par = [par_ref[:, lanes] for _, lanes, _ in units]
        rkr = [rwp_ref[0, rows, lanes] * rwp_ref[2, rows, lanes] * (par[u][0:1] * N)
               for u, (rows, lanes, _) in enumerate(units)]
        sums = _group_sums(y + rkr, mean_bd)
        yield
        yc = [y[u] - sums[u] for u in n]
        var = _group_sums([], mean_bd, coarse=[yc[u] * yc[u] for u in n])
        yield
        for u, (rows, lanes, _) in enumerate(units):
            yn = yc[u] * lax.rsqrt(var[u] + RWKV_GN_EPS) * par[u][1:2] + par[u][2:3]
            o_ref[rows, lanes] = ((yn + sums[len(units) + u] * rwp_ref[3, rows, lanes])
                                  * rwp_ref[6, rows, lanes])
            yield

    def in_turn(*gens):
        outs = []
        for g in gens:
            outs.append((yield from g))
        return outs

    nw = RWKV_WAVES_PER_STEP
    per_wave = cb // nw
    units = [wave_units(range(w * per_wave, (w + 1) * per_wave)) for w in range(nw)]
    fac, ys = [None] * nw, [None] * nw
    for k in range(nw):
        side = []
        if k >= 1:
            side.append(chain(units[k - 1], fac[k - 1]))
        if k >= 2:
            side.append(epilogue(units[k - 2], ys[k - 2]))
        fac[k], outs = _interleave(factors(units[k]), in_turn(*side))
        if k >= 1:
            ys[k - 1] = outs[0]
    drain = epilogue(units[nw - 2], ys[nw - 2]) if nw >= 2 else iter(())
    _, ys[nw - 1] = _interleave(drain, chain(units[nw - 1], fac[nw - 1]))
    _interleave(epilogue(units[nw - 1], ys[nw - 1]), iter(()))
    for gi in range(ngroups):
        st_ref[gi] = m[gi]


def _rwkv(rwp, par, batch, seq, cb):
    _, T, RW = rwp.shape
    steps = seq // (CHUNK * cb)
    return pl.pallas_call(
        functools.partial(_rwkv_kernel, cb),
        grid=(batch, steps),
        in_specs=[pl.BlockSpec((7, cb * CHUNK, RW), lambda b, c: (0, b * steps + c, 0)),
                  _const_spec(par.shape)],
        out_specs=pl.BlockSpec((cb * CHUNK, RW), lambda b, c: (b * steps + c, 0)),
        out_shape=jax.ShapeDtypeStruct((T, RW), F32),
        scratch_shapes=[pltpu.VMEM((RW // GROUP_W, HEAD_DV, GROUP_W), F32)],
        compiler_params=pltpu.CompilerParams(
            dimension_semantics=("parallel", "arbitrary"), vmem_limit_bytes=VMEM_LIMIT_BYTES),
        name="rwkv7_chunk",
    )(rwp, par)


def _glaret_kernel(cb, gr_ref, lng_ref, dec_ref, kdec_ref, qdec_ref, cdec_ref, o_ref, sg_ref, sr_ref):
    C, G, DV = CHUNK, GLA_HEADS, HEAD_DV
    WV = G * DV

    @pl.when(pl.program_id(1) == 0)
    def _():
        sg_ref[...] = jnp.zeros_like(sg_ref)
        sr_ref[...] = jnp.zeros_like(sr_ref)

    _, incl, _ = _chunk_masks(G)
    mean_bd = _block_mask(WV, WV, DV, DV).astype(F32) * (1.0 / DV)
    st_mask = _block_mask(WV, G * GLA_DK, DV, GLA_DK)
    lng = lng_ref[...]
    dec = dec_ref[...]
    kdec = kdec_ref[...]
    qdec = qdec_ref[...]
    cdec = cdec_ref[...]

    n = range(cb)
    rows = [slice(j * C, (j + 1) * C) for j in n]
    gq = [gr_ref[rows[j], _GQ:_GK] for j in n]
    gk = [gr_ref[rows[j], _GK:_GLA] for j in n]
    gv = [gr_ref[rows[j], _GV:_GG] for j in n]
    bc = [_cumsum_rows(gr_ref[rows[j], _GLA:_GV]) for j in n]
    be = [bc[j][C - 1:C] for j in n]
    mid = [bc[j][C // 2 - 1:C // 2] for j in n]
    ep = [jnp.exp(bc[j] - mid[j]) for j in n]
    en = [jnp.exp(mid[j] - bc[j]) for j in n]
    qp = [gq[j] * jnp.exp(bc[j]) for j in n]
    att_lo = [_dot_nt(gq[j] * ep[j], _bd_rows(gk[j] * en[j], G)) for j in n]
    att_hi = [_dot_nt(gq[j] * en[j], _bd_rows(gk[j] * ep[j], G)) for j in n]
    rq = [gr_ref[rows[j], _RQ:_RK] for j in n]
    rk = [gr_ref[rows[j], _RK:_RV] for j in n]
    rv = [gr_ref[rows[j], _RV:_RG] for j in n]
    sc = [_dot_nt(rq[j], _bd_rows(rk[j], G)) * dec for j in n]
    g_intra = [_bd_dot(jnp.where(incl, att_lo[j], att_hi[j]), gv[j], G) for j in n]
    r_intra = [_bd_dot(sc[j], rv[j], G) for j in n]
    g_kv = [jnp.where(st_mask, _dot_tn(gv[j], gk[j] * jnp.exp(be[j] - bc[j])), 0.0) for j in n]
    r_kv = [jnp.where(st_mask, _dot_tn(rv[j], rk[j] * kdec), 0.0) for j in n]

    sg = sg_ref[...]
    sr = sr_ref[...]
    g_o = []
    r_o = []
    for j in n:
        g_o.append(g_intra[j] + _dot_nt(qp[j], sg))
        sg = sg * jnp.exp(be[j]) + g_kv[j]
        r_o.append(r_intra[j] + _dot_nt(rq[j] * qdec, sr))
        sr = sr * cdec + r_kv[j]
    sg_ref[...] = sg
    sr_ref[...] = sr

    means = _group_sums(r_o, mean_bd, coarse=[g_o[j] * g_o[j] for j in n])
    r_c = [r_o[j] - means[j] for j in n]
    r_var = _group_sums([], mean_bd, coarse=[r_c[j] * r_c[j] for j in n])
    for j in n:
        o_ref[rows[j], 0:WV] = (g_o[j] * lax.rsqrt(means[cb + j] + EPS) * lng
                                * gr_ref[rows[j], _GG:_RQ])
        o_ref[rows[j], WV:2 * WV] = (r_c[j] * lax.rsqrt(r_var[j] + EPS)
                                     * gr_ref[rows[j], _RG:_GR_W])


def _glaret(gr, lng, dec, kdec, qdec, cdec, batch, seq, cb):
    T = gr.shape[0]
    nc = seq // (CHUNK * cb)
    wv = GLA_HEADS * HEAD_DV
    consts = [lng, dec, kdec, qdec, cdec]
    return pl.pallas_call(
        functools.partial(_glaret_kernel, cb),
        grid=(batch, nc),
        in_specs=[pl.BlockSpec((cb * CHUNK, _GR_W), lambda b, c: (b * nc + c, 0))]
                 + [_const_spec(a.shape) for a in consts],
        out_specs=pl.BlockSpec((cb * CHUNK, 2 * wv), lambda b, c: (b * nc + c, 0)),
        out_shape=jax.ShapeDtypeStruct((T, 2 * wv), F32),
        scratch_shapes=[pltpu.VMEM((wv, GLA_HEADS * GLA_DK), F32),
                        pltpu.VMEM((wv, RET_HEADS * RET_DK), F32)],
        compiler_params=pltpu.CompilerParams(
            dimension_semantics=("parallel", "arbitrary"), vmem_limit_bytes=VMEM_LIMIT_BYTES),
        name="gla_retention_chunk",
    )(gr, *consts)


def _outffn_kernel(final_norm, x_ref, ya_ref, ybc_ref, mod_ref, n2g_ref, wo_ref, wg_ref, wu_ref,
                   wd_ref, nfg_ref, o_ref):
    mod = mod_ref[0]
    y = jnp.concatenate([ya_ref[...], ybc_ref[...]], axis=-1)
    x = x_ref[...] + mod[2:3] * _dot(y, wo_ref[...])
    ms = jnp.mean(x * x, axis=-1, keepdims=True)
    h = (x * lax.rsqrt(ms + EPS) * (n2g_ref[...] * (1.0 + mod[4:5])) + mod[3:4]).astype(BF16)
    acts = []
    for c in range(0, wg_ref.shape[1], FFN_COLUMN_CHUNK):
        gate = jnp.dot(h, wg_ref[:, c:c + FFN_COLUMN_CHUNK], preferred_element_type=F32)
        up = jnp.dot(h, wu_ref[:, c:c + FFN_COLUMN_CHUNK], preferred_element_type=F32)
        acts.append((gate * _sigmoid(gate) * up).astype(BF16))
    x = x + mod[5:6] * jnp.dot(jnp.concatenate(acts, axis=1), wd_ref[...], preferred_element_type=F32)
    if final_norm:
        ms = jnp.mean(x * x, axis=-1, keepdims=True)
        x = x * lax.rsqrt(ms + EPS) * nfg_ref[...]
    o_ref[...] = x


def _outffn(x2d, ya, ybc, mod, layer, seq, tm, final_norm, n2g, wo, wg, wu, wd, nfg):
    T, D = x2d.shape
    tiles_per_seq = seq // tm
    per_layer = [n2g, wo, wg, wu, wd]
    return pl.pallas_call(
        functools.partial(_outffn_kernel, final_norm),
        grid=(T // tm,),
        in_specs=[pl.BlockSpec((tm, D), lambda i: (i, 0)),
                  pl.BlockSpec((tm, ya.shape[1]), lambda i: (i, 0)),
                  pl.BlockSpec((tm, ybc.shape[1]), lambda i: (i, 0)),
                  pl.BlockSpec((None, 1, 6, D), lambda i: (layer, i // tiles_per_seq, 0, 0))]
                 + [_layer_spec(a.shape, layer) for a in per_layer] + [_const_spec(nfg.shape)],
        out_specs=pl.BlockSpec((tm, D), lambda i: (i, 0)),
        out_shape=jax.ShapeDtypeStruct((T, D), F32),
        compiler_params=pltpu.CompilerParams(
            dimension_semantics=("parallel",), vmem_limit_bytes=VMEM_LIMIT_BYTES),
        name="outproj_swiglu",
    )(x2d, ya, ybc, mod, *per_layer, nfg)


def _rope_tables(seq):
    half = RET_DK // 2
    inv_freq = ROPE_BASE ** (-jnp.arange(half, dtype=F32) / half)
    ang = jnp.arange(seq, dtype=F32)[:, None] * inv_freq[None, :]
    cos, sin = jnp.cos(ang), jnp.sin(ang)
    cos_t = jnp.tile(jnp.concatenate([cos, cos], axis=-1), (1, RET_HEADS))
    sin_t = jnp.tile(jnp.concatenate([-sin, sin], axis=-1), (1, RET_HEADS))
    return cos_t, sin_t


def _retention_tables():
    H, C = RET_HEADS, CHUNK
    log_gamma = jnp.log1p(-(2.0 ** (-5.0 - jnp.arange(H, dtype=F32))))
    pos = jnp.arange(C, dtype=F32)
    intra = jnp.exp(log_gamma[:, None, None] * jnp.abs(pos[:, None] - pos[None, :]))
    dec = jnp.transpose(intra, (1, 0, 2)).reshape(C, H * C)
    k_dec = jnp.exp(log_gamma[None, :] * (C - 1.0 - pos)[:, None])
    q_dec = jnp.exp(log_gamma[None, :] * (pos + 1.0)[:, None])
    chunk_dec = jnp.exp(log_gamma * C)
    kdec = jnp.repeat(k_dec, RET_DK, axis=1)
    qdec = jnp.repeat(q_dec, RET_DK, axis=1)
    cdec = jnp.repeat(chunk_dec, RET_DK)[None, :]
    return dec, kdec, qdec, cdec


def kernel(x, c, ada_w, ada_b, norm1_g, norm2_g, w_in, w_out, rk_mu_rkv, rk_mu_x, rk_w0, rk_w1, rk_w2, rk_a0, rk_a1, rk_a2, rk_g1, rk_g2, rk_k_k, rk_k_a, rk_r_k, rk_ln_g, rk_ln_b, rk_mu_v, rk_v0, rk_v1, rk_v2, gla_a1, gla_a2, gla_ab, gla_ln_g, ffn_w_gate, ffn_w_up, ffn_w_down, norm_f_g):
    B, S, D = x.shape
    L = ada_w.shape[0]
    T = B * S
    assert S % PREP_TOKENS_PER_STEP == 0 and S % FFN_TOKENS_PER_STEP == 0
    assert S % (CHUNK * RWKV_CHUNKS_PER_STEP) == 0 and S % (CHUNK * GLARET_CHUNKS_PER_STEP) == 0

    mod = _adaln(c, ada_w, ada_b).reshape(L, B, 6, D)
    cos_t, sin_t = _rope_tables(S)
    dec, kdec, qdec, cdec = _retention_tables()
    obd = _block_mask(GROUP_W, GROUP_W, HEAD_DV, HEAD_DV).astype(BF16)
    RW = RWKV_WIDTH

    def zeros(*shape):
        return jnp.zeros(shape, F32)

    mu_v = jnp.concatenate([zeros(1, D), rk_mu_v], axis=0)
    v0 = jnp.concatenate([zeros(1, RW), rk_v0], axis=0)
    v1 = jnp.concatenate([zeros(1, D, rk_v1.shape[2]), rk_v1], axis=0)
    v2 = jnp.concatenate([zeros(1, rk_v2.shape[1], RW), rk_v2], axis=0)

    def on_h(mu, w):
        return (1.0 - mu)[:, :, None] * w

    def on_prev(mu, w):
        return mu[:, :, None] * w

    mu = rk_mu_x
    misc = jnp.concatenate([gla_a1, zeros(L, D, 16), on_h(mu_v, v1), on_prev(mu_v, v1), zeros(L, D, 32)],
                           axis=2)
    lora_h = jnp.concatenate([on_h(mu[:, 0], rk_w1), on_h(mu[:, 1], rk_a1), on_h(mu[:, 2], rk_g1)], axis=2)
    lora_s = jnp.concatenate([on_prev(mu[:, 0], rk_w1), on_prev(mu[:, 1], rk_a1), on_prev(mu[:, 2], rk_g1)],
                             axis=2)
    win = jnp.concatenate([t.astype(BF16) for t in (misc, lora_h, lora_s, w_in)], axis=2)
    assert win.shape[2] == _WIN_W and misc.shape[2] == _W_LORA_H - _W_MISC
    w2 = jnp.concatenate([
        jnp.concatenate([rk_w2, zeros(L, 64, 2 * RW)], axis=2),
        jnp.concatenate([zeros(L, 64, RW), rk_a2, zeros(L, 64, RW)], axis=2),
        jnp.concatenate([zeros(L, 128, 2 * RW), rk_g2], axis=2)], axis=1).astype(BF16)
    ga2 = jnp.concatenate([gla_a2, zeros(L, 128 - gla_a2.shape[1], gla_a2.shape[2])], axis=1).astype(BF16)
    v2p = jnp.concatenate([zeros(L, 32, RW), v2, v2, zeros(L, 32, RW)], axis=1).astype(BF16)
    vec = jnp.stack([rk_mu_rkv[:, 0], rk_mu_rkv[:, 1], rk_mu_rkv[:, 2], rk_w0, rk_a0, v0, rk_k_k, rk_k_a],
                    axis=1)
    par = jnp.stack([rk_r_k.reshape(L, RW), rk_ln_g, rk_ln_b], axis=1)
    lng = jnp.tile(gla_ln_g, (1, GLA_HEADS))[:, None, :]
    wo, wg, wu, wd = (w.astype(BF16) for w in (w_out, ffn_w_gate, ffn_w_up, ffn_w_down))

    x2d = x.reshape(T, D)
    rwp_first = None
    for l in range(L):
        has_vres = l > 0
        extra = dict(v2=v2p, rwp_first=rwp_first) if has_vres else {}
        rwp, gr = _prep(x2d, mod, l, S, PREP_TOKENS_PER_STEP, has_vres, norm1_g[:, None, :], win, w2, ga2,
                        gla_ab[:, None, :], vec, obd, cos_t, sin_t, **extra)
        if l == 0:
            rwp_first = rwp
        ya = _rwkv(rwp, par[l], B, S, RWKV_CHUNKS_PER_STEP)
        ybc = _glaret(gr, lng[l], dec, kdec, qdec, cdec, B, S, GLARET_CHUNKS_PER_STEP)
        x2d = _outffn(x2d, ya, ybc, mod, l, S, FFN_TOKENS_PER_STEP, l == L - 1, norm2_g[:, None, :],
                      wo, wg, wu, wd, norm_f_g[None])
    return x2d.reshape(B, S, D)
```
